```python
import math
import jax, jax.numpy as jnp
from jax import lax
import numpy as np

D_MODEL = 2048
BATCH = 16
SEQ = 2048
DEPTH = 1

RET_HEADS = 8
RET_DK = 128
RET_DV = 128
RET_WIDTH = RET_HEADS * RET_DV
RET_CHUNK = 128

NSA_HEADS = 8
NSA_GROUPS = 2
NSA_HD = 128
NSA_WIDTH = NSA_HEADS * NSA_HD
NSA_KV_WIDTH = NSA_GROUPS * NSA_HD
CMP_BLOCK = 32
CMP_STRIDE = 16
CMP_HIDDEN = 256
SEL_BLOCK = 64
SEL_TOPN = 16
SEL_QBLK = 32
WINDOW = 512
WIN_QBLK = 128
FORCE_BONUS = 1.0e4

ROPE_THETA = 10000.0
EPS = 1e-6
NEG = -1.0e30

IN_SPLITS = (RET_HEADS * RET_DK, RET_HEADS * RET_DK, RET_WIDTH, RET_WIDTH,
             NSA_WIDTH, NSA_KV_WIDTH, NSA_KV_WIDTH, NSA_KV_WIDTH, NSA_KV_WIDTH,
             NSA_KV_WIDTH, NSA_KV_WIDTH, NSA_WIDTH, NSA_HEADS * 3, D_MODEL, D_MODEL)
IN_WIDTH = sum(IN_SPLITS)

kernel_name = "gated_parallel_retention_nsa_block"


def rmsnorm(x, g):
    xf = x.astype(jnp.float32)
    y = xf * lax.rsqrt(jnp.mean(xf * xf, -1, keepdims=True) + EPS)
    return (y * g.astype(jnp.float32)).astype(x.dtype)


def rope_tables(pos, hd):
    inv = jnp.exp(jnp.arange(0, hd, 2, dtype=jnp.float32) * (-math.log(ROPE_THETA) / hd))
    ang = pos.astype(jnp.float32)[..., None] * inv
    ang = jnp.concatenate([ang, ang], -1)[:, None]
    return jnp.cos(ang), jnp.sin(ang)


def apply_rope(x, cos, sin):
    xf = x.astype(jnp.float32)
    half = xf.shape[-1] // 2
    rot = jnp.concatenate([-xf[..., half:], xf[..., :half]], -1)
    return (xf * cos + rot * sin).astype(x.dtype)


def heads(t, n):
    B, S, W = t.shape
    return t.reshape(B, S, n, W // n).transpose(0, 2, 1, 3)


def retention(q, k, v, g):
    f32 = jnp.float32
    B, H, S, DK = q.shape
    DV = v.shape[-1]
    C = RET_CHUNK
    N = S // C
    log_gamma = jnp.log1p(-jnp.exp2(-5.0 - jnp.arange(H, dtype=f32)))
    idx = jnp.arange(C, dtype=f32)
    diff = idx[:, None] - idx[None, :]
    dmask = jnp.where(diff >= 0, jnp.exp(log_gamma[:, None, None] * jnp.maximum(diff, 0.0)), 0.0)
    zeta = jnp.exp(log_gamma[:, None] * (C - 1 - idx))
    xi = jnp.exp(log_gamma[:, None] * (idx + 1))
    chunk_decay = jnp.exp(log_gamma * C)
    qc = q.astype(f32).reshape(B, H, N, C, DK)
    kc = k.astype(f32).reshape(B, H, N, C, DK) * (DK ** -0.5)
    vc = v.astype(f32).reshape(B, H, N, C, DV)
    sc = jnp.einsum('bhncd,bhnmd->bhncm', qc, kc) * dmask[None, :, None]
    inner = jnp.einsum('bhncm,bhnme->bhnce', sc, vc)
    kv = jnp.einsum('bhnmd,hm,bhnme->bhnde', kc, zeta, vc)

    def step(state, kv_n):
        return state * chunk_decay[None, :, None, None] + kv_n, state

    _, prev = lax.scan(step, jnp.zeros((B, H, DK, DV), f32), jnp.moveaxis(kv, 2, 0))
    prev = jnp.moveaxis(prev, 0, 2)
    cross = jnp.einsum('bhncd,bhnde->bhnce', qc, prev) * xi[None, :, None, :, None]
    o = (inner + cross).reshape(B, H, S, DV)
    mu = jnp.mean(o, -1, keepdims=True)
    var = jnp.mean(jnp.square(o - mu), -1, keepdims=True)
    o = (o - mu) * lax.rsqrt(var + EPS)
    o = o.transpose(0, 2, 1, 3).reshape(B, S, H * DV) * g.astype(f32)
    return o.astype(v.dtype)


def compress_blocks(kraw, cidx, pe, w1, w2):
    B, G, _, hd = kraw.shape
    blk = kraw[:, :, cidx] + pe
    flat = blk.reshape(B, G, cidx.shape[0], CMP_BLOCK * hd)
    return jax.nn.silu(flat @ w1) @ w2


def nsa_attention(q, k_cmp, v_cmp, k_sel, v_sel, k_win, v_win, gates, positions,
                  w_ck1, w_ck2, pe_ck, w_cv1, w_cv2, pe_cv):
    f32 = jnp.float32
    B, H, S, hd = q.shape
    G = k_sel.shape[1]
    R = H // G
    scale = hd ** -0.5
    qg = q.reshape(B, G, R, S, hd)
    t = jnp.arange(S)

    ncmp = (S - CMP_BLOCK) // CMP_STRIDE + 1
    cidx = jnp.arange(ncmp)[:, None] * CMP_STRIDE + jnp.arange(CMP_BLOCK)[None, :]
    cend = cidx[:, -1]
    kc = compress_blocks(k_cmp, cidx, pe_ck, w_ck1, w_ck2)
    vc = compress_blocks(v_cmp, cidx, pe_cv, w_cv1, w_cv2)
    ccos, csin = rope_tables(positions[:, cend], hd)
    kc = apply_rope(kc, ccos, csin)
    s_c = jnp.einsum('bgrtd,bgnd->bgrtn', qg, kc).astype(f32) * scale
    valid_c = cend[None, :] <= t[:, None]
    p_c = jax.nn.softmax(jnp.where(valid_c, s_c, NEG), -1) * valid_c.any(-1)[:, None]
    o_cmp = jnp.einsum('bgrtn,bgnd->bgrtd', p_c.astype(vc.dtype), vc)

    nsel = S // SEL_BLOCK
    topn = min(SEL_TOPN, nsel)
    cstart = jnp.arange(ncmp) * CMP_STRIDE
    sstart = jnp.arange(nsel) * SEL_BLOCK
    overlap = jnp.clip(jnp.minimum(cstart[:, None] + CMP_BLOCK, sstart[None, :] + SEL_BLOCK)
                       - jnp.maximum(cstart[:, None], sstart[None, :]), 0, None).astype(f32) / CMP_BLOCK
    imp = jnp.einsum('bgrtn,nj->bgtj', p_c, overlap)
    tblk = t // SEL_BLOCK
    j = jnp.arange(nsel)
    forced = (j[None, :] == 0) | (j[None, :] == tblk[:, None]) | (j[None, :] == tblk[:, None] - 1)
    causal_b = j[None, :] <= tblk[:, None]
    scores = jnp.where(causal_b, imp + FORCE_BONUS * forced, NEG)
    top_v, top_i = lax.top_k(scores, topn)
    top_ok = top_v > NEG * 0.5
    kb = k_sel.reshape(B, G, nsel, SEL_BLOCK, hd)
    vb = v_sel.reshape(B, G, nsel, SEL_BLOCK, hd)
    nqb = S // SEL_QBLK
    bi = jnp.arange(B)[:, None, None, None]
    gi = jnp.arange(G)[None, :, None, None]

    def sel_block(args):
        qb, ib, okb, tb = args
        kg = kb[bi, gi, ib]
        vg = vb[bi, gi, ib]
        s = jnp.einsum('bgrqd,bgqnkd->bgrqnk', qb, kg).astype(f32) * scale
        kpos = ib[..., None] * SEL_BLOCK + jnp.arange(SEL_BLOCK)
        m = okb[..., None] & (kpos <= tb[None, None, :, None, None])
        s = jnp.where(m[:, :, None], s, NEG)
        p = jax.nn.softmax(s.reshape(s.shape[:4] + (-1,)), -1).reshape(s.shape)
        return jnp.einsum('bgrqnk,bgqnkd->bgrqd', p.astype(vg.dtype), vg)

    xs = (jnp.moveaxis(qg.reshape(B, G, R, nqb, SEL_QBLK, hd), 3, 0),
          jnp.moveaxis(top_i.reshape(B, G, nqb, SEL_QBLK, topn), 2, 0),
          jnp.moveaxis(top_ok.reshape(B, G, nqb, SEL_QBLK, topn), 2, 0),
          t.reshape(nqb, SEL_QBLK))
    o_sel = jnp.moveaxis(lax.map(sel_block, xs), 0, 3).reshape(B, G, R, S, hd)

    nw = S // WIN_QBLK
    span = WINDOW + WIN_QBLK
    widx = jnp.arange(nw)[:, None] * WIN_QBLK + jnp.arange(span)[None, :]
    pad = ((0, 0), (0, 0), (WINDOW, 0), (0, 0))
    kw = jnp.pad(k_win, pad)[:, :, widx]
    vw = jnp.pad(v_win, pad)[:, :, widx]
    qw = qg.reshape(B, G, R, nw, WIN_QBLK, hd)
    s_w = jnp.einsum('bgrnqd,bgnkd->bgrnqk', qw, kw).astype(f32) * scale
    kpos = widx - WINDOW
    qpos = t.reshape(nw, WIN_QBLK)
    d = qpos[:, :, None] - kpos[:, None, :]
    m_w = (kpos[:, None, :] >= 0) & (d >= 0) & (d < WINDOW)
    p_w = jax.nn.softmax(jnp.where(m_w, s_w, NEG), -1)
    o_win = jnp.einsum('bgrnqk,bgnkd->bgrnqd', p_w.astype(vw.dtype), vw).reshape(B, G, R, S, hd)

    gt = gates.reshape(B, S, G, R, 3).transpose(0, 2, 3, 1, 4)
    o = gt[..., 0:1] * o_cmp + gt[..., 1:2] * o_sel + gt[..., 2:3] * o_win
    return o.transpose(0, 3, 1, 2, 4).reshape(B, S, H * hd)


def setup_inputs(seed: int = 0) -> dict:
    key = jax.random.key(seed)
    ks = jax.random.split(key, 18)
    f32 = jnp.float32
    nrm = lambda k, shape, s: jax.random.normal(k, shape, f32) * s
    x = jax.random.normal(ks[0], (BATCH, SEQ, D_MODEL), f32)
    c = jax.random.normal(ks[1], (BATCH, D_MODEL), f32)
    offset = jax.random.randint(ks[2], (BATCH, 1), 0, 4096, dtype=jnp.int32)
    positions = offset + jnp.arange(SEQ, dtype=jnp.int32)[None, :]
    return {
        "x": x,
        "c": c,
        "positions": positions,
        "w_ada": nrm(ks[3], (DEPTH, D_MODEL, 3 * D_MODEL), D_MODEL ** -0.5),
        "b_ada": nrm(ks[4], (DEPTH, 3 * D_MODEL), 0.01),
        "g_norm": 1.0 + nrm(ks[5], (DEPTH, D_MODEL), 0.02),
        "w_in": nrm(ks[6], (DEPTH, D_MODEL, IN_WIDTH), D_MODEL ** -0.5),
        "g_ret": 1.0 + nrm(ks[7], (DEPTH, RET_WIDTH), 0.02),
        "w_ck1": nrm(ks[8], (DEPTH, CMP_BLOCK * NSA_HD, CMP_HIDDEN), (CMP_BLOCK * NSA_HD) ** -0.5),
        "w_ck2": nrm(ks[9], (DEPTH, CMP_HIDDEN, NSA_HD), CMP_HIDDEN ** -0.5),
        "pe_ck": nrm(ks[10], (DEPTH, CMP_BLOCK, NSA_HD), 0.1),
        "w_cv1": nrm(ks[11], (DEPTH, CMP_BLOCK * NSA_HD, CMP_HIDDEN), (CMP_BLOCK * NSA_HD) ** -0.5),
        "w_cv2": nrm(ks[12], (DEPTH, CMP_HIDDEN, NSA_HD), CMP_HIDDEN ** -0.5),
        "pe_cv": nrm(ks[13], (DEPTH, CMP_BLOCK, NSA_HD), 0.1),
        "w_up_ret": nrm(ks[14], (DEPTH, RET_WIDTH, D_MODEL), RET_WIDTH ** -0.5),
        "w_up_nsa": nrm(ks[15], (DEPTH, NSA_WIDTH, D_MODEL), NSA_WIDTH ** -0.5),
        "w_out": nrm(ks[16], (DEPTH, D_MODEL, D_MODEL), D_MODEL ** -0.5),
        "g_final": 1.0 + nrm(ks[17], (D_MODEL,), 0.02),
    }


def reference(x, c, positions, w_ada, b_ada, g_norm, w_in, g_ret, w_ck1, w_ck2, pe_ck,
              w_cv1, w_cv2, pe_cv, w_up_ret, w_up_nsa, w_out, g_final):
    cos, sin = rope_tables(positions, NSA_HD)
    split_at = [int(v) for v in np.cumsum(IN_SPLITS)[:-1]]
    for l in range(DEPTH):
        mod = jax.nn.silu(c) @ w_ada[l] + b_ada[l]
        shift, scl, gate = jnp.split(mod, 3, -1)
        h = rmsnorm(x, g_norm[l]) * (1.0 + scl[:, None]) + shift[:, None]
        (r_q, r_k, r_v, r_g, n_q, n_kc, n_vc, n_ks, n_vs, n_kw, n_vw, n_g, n_bg,
         m_a, m_b) = jnp.split(h @ w_in[l], split_at, -1)
        y_ret = retention(apply_rope(heads(r_q, RET_HEADS), cos, sin),
                          apply_rope(heads(r_k, RET_HEADS), cos, sin),
                          heads(r_v, RET_HEADS), g_ret[l]) * jax.nn.silu(r_g)
        y_nsa = nsa_attention(apply_rope(heads(n_q, NSA_HEADS), cos, sin),
                              heads(n_kc, NSA_GROUPS), heads(n_vc, NSA_GROUPS),
                              apply_rope(heads(n_ks, NSA_GROUPS), cos, sin), heads(n_vs, NSA_GROUPS),
                              apply_rope(heads(n_kw, NSA_GROUPS), cos, sin), heads(n_vw, NSA_GROUPS),
                              jax.nn.sigmoid(n_bg), positions,
                              w_ck1[l], w_ck2[l], pe_ck[l], w_cv1[l], w_cv2[l], pe_cv[l]) * jax.nn.silu(n_g)
        merged = jax.nn.sigmoid(m_a) * (y_ret @ w_up_ret[l]) + jax.nn.sigmoid(m_b) * (y_nsa @ w_up_nsa[l])
        x = x + gate[:, None] * (merged @ w_out[l])
    return rmsnorm(x, g_final)
```

```python
import functools
import math

import jax
import jax.numpy as jnp
from jax import lax
from jax.experimental import pallas as pl
from jax.experimental.pallas import tpu as pltpu

F32 = jnp.float32
BF16 = jnp.bfloat16

HD = 128
RET_HEADS = 8
NSA_HEADS = 8
NSA_GROUPS = 2
NSA_REP = NSA_HEADS // NSA_GROUPS
CMP_BLOCK = 32
CMP_STRIDE = 16
CMP_HIDDEN = 256
SEL_BLOCK = 64
SEL_TOPN = 16
WINDOW = 512
FORCE_BONUS = 1.0e4
ROPE_THETA = 10000.0
EPS = 1e-6
NEG = -1.0e30

LANES = 128
PROJ_TN = 512
RET_CHUNK = 256
NSA_TQ = 256
VMEM_LIMIT = 56 * 1024 * 1024


def _sigmoid(v):
    return 1.0 / (1.0 + jnp.exp(-v))


def _dot(a, b):
    return jnp.dot(a, b, preferred_element_type=F32)


def _dot_nt(a, b):
    return lax.dot_general(a, b, (((1,), (1,)), ((), ())), preferred_element_type=F32)


def _dot_tn(a, b):
    return lax.dot_general(a, b, (((0,), (0,)), ((), ())), preferred_element_type=F32)


def _rope(v, cos, sin_signed):
    return v * cos + pltpu.roll(v, HD // 2, 1) * sin_signed


def _mod_kernel(c_ref, w_ref, b_ref, o_ref):
    c = c_ref[...]
    sc = c * _sigmoid(c)
    o_ref[...] = jnp.dot(sc, w_ref[...], preferred_element_type=F32,
                         precision=lax.Precision.HIGHEST) + b_ref[...]


def _mod(c, w_ada, b_ada):
    B, D = c.shape
    N = w_ada.shape[1]
    tn = math.gcd(N, 512)
    return pl.pallas_call(
        _mod_kernel,
        grid=(N // tn,),
        in_specs=[pl.BlockSpec((B, D), lambda j: (0, 0)),
                  pl.BlockSpec((D, tn), lambda j: (0, j)),
                  pl.BlockSpec((1, tn), lambda j: (0, j))],
        out_specs=pl.BlockSpec((B, tn), lambda j: (0, j)),
        out_shape=jax.ShapeDtypeStruct((B, N), F32),
        name="mod",
    )(c, w_ada, b_ada.reshape(1, N))


def _cs_kernel(pos_ref, inv_ref, cos_ref, sin_ref):
    ang = pos_ref[...].astype(F32) * inv_ref[...]
    lane = lax.broadcasted_iota(jnp.int32, (1, HD), 1)
    cos_ref[...] = jnp.cos(ang)
    sin_ref[...] = jnp.where(lane < HD // 2, -1.0, 1.0) * jnp.sin(ang)


def _rope_tables(pos_col, inv2):
    T = pos_col.shape[0]
    tm = 1024
    return pl.pallas_call(
        _cs_kernel,
        grid=(T // tm,),
        in_specs=[pl.BlockSpec((tm, 1), lambda i: (i, 0)),
                  pl.BlockSpec((1, HD), lambda i: (0, 0))],
        out_specs=[pl.BlockSpec((tm, HD), lambda i: (i, 0)),
                   pl.BlockSpec((tm, HD), lambda i: (i, 0))],
        out_shape=[jax.ShapeDtypeStruct((T, HD), F32)] * 2,
        name="cs",
    )(pos_col, inv2)


def _h_kernel(x_ref, g_ref, scl_ref, sh_ref, h_ref):
    x = x_ref[0]
    ms = jnp.mean(x * x, -1, keepdims=True)
    y = x * lax.rsqrt(ms + EPS) * g_ref[...]
    h_ref[0] = (y * (1.0 + scl_ref[0]) + sh_ref[0]).astype(BF16)


def _h(x, g_norm, scl, shift):
    B, S, D = x.shape
    ts = 512
    return pl.pallas_call(
        _h_kernel,
        grid=(B, S // ts),
        in_specs=[pl.BlockSpec((1, ts, D), lambda b, i: (b, i, 0)),
                  pl.BlockSpec((1, D), lambda b, i: (0, 0)),
                  pl.BlockSpec((1, 1, D), lambda b, i: (b, 0, 0)),
                  pl.BlockSpec((1, 1, D), lambda b, i: (b, 0, 0))],
        out_specs=pl.BlockSpec((1, ts, D), lambda b, i: (b, i, 0)),
        out_shape=jax.ShapeDtypeStruct((B, S, D), BF16),
        name="h",
    )(x, g_norm.reshape(1, D), scl.reshape(B, 1, D), shift.reshape(B, 1, D))


def _proj_kernel(h_ref, w_ref, cos_ref, sin_ref, o_ref, *, n_sig, n_rope, n_silu, scaled_lo, scaled_hi):
    j = pl.program_id(0)
    acc = _dot(h_ref[...], w_ref[...])

    @pl.when(j < n_sig)
    def _():
        o_ref[...] = _sigmoid(acc).astype(o_ref.dtype)

    @pl.when((j >= n_sig) & (j < n_sig + n_rope))
    def _():
        f = jnp.where((j >= scaled_lo) & (j < scaled_hi), HD ** -0.5, 1.0).astype(F32)
        cos = cos_ref[...] * f
        sin = sin_ref[...] * f
        for hh in range(PROJ_TN // HD):
            sl = slice(hh * HD, (hh + 1) * HD)
            o_ref[:, sl] = _rope(acc[:, sl], cos, sin).astype(o_ref.dtype)

    @pl.when((j >= n_sig + n_rope) & (j < n_sig + n_rope + n_silu))
    def _():
        o_ref[...] = (acc * _sigmoid(acc)).astype(o_ref.dtype)

    @pl.when(j >= n_sig + n_rope + n_silu)
    def _():
        o_ref[...] = acc.astype(o_ref.dtype)


def _proj(h2, w_main, cos, sin, *, n_sig, n_rope, n_silu, scaled_lo, scaled_hi, tm):
    T, D = h2.shape
    N = w_main.shape[1]
    n_rope_end = n_sig + n_rope

    def tab_map(j, i):
        return (jnp.where((j >= n_sig) & (j < n_rope_end), i, 0), 0)

    kern = functools.partial(_proj_kernel, n_sig=n_sig, n_rope=n_rope, n_silu=n_silu,
                             scaled_lo=scaled_lo, scaled_hi=scaled_hi)
    return pl.pallas_call(
        kern,
        grid=(N // PROJ_TN, T // tm),
        in_specs=[pl.BlockSpec((tm, D), lambda j, i: (i, 0)),
                  pl.BlockSpec((D, PROJ_TN), lambda j, i: (0, j)),
                  pl.BlockSpec((tm, HD), tab_map),
                  pl.BlockSpec((tm, HD), tab_map)],
        out_specs=pl.BlockSpec((tm, PROJ_TN), lambda j, i: (i, j)),
        out_shape=jax.ShapeDtypeStruct((T, N), BF16),
        compiler_params=pltpu.CompilerParams(
            dimension_semantics=("arbitrary", "arbitrary"), vmem_limit_bytes=VMEM_LIMIT),
        name="proj",
    )(h2, w_main, cos, sin)


def _projf_kernel(h_ref, w_ref, o_ref):
    o_ref[...] = _dot(h_ref[...], w_ref[...])


def _projf(h2, w_f, *, tm):
    T, D = h2.shape
    N = w_f.shape[1]
    return pl.pallas_call(
        _projf_kernel,
        grid=(T // tm,),
        in_specs=[pl.BlockSpec((tm, D), lambda i: (i, 0)),
                  pl.BlockSpec((D, N), lambda i: (0, 0))],
        out_specs=pl.BlockSpec((tm, N), lambda i: (i, 0)),
        out_shape=jax.ShapeDtypeStruct((T, N), F32),
        compiler_params=pltpu.CompilerParams(vmem_limit_bytes=VMEM_LIMIT),
        name="projf",
    )(h2, w_f)


def _ret_kernel(lg_ref, q_ref, k_ref, v_ref, rg_ref, g_ref, o_ref):
    S = q_ref.shape[1]
    C = RET_CHUNK
    lg = lg_ref[pl.program_id(1)]
    r = lax.broadcasted_iota(jnp.int32, (C, C), 0)
    c = lax.broadcasted_iota(jnp.int32, (C, C), 1)
    diff = (r - c).astype(F32)
    dmask = jnp.where(diff >= 0, jnp.exp(lg * jnp.maximum(diff, 0.0)), 0.0)
    rowf = lax.broadcasted_iota(jnp.int32, (C, HD), 0).astype(F32)
    zeta = jnp.exp(lg * (C - 1 - rowf))
    xi = jnp.exp(lg * (rowf + 1.0))
    decay = jnp.exp(jnp.full((HD, HD), lg * C, F32))
    g = g_ref[...]
    state = jnp.zeros((HD, HD), F32)
    for n in range(S // C):
        sl = slice(n * C, (n + 1) * C)
        q = q_ref[0, sl, :]
        k = k_ref[0, sl, :]
        v = v_ref[0, sl, :]
        sc = _dot_nt(q, k) * dmask
        o = _dot(sc.astype(BF16), v)
        if n > 0:
            o = o + _dot(q, state.astype(BF16)) * xi
        if n + 1 < S // C:
            kz = (k.astype(F32) * zeta).astype(BF16)
            state = state * decay + _dot_tn(kz, v)
        mu = jnp.mean(o, -1, keepdims=True)
        d = o - mu
        var = jnp.mean(d * d, -1, keepdims=True)
        on = d * lax.rsqrt(var + EPS)
        o_ref[0, sl, :] = (on * g * rg_ref[0, sl, :].astype(F32)).astype(o_ref.dtype)


def _retention(P3, log_gamma, g_ret, *, q_blk, k_blk, v_blk, rg_blk):
    B, S, _ = P3.shape
    H = RET_HEADS

    def col(off):
        return lambda b, h, lg: (b, 0, off + h)

    return pl.pallas_call(
        _ret_kernel,
        grid_spec=pltpu.PrefetchScalarGridSpec(
            num_scalar_prefetch=1,
            grid=(B, H),
            in_specs=[pl.BlockSpec((1, S, HD), col(q_blk)),
                      pl.BlockSpec((1, S, HD), col(k_blk)),
                      pl.BlockSpec((1, S, HD), col(v_blk)),
                      pl.BlockSpec((1, S, HD), col(rg_blk)),
                      pl.BlockSpec((1, HD), lambda b, h, lg: (0, h))],
            out_specs=pl.BlockSpec((1, S, HD), lambda b, h, lg: (b, 0, h)),
        ),
        out_shape=jax.ShapeDtypeStruct((B, S, H * HD), BF16),
        name="ret",
    )(log_gamma, P3, P3, P3, P3, g_ret.reshape(1, H * HD))


def _cmp_kernel(x_ref, pe_ref, w1_ref, w2_ref, cos_ref, sin_ref, o_ref):
    half = (CMP_BLOCK // 2) * HD
    x = x_ref[0, 0]
    xa = (x + pe_ref[0, 0:1, :]).astype(BF16)
    xb = (x + pe_ref[0, 1:2, :]).astype(BF16)
    a = _dot(xa, w1_ref[0, :half, :])
    b = _dot(xb, w1_ref[0, half:, :])
    nrow = a.shape[0]
    pre = a + pltpu.roll(b, nrow - 1, 0)
    hid = pre * _sigmoid(pre)
    out = _dot(hid.astype(BF16), w2_ref[0])
    roped = _rope(out, cos_ref[0], sin_ref[0])
    is_key = pl.program_id(1) < NSA_GROUPS
    o_ref[0, 0] = jnp.where(is_key, roped, out).astype(o_ref.dtype)


def _compress(xh, pe2, w1s, w2s, cos_c, sin_c):
    B, NC, M, W = xh.shape
    G = NSA_GROUPS
    return pl.pallas_call(
        _cmp_kernel,
        grid=(B, NC),
        in_specs=[pl.BlockSpec((1, 1, M, W), lambda b, c: (b, c, 0, 0)),
                  pl.BlockSpec((1, 2, W), lambda b, c: (c // G, 0, 0)),
                  pl.BlockSpec((1, 2 * W, CMP_HIDDEN), lambda b, c: (c // G, 0, 0)),
                  pl.BlockSpec((1, CMP_HIDDEN, HD), lambda b, c: (c // G, 0, 0)),
                  pl.BlockSpec((1, M, HD), lambda b, c: (b, 0, 0)),
                  pl.BlockSpec((1, M, HD), lambda b, c: (b, 0, 0))],
        out_specs=pl.BlockSpec((1, 1, M, HD), lambda b, c: (b, c, 0, 0)),
        out_shape=jax.ShapeDtypeStruct((B, NC, M, HD), BF16),
        name="cmp",
    )(xh, pe2, w1s, w2s, cos_c, sin_c)


def _flash_tile(q, k, v, mask, m_ref, l_ref, acc_ref):
    s = _dot_nt(q, k)
    if mask is not None:
        s = jnp.where(mask, s, NEG)
    m_prev = m_ref[...]
    m_new = jnp.maximum(m_prev, jnp.max(s, -1, keepdims=True))
    alpha = jnp.exp(m_prev - m_new)
    p = jnp.exp(s - m_new)
    l_ref[...] = alpha * l_ref[...] + jnp.sum(p, -1, keepdims=True)
    acc_ref[...] = alpha * acc_ref[...] + _dot(p.astype(BF16), v)
    m_ref[...] = m_new


def _flash_init(m_ref, l_ref, acc_ref):
    m_ref[...] = jnp.full(m_ref.shape, NEG, F32)
    l_ref[...] = jnp.zeros(l_ref.shape, F32)
    acc_ref[...] = jnp.zeros(acc_ref.shape, F32)


def _nsa_kernel(q_ref, ks_ref, vs_ref, kw_ref, vw_ref, kc_ref, vc_ref, gt_ref, ng_ref, o_ref,
                qa_ref, m_ref, l_ref, acc_ref, oc_ref, os_ref):
    TQ = q_ref.shape[1]
    TK = TQ
    R = NSA_REP
    RT = R * TQ
    NSEL = ks_ref.shape[1] // SEL_BLOCK
    qi = pl.program_id(2)
    t0 = qi * TQ

    for h in range(R):
        qa_ref[h * TQ:(h + 1) * TQ, 0:HD] = q_ref[0, :, h * HD:(h + 1) * HD]
    q4 = qa_ref[:, 0:HD]

    s = _dot_nt(q4, kc_ref[0, 0])
    trow = t0 + (lax.broadcasted_iota(jnp.int32, (RT, HD), 0) & (TQ - 1))
    ncol = lax.broadcasted_iota(jnp.int32, (RT, HD), 1)
    s = jnp.where(ncol * CMP_STRIDE + (CMP_BLOCK - 1) <= trow, s, NEG)
    mx = jnp.max(s, -1, keepdims=True)
    p = jnp.exp(s - mx)
    tcol = t0 + (lax.broadcasted_iota(jnp.int32, (RT, 1), 0) & (TQ - 1))
    any_valid = jnp.where(tcol >= CMP_BLOCK - 1, 1.0, 0.0)
    p = p * (any_valid / jnp.sum(p, -1, keepdims=True))
    oc_ref[...] = _dot(p.astype(BF16), vc_ref[0, 0])

    psum = p[0:TQ]
    for h in range(1, R):
        psum = psum + p[h * TQ:(h + 1) * TQ]
    p_hi = psum.astype(BF16)
    rem = psum - p_hi.astype(F32)
    p_mid = rem.astype(BF16)
    p_lo = (rem - p_mid.astype(F32)).astype(BF16)
    jj = lax.broadcasted_iota(jnp.int32, (NSEL, HD), 0)
    nn = lax.broadcasted_iota(jnp.int32, (NSEL, HD), 1)
    ov = jnp.maximum(jnp.minimum(nn * CMP_STRIDE + CMP_BLOCK, (jj + 1) * SEL_BLOCK)
                     - jnp.maximum(nn * CMP_STRIDE, jj * SEL_BLOCK), 0).astype(F32) * (1.0 / CMP_BLOCK)
    ov = ov.astype(BF16)
    imp = (_dot_nt(ov, p_lo) + _dot_nt(ov, p_mid)) + _dot_nt(ov, p_hi)

    j = lax.broadcasted_iota(jnp.int32, (NSEL, TQ), 0)
    tb = (t0 + lax.broadcasted_iota(jnp.int32, (NSEL, TQ), 1)) >> 6
    forced = (j == 0) | (j == tb) | (j == tb - 1)
    causal = j <= tb
    sc = jnp.where(causal, imp + jnp.where(forced, FORCE_BONUS, 0.0), NEG)
    rank = jnp.zeros((NSEL, TQ), F32)
    for i in range(NSEL):
        si = sc[i:i + 1, :]
        rank = rank + jnp.where(j > i, jnp.where(si >= sc, 1.0, 0.0), jnp.where(si > sc, 1.0, 0.0))
    sel = jnp.where((rank < SEL_TOPN) & causal, 1.0, 0.0)

    selpad = jnp.concatenate([sel, jnp.zeros((HD - NSEL, TQ), F32)], 0).astype(BF16)
    eye = jnp.where(lax.broadcasted_iota(jnp.int32, (TQ, TQ), 0)
                    == lax.broadcasted_iota(jnp.int32, (TQ, TQ), 1), 1.0, 0.0).astype(BF16)
    sel_t = _dot_nt(eye, selpad)
    lane = lax.broadcasted_iota(jnp.int32, (TQ, HD), 1)
    bias = jnp.where((lane < NSEL) & (sel_t < 0.5), NEG, 0.0).astype(BF16)
    for h in range(R):
        qa_ref[h * TQ:(h + 1) * TQ, HD:2 * HD] = bias
    q_aug = qa_ref[...]

    rr = lax.broadcasted_iota(jnp.int32, (RT, TK), 0) & (TQ - 1)
    cc = lax.broadcasted_iota(jnp.int32, (RT, TK), 1)

    def k_aug(off):
        blk = (off + lax.broadcasted_iota(jnp.int32, (TK, HD), 0)) >> 6
        onehot = jnp.where(blk == lax.broadcasted_iota(jnp.int32, (TK, HD), 1), 1.0, 0.0).astype(BF16)
        return jnp.concatenate([ks_ref[0, pl.ds(off, TK), :], onehot], 1)

    _flash_init(m_ref, l_ref, acc_ref)

    def sel_body(kt, carry):
        off = pl.multiple_of(kt * TK, TK)
        _flash_tile(q_aug, k_aug(off), vs_ref[0, pl.ds(off, TK), :], None, m_ref, l_ref, acc_ref)
        return carry

    lax.fori_loop(0, qi, sel_body, 0)
    off_d = pl.multiple_of(t0, TK)
    _flash_tile(q_aug, k_aug(off_d), vs_ref[0, pl.ds(off_d, TK), :], cc <= rr, m_ref, l_ref, acc_ref)
    os_ref[...] = acc_ref[...] / l_ref[...]

    _flash_init(m_ref, l_ref, acc_ref)

    @pl.when(qi >= 2)
    def _():
        off = pl.multiple_of(t0 - 2 * TK, TK)
        _flash_tile(q4, kw_ref[0, pl.ds(off, TK), :], vw_ref[0, pl.ds(off, TK), :], cc > rr,
                    m_ref, l_ref, acc_ref)

    @pl.when(qi >= 1)
    def _():
        off = pl.multiple_of(t0 - TK, TK)
        _flash_tile(q4, kw_ref[0, pl.ds(off, TK), :], vw_ref[0, pl.ds(off, TK), :], None,
                    m_ref, l_ref, acc_ref)

    _flash_tile(q4, kw_ref[0, pl.ds(off_d, TK), :], vw_ref[0, pl.ds(off_d, TK), :], cc <= rr,
                m_ref, l_ref, acc_ref)
    o_win = acc_ref[...] / l_ref[...]

    gt = _sigmoid(gt_ref[0])
    for h in range(R):
        rows = slice(h * TQ, (h + 1) * TQ)
        o = (gt[:, 3 * h:3 * h + 1] * oc_ref[rows, :]
             + gt[:, 3 * h + 1:3 * h + 2] * os_ref[rows, :]
             + gt[:, 3 * h + 2:3 * h + 3] * o_win[rows, :])
        o_ref[0, :, h * HD:(h + 1) * HD] = (o * ng_ref[0, :, h * HD:(h + 1) * HD].astype(F32)).astype(o_ref.dtype)


def _nsa(P3, F3, kcv, *, q_blk4, ks_blk, vs_blk, kw_blk, vw_blk, ng_blk4, gt_blk):
    B, S, _ = P3.shape
    G, R, TQ = NSA_GROUPS, NSA_REP, NSA_TQ
    assert WINDOW == 2 * TQ and SEL_BLOCK == 64 and S % TQ == 0
    M = kcv.shape[2]
    kv_spec = lambda blk: pl.BlockSpec((1, S, HD), lambda b, g, i: (b, 0, blk + g))
    return pl.pallas_call(
        _nsa_kernel,
        grid=(B, G, S // TQ),
        in_specs=[pl.BlockSpec((1, TQ, R * HD), lambda b, g, i: (b, i, q_blk4 + g)),
                  kv_spec(ks_blk), kv_spec(vs_blk), kv_spec(kw_blk), kv_spec(vw_blk),
                  pl.BlockSpec((1, 1, M, HD), lambda b, g, i: (b, g, 0, 0)),
                  pl.BlockSpec((1, 1, M, HD), lambda b, g, i: (b, G + g, 0, 0)),
                  pl.BlockSpec((1, TQ, LANES), lambda b, g, i: (b, i, gt_blk + g)),
                  pl.BlockSpec((1, TQ, R * HD), lambda b, g, i: (b, i, ng_blk4 + g))],
        out_specs=pl.BlockSpec((1, TQ, R * HD), lambda b, g, i: (b, i, g)),
        out_shape=jax.ShapeDtypeStruct((B, S, NSA_HEADS * HD), BF16),
        scratch_shapes=[pltpu.VMEM((R * TQ, 2 * HD), BF16),
                        pltpu.VMEM((R * TQ, 1), F32),
                        pltpu.VMEM((R * TQ, 1), F32),
                        pltpu.VMEM((R * TQ, HD), F32),
                        pltpu.VMEM((R * TQ, HD), F32),
                        pltpu.VMEM((R * TQ, HD), F32)],
        compiler_params=pltpu.CompilerParams(
            dimension_semantics=("arbitrary", "arbitrary", "arbitrary"), vmem_limit_bytes=VMEM_LIMIT),
        name="nsa",
    )(P3, P3, P3, P3, P3, kcv, kcv, F3, P3)


def _up_kernel(yr_ref, yn_ref, wr_ref, wn_ref, sa_ref, sb_ref, o_ref):
    a = _dot(yr_ref[...], wr_ref[...])
    b = _dot(yn_ref[...], wn_ref[...])
    o_ref[...] = (sa_ref[...].astype(F32) * a + sb_ref[...].astype(F32) * b).astype(o_ref.dtype)


def _up(y_ret, y_nsa, w_ur, w_un, P2, *, tm):
    T, W = y_ret.shape
    D = w_ur.shape[1]
    return pl.pallas_call(
        _up_kernel,
        grid=(T // tm,),
        in_specs=[pl.BlockSpec((tm, W), lambda i: (i, 0)),
                  pl.BlockSpec((tm, W), lambda i: (i, 0)),
                  pl.BlockSpec((W, D), lambda i: (0, 0)),
                  pl.BlockSpec((W, D), lambda i: (0, 0)),
                  pl.BlockSpec((tm, D), lambda i: (i, 0)),
                  pl.BlockSpec((tm, D), lambda i: (i, 1))],
        out_specs=pl.BlockSpec((tm, D), lambda i: (i, 0)),
        out_shape=jax.ShapeDtypeStruct((T, D), BF16),
        compiler_params=pltpu.CompilerParams(vmem_limit_bytes=VMEM_LIMIT),
        name="up",
    )(y_ret, y_nsa, w_ur, w_un, P2, P2)


def _out_kernel(m_ref, w_ref, x_ref, gate_ref, g_ref, o_ref, *, final_norm):
    y = x_ref[0] + gate_ref[0] * _dot(m_ref[0], w_ref[...])
    if final_norm:
        ms = jnp.mean(y * y, -1, keepdims=True)
        y = y * lax.rsqrt(ms + EPS) * g_ref[...]
    o_ref[0] = y


def _out(merged3, w_out, x, gate, g_final, *, ts, final_norm):
    B, S, D = x.shape
    return pl.pallas_call(
        functools.partial(_out_kernel, final_norm=final_norm),
        grid=(B, S // ts),
        in_specs=[pl.BlockSpec((1, ts, D), lambda b, i: (b, i, 0)),
                  pl.BlockSpec((D, D), lambda b, i: (0, 0)),
                  pl.BlockSpec((1, ts, D), lambda b, i: (b, i, 0)),
                  pl.BlockSpec((1, 1, D), lambda b, i: (b, 0, 0)),
                  pl.BlockSpec((1, D), lambda b, i: (0, 0))],
        out_specs=pl.BlockSpec((1, ts, D), lambda b, i: (b, i, 0)),
        out_shape=jax.ShapeDtypeStruct((B, S, D), F32),
        compiler_params=pltpu.CompilerParams(vmem_limit_bytes=VMEM_LIMIT),
        name="out",
    )(merged3, w_out, x, gate.reshape(B, 1, D), g_final.reshape(1, D))


def kernel(x, c, positions, w_ada, b_ada, g_norm, w_in, g_ret, w_ck1, w_ck2, pe_ck,
           w_cv1, w_cv2, pe_cv, w_up_ret, w_up_nsa, w_out, g_final):
    B, S, D = x.shape
    T = B * S
    depth = w_ada.shape[0]
    G = NSA_GROUPS
    RW = RET_HEADS * HD
    NW = NSA_HEADS * HD
    KVW = G * HD
    splits = (RW, RW, RW, RW, NW, KVW, KVW, KVW, KVW, KVW, KVW, NW, NSA_HEADS * 3, D, D)
    names = ("r_q", "r_k", "r_v", "r_g", "n_q", "n_kc", "n_vc", "n_ks", "n_vs", "n_kw", "n_vw",
             "n_g", "n_bg", "m_a", "m_b")
    starts = {}
    off = 0
    for nm, wd in zip(names, splits):
        starts[nm] = (off, wd)
        off += wd

    sig_cols = ("m_a", "m_b")
    rope_cols = ("r_q", "r_k", "n_q", "n_ks", "n_kw")
    silu_cols = ("r_g", "n_g")
    plain_cols = ("r_v", "n_vs", "n_vw")
    order = sig_cols + rope_cols + silu_cols + plain_cols
    blk = {}
    o128 = 0
    for nm in order:
        blk[nm] = o128
        o128 += starts[nm][1] // LANES
    per_tile = PROJ_TN // LANES
    width = lambda cols: sum(starts[nm][1] for nm in cols)
    n_sig = width(sig_cols) // PROJ_TN
    n_rope = width(rope_cols) // PROJ_TN
    n_silu = width(silu_cols) // PROJ_TN
    assert width(sig_cols) % PROJ_TN == 0 and width(rope_cols) % PROJ_TN == 0
    assert width(silu_cols) % PROJ_TN == 0 and width(plain_cols) % PROJ_TN == 0
    assert blk["n_q"] == blk["r_k"] + RW // LANES and blk["r_k"] % per_tile == 0
    scaled_lo = blk["r_k"] // per_tile
    scaled_hi = (blk["n_q"] + NW // LANES) // per_tile
    assert (blk["n_q"] + NW // LANES) % per_tile == 0

    inv = jnp.exp(jnp.arange(0, HD, 2, dtype=F32) * (-math.log(ROPE_THETA) / HD))
    inv2 = jnp.concatenate([inv, inv]).reshape(1, HD)
    cos, sin = _rope_tables(positions.reshape(T, 1), inv2)
    cend = jnp.arange(S // CMP_STRIDE) * CMP_STRIDE + (CMP_BLOCK - 1)
    cend = jnp.minimum(cend, S - 1)
    cos_c = cos.reshape(B, S, HD)[:, cend]
    sin_c = sin.reshape(B, S, HD)[:, cend]
    log_gamma = jnp.log1p(-jnp.exp2(-5.0 - jnp.arange(RET_HEADS, dtype=F32)))

    tm = min(1024, T)
    for l in range(depth):
        w = w_in[l]
        colsel = lambda nm: w[:, starts[nm][0]:starts[nm][0] + starts[nm][1]]
        w_main = jnp.concatenate([colsel(nm) for nm in order], 1).astype(BF16)
        bg = colsel("n_bg").reshape(D, G, NSA_REP * 3)
        bg = jnp.pad(bg, ((0, 0), (0, 0), (0, LANES - NSA_REP * 3))).reshape(D, G * LANES)
        w_f = jnp.concatenate([colsel("n_kc"), colsel("n_vc"), bg], 1).astype(BF16)

        mod = _mod(c, w_ada[l], b_ada[l])
        shift, scl, gate = mod[:, :D], mod[:, D:2 * D], mod[:, 2 * D:]
        h2 = _h(x, g_norm[l], scl, shift).reshape(T, D)

        P2 = _proj(h2, w_main, cos, sin, n_sig=n_sig, n_rope=n_rope, n_silu=n_silu,
                   scaled_lo=scaled_lo, scaled_hi=scaled_hi, tm=tm)
        F2 = _projf(h2, w_f, tm=tm)
        P3 = P2.reshape(B, S, -1)
        F3 = F2.reshape(B, S, -1)

        y_ret = _retention(P3, log_gamma, g_ret[l], q_blk=blk["r_q"], k_blk=blk["r_k"],
                           v_blk=blk["r_v"], rg_blk=blk["r_g"])

        xh = F3[:, :, :2 * KVW].reshape(B, S // CMP_STRIDE, CMP_STRIDE, 2 * G, HD)
        xh = xh.transpose(0, 3, 1, 2, 4).reshape(B, 2 * G, S // CMP_STRIDE, CMP_STRIDE * HD)
        half = CMP_BLOCK // 2
        pe2 = jnp.stack([pe_ck[l].reshape(2, half * HD), pe_cv[l].reshape(2, half * HD)])
        w1s = jnp.stack([w_ck1[l], w_cv1[l]]).astype(BF16)
        w2s = jnp.stack([w_ck2[l], w_cv2[l]]).astype(BF16)
        kcv = _compress(xh, pe2, w1s, w2s, cos_c, sin_c)

        y_nsa = _nsa(P3, F3, kcv, q_blk4=blk["n_q"] // NSA_REP, ks_blk=blk["n_ks"], vs_blk=blk["n_vs"],
                     kw_blk=blk["n_kw"], vw_blk=blk["n_vw"], ng_blk4=blk["n_g"] // NSA_REP,
                     gt_blk=2 * KVW // LANES)

        merged = _up(y_ret.reshape(T, RW), y_nsa.reshape(T, NW), w_up_ret[l].astype(BF16),
                     w_up_nsa[l].astype(BF16), P2, tm=min(512, T))
        x = _out(merged.reshape(B, S, D), w_out[l].astype(BF16), x, gate, g_final,
                 ts=min(256, S), final_norm=(l + 1 == depth))
    return x
```

```python
import functools
import math

import jax
import jax.numpy as jnp
from jax import lax
from jax.experimental import pallas as pl
from jax.experimental.pallas import tpu as pltpu

F32 = jnp.float32
BF16 = jnp.bfloat16

HD = 128
RET_HEADS = 8
NSA_HEADS = 8
NSA_GROUPS = 2
NSA_REP = NSA_HEADS // NSA_GROUPS
CMP_BLOCK = 32
CMP_STRIDE = 16
CMP_HIDDEN = 256
SEL_BLOCK = 64
SEL_TOPN = 16
WINDOW = 512
FORCE_BONUS = 1.0e4
ROPE_THETA = 10000.0
EPS = 1e-6
NEG = -1.0e30

LANES = 128
PROJ_TN = 512
RET_CHUNK = 256
NSA_TQ = 256
VMEM_LIMIT = 56 * 1024 * 1024


def _sigmoid(v):
    return 1.0 / (1.0 + jnp.exp(-v))


def _dot(a, b):
    return jnp.dot(a, b, preferred_element_type=F32)


def _dot_nt(a, b):
    return lax.dot_general(a, b, (((1,), (1,)), ((), ())), preferred_element_type=F32)


def _dot_tn(a, b):
    return lax.dot_general(a, b, (((0,), (0,)), ((), ())), preferred_element_type=F32)


def _rope(v, cos, sin_signed):
    return v * cos + pltpu.roll(v, HD // 2, 1) * sin_signed


def _mod_kernel(c_ref, w_ref, b_ref, o_ref):
    c = c_ref[...]
    sc = c * _sigmoid(c)
    o_ref[...] = jnp.dot(sc, w_ref[...], preferred_element_type=F32,
                         precision=lax.Precision.HIGHEST) + b_ref[...]


def _mod(c, w_ada, b_ada):
    B, D = c.shape
    N = w_ada.shape[1]
    tn = math.gcd(N, 512)
    return pl.pallas_call(
        _mod_kernel,
        grid=(N // tn,),
        in_specs=[pl.BlockSpec((B, D), lambda j: (0, 0)),
                  pl.BlockSpec((D, tn), lambda j: (0, j)),
                  pl.BlockSpec((1, tn), lambda j: (0, j))],
        out_specs=pl.BlockSpec((B, tn), lambda j: (0, j)),
        out_shape=jax.ShapeDtypeStruct((B, N), F32),
        name="mod",
    )(c, w_ada, b_ada.reshape(1, N))


def _cs_kernel(pos_ref, inv_ref, cos_ref, sin_ref):
    ang = pos_ref[...].astype(F32) * inv_ref[...]
    lane = lax.broadcasted_iota(jnp.int32, (1, HD), 1)
    cos_ref[...] = jnp.cos(ang)
    sin_ref[...] = jnp.where(lane < HD // 2, -1.0, 1.0) * jnp.sin(ang)


def _rope_tables(pos_col, inv2):
    T = pos_col.shape[0]
    tm = 1024
    return pl.pallas_call(
        _cs_kernel,
        grid=(T // tm,),
        in_specs=[pl.BlockSpec((tm, 1), lambda i: (i, 0)),
                  pl.BlockSpec((1, HD), lambda i: (0, 0))],
        out_specs=[pl.BlockSpec((tm, HD), lambda i: (i, 0)),
                   pl.BlockSpec((tm, HD), lambda i: (i, 0))],
        out_shape=[jax.ShapeDtypeStruct((T, HD), F32)] * 2,
        name="cs",
    )(pos_col, inv2)


def _h_kernel(x_ref, g_ref, scl_ref, sh_ref, h_ref):
    x = x_ref[0]
    ms = jnp.mean(x * x, -1, keepdims=True)
    y = x * lax.rsqrt(ms + EPS) * g_ref[...]
    h_ref[0] = (y * (1.0 + scl_ref[0]) + sh_ref[0]).astype(BF16)


def _h(x, g_norm, scl, shift):
    B, S, D = x.shape
    ts = 512
    return pl.pallas_call(
        _h_kernel,
        grid=(B, S // ts),
        in_specs=[pl.BlockSpec((1, ts, D), lambda b, i: (b, i, 0)),
                  pl.BlockSpec((1, D), lambda b, i: (0, 0)),
                  pl.BlockSpec((1, 1, D), lambda b, i: (b, 0, 0)),
                  pl.BlockSpec((1, 1, D), lambda b, i: (b, 0, 0))],
        out_specs=pl.BlockSpec((1, ts, D), lambda b, i: (b, i, 0)),
        out_shape=jax.ShapeDtypeStruct((B, S, D), BF16),
        name="h",
    )(x, g_norm.reshape(1, D), scl.reshape(B, 1, D), shift.reshape(B, 1, D))


def _proj_kernel(h_ref, w_ref, cos_ref, sin_ref, o_ref, *, n_sig, n_rope, n_silu, scaled_lo, scaled_hi):
    j = pl.program_id(0)
    acc = _dot(h_ref[...], w_ref[...])

    @pl.when(j < n_sig)
    def _():
        o_ref[...] = _sigmoid(acc).astype(o_ref.dtype)

    @pl.when((j >= n_sig) & (j < n_sig + n_rope))
    def _():
        f = jnp.where((j >= scaled_lo) & (j < scaled_hi), HD ** -0.5, 1.0).astype(F32)
        cos = cos_ref[...] * f
        sin = sin_ref[...] * f
        for hh in range(PROJ_TN // HD):
            sl = slice(hh * HD, (hh + 1) * HD)
            o_ref[:, sl] = _rope(acc[:, sl], cos, sin).astype(o_ref.dtype)

    @pl.when((j >= n_sig + n_rope) & (j < n_sig + n_rope + n_silu))
    def _():
        o_ref[...] = (acc * _sigmoid(acc)).astype(o_ref.dtype)

    @pl.when(j >= n_sig + n_rope + n_silu)
    def _():
        o_ref[...] = acc.astype(o_ref.dtype)


def _proj(h2, w_main, cos, sin, *, n_sig, n_rope, n_silu, scaled_lo, scaled_hi, tm):
    T, D = h2.shape
    N = w_main.shape[1]
    n_rope_end = n_sig + n_rope

    def tab_map(j, i):
        return (jnp.where((j >= n_sig) & (j < n_rope_end), i, 0), 0)

    kern = functools.partial(_proj_kernel, n_sig=n_sig, n_rope=n_rope, n_silu=n_silu,
                             scaled_lo=scaled_lo, scaled_hi=scaled_hi)
    return pl.pallas_call(
        kern,
        grid=(N // PROJ_TN, T // tm),
        in_specs=[pl.BlockSpec((tm, D), lambda j, i: (i, 0)),
                  pl.BlockSpec((D, PROJ_TN), lambda j, i: (0, j)),
                  pl.BlockSpec((tm, HD), tab_map),
                  pl.BlockSpec((tm, HD), tab_map)],
        out_specs=pl.BlockSpec((tm, PROJ_TN), lambda j, i: (i, j)),
        out_shape=jax.ShapeDtypeStruct((T, N), BF16),
        compiler_params=pltpu.CompilerParams(
            dimension_semantics=("arbitrary", "arbitrary"), vmem_limit_bytes=VMEM_LIMIT),
        name="proj",
    )(h2, w_main, cos, sin)


def _projf_kernel(h_ref, w_ref, o_ref):
    o_ref[...] = _dot(h_ref[...], w_ref[...])


def _projf(h2, w_f, *, tm):
    T, D = h2.shape
    N = w_f.shape[1]
    return pl.pallas_call(
        _projf_kernel,
        grid=(T // tm,),
        in_specs=[pl.BlockSpec((tm, D), lambda i: (i, 0)),
                  pl.BlockSpec((D, N), lambda i: (0, 0))],
        out_specs=pl.BlockSpec((tm, N), lambda i: (i, 0)),
        out_shape=jax.ShapeDtypeStruct((T, N), F32),
        compiler_params=pltpu.CompilerParams(vmem_limit_bytes=VMEM_LIMIT),
        name="projf",
    )(h2, w_f)


def _ret_kernel(lg_ref, q_ref, k_ref, v_ref, rg_ref, g_ref, o_ref):
    S = q_ref.shape[1]
    C = RET_CHUNK
    lg = lg_ref[pl.program_id(1)]
    r = lax.broadcasted_iota(jnp.int32, (C, C), 0)
    c = lax.broadcasted_iota(jnp.int32, (C, C), 1)
    diff = (r - c).astype(F32)
    dmask = jnp.where(diff >= 0, jnp.exp(lg * jnp.maximum(diff, 0.0)), 0.0)
    rowf = lax.broadcasted_iota(jnp.int32, (C, HD), 0).astype(F32)
    zeta = jnp.exp(lg * (C - 1 - rowf))
    xi = jnp.exp(lg * (rowf + 1.0))
    decay = jnp.exp(jnp.full((HD, HD), lg * C, F32))
    g = g_ref[...]
    state = jnp.zeros((HD, HD), F32)
    for n in range(S // C):
        sl = slice(n * C, (n + 1) * C)
        q = q_ref[0, sl, :]
        k = k_ref[0, sl, :]
        v = v_ref[0, sl, :]
        sc = _dot_nt(q, k) * dmask
        o = _dot(sc.astype(BF16), v)
        if n > 0:
            o = o + _dot(q, state.astype(BF16)) * xi
        if n + 1 < S // C:
            kz = (k.astype(F32) * zeta).astype(BF16)
            state = state * decay + _dot_tn(kz, v)
        mu = jnp.mean(o, -1, keepdims=True)
        d = o - mu
        var = jnp.mean(d * d, -1, keepdims=True)
        on = d * lax.rsqrt(var + EPS)
        o_ref[0, sl, :] = (on * g * rg_ref[0, sl, :].astype(F32)).astype(o_ref.dtype)


def _retention(P3, log_gamma, g_ret, *, q_blk, k_blk, v_blk, rg_blk):
    B, S, _ = P3.shape
    H = RET_HEADS

    def col(off):
        return lambda b, h, lg: (b, 0, off + h)

    return pl.pallas_call(
        _ret_kernel,
        grid_spec=pltpu.PrefetchScalarGridSpec(
            num_scalar_prefetch=1,
            grid=(B, H),
            in_specs=[pl.BlockSpec((1, S, HD), col(q_blk)),
                      pl.BlockSpec((1, S, HD), col(k_blk)),
                      pl.BlockSpec((1, S, HD), col(v_blk)),
                      pl.BlockSpec((1, S, HD), col(rg_blk)),
                      pl.BlockSpec((1, HD), lambda b, h, lg: (0, h))],
            out_specs=pl.BlockSpec((1, S, HD), lambda b, h, lg: (b, 0, h)),
        ),
        out_shape=jax.ShapeDtypeStruct((B, S, H * HD), BF16),
        name="ret",
    )(log_gamma, P3, P3, P3, P3, g_ret.reshape(1, H * HD))


def _cmp_kernel(x_ref, pe_ref, w1_ref, w2_ref, cos_ref, sin_ref, o_ref):
    half = (CMP_BLOCK // 2) * HD
    x = x_ref[0, 0]
    xa = (x + pe_ref[0, 0:1, :]).astype(BF16)
    xb = (x + pe_ref[0, 1:2, :]).astype(BF16)
    a = _dot(xa, w1_ref[0, :half, :])
    b = _dot(xb, w1_ref[0, half:, :])
    nrow = a.shape[0]
    pre = a + pltpu.roll(b, nrow - 1, 0)
    hid = pre * _sigmoid(pre)
    out = _dot(hid.astype(BF16), w2_ref[0])
    roped = _rope(out, cos_ref[0], sin_ref[0])
    is_key = pl.program_id(1) < NSA_GROUPS
    o_ref[0, 0] = jnp.where(is_key, roped, out).astype(o_ref.dtype)


def _compress(xh, pe2, w1s, w2s, cos_c, sin_c):
    B, NC, M, W = xh.shape
    G = NSA_GROUPS
    return pl.pallas_call(
        _cmp_kernel,
        grid=(B, NC),
        in_specs=[pl.BlockSpec((1, 1, M, W), lambda b, c: (b, c, 0, 0)),
                  pl.BlockSpec((1, 2, W), lambda b, c: (c // G, 0, 0)),
                  pl.BlockSpec((1, 2 * W, CMP_HIDDEN), lambda b, c: (c // G, 0, 0)),
                  pl.BlockSpec((1, CMP_HIDDEN, HD), lambda b, c: (c // G, 0, 0)),
                  pl.BlockSpec((1, M, HD), lambda b, c: (b, 0, 0)),
                  pl.BlockSpec((1, M, HD), lambda b, c: (b, 0, 0))],
        out_specs=pl.BlockSpec((1, 1, M, HD), lambda b, c: (b, c, 0, 0)),
        out_shape=jax.ShapeDtypeStruct((B, NC, M, HD), BF16),
        name="cmp",
    )(xh, pe2, w1s, w2s, cos_c, sin_c)


def _flash_tile(q, k, v, mask, m_ref, l_ref, acc_ref):
    s = _dot_nt(q, k)
    if mask is not None:
        s = jnp.where(mask, s, NEG)
    m_prev = m_ref[...]
    m_new = jnp.maximum(m_prev, jnp.max(s, -1, keepdims=True))
    alpha = jnp.exp(m_prev - m_new)
    p = jnp.exp(s - pltpu.repeat(m_new, s.shape[1] // LANES, 1))
    l_ref[...] = alpha * l_ref[...] + jnp.sum(p, -1, keepdims=True)
    acc_ref[...] = alpha * acc_ref[...] + _dot(p.astype(BF16), v)
    m_ref[...] = m_new


def _flash_init(m_ref, l_ref, acc_ref):
    m_ref[...] = jnp.full(m_ref.shape, NEG, F32)
    l_ref[...] = jnp.zeros(l_ref.shape, F32)
    acc_ref[...] = jnp.zeros(acc_ref.shape, F32)


def _nsa_kernel(q_ref, ks_ref, vs_ref, kw_ref, vw_ref, kc_ref, vc_ref, gt_ref, ng_ref, o_ref,
                qa_ref, m_ref, l_ref, acc_ref, oc_ref, os_ref):
    TQ = q_ref.shape[1]
    TK = TQ
    R = NSA_REP
    RT = R * TQ
    NSEL = ks_ref.shape[1] // SEL_BLOCK
    qi = pl.program_id(2)
    t0 = qi * TQ

    for h in range(R):
        qa_ref[h * TQ:(h + 1) * TQ, 0:HD] = q_ref[0, :, h * HD:(h + 1) * HD]
    q4 = qa_ref[:, 0:HD]

    s = _dot_nt(q4, kc_ref[0, 0])
    trow = t0 + (lax.broadcasted_iota(jnp.int32, (RT, HD), 0) & (TQ - 1))
    ncol = lax.broadcasted_iota(jnp.int32, (RT, HD), 1)
    s = jnp.where(ncol * CMP_STRIDE + (CMP_BLOCK - 1) <= trow, s, NEG)
    mx = jnp.max(s, -1, keepdims=True)
    p = jnp.exp(s - mx)
    tcol = t0 + (lax.broadcasted_iota(jnp.int32, (RT, 1), 0) & (TQ - 1))
    any_valid = jnp.where(tcol >= CMP_BLOCK - 1, 1.0, 0.0)
    p = p * (any_valid / jnp.sum(p, -1, keepdims=True))
    oc_ref[...] = _dot(p.astype(BF16), vc_ref[0, 0])

    psum = p[0:TQ]
    for h in range(1, R):
        psum = psum + p[h * TQ:(h + 1) * TQ]
    p_hi = psum.astype(BF16)
    rem = psum - p_hi.astype(F32)
    p_mid = rem.astype(BF16)
    p_lo = (rem - p_mid.astype(F32)).astype(BF16)
    jj = lax.broadcasted_iota(jnp.int32, (NSEL, HD), 0)
    nn = lax.broadcasted_iota(jnp.int32, (NSEL, HD), 1)
    ov = jnp.maximum(jnp.minimum(nn * CMP_STRIDE + CMP_BLOCK, (jj + 1) * SEL_BLOCK)
                     - jnp.maximum(nn * CMP_STRIDE, jj * SEL_BLOCK), 0).astype(F32) * (1.0 / CMP_BLOCK)
    ov = ov.astype(BF16)
    imp = (_dot_nt(ov, p_lo) + _dot_nt(ov, p_mid)) + _dot_nt(ov, p_hi)

    j = lax.broadcasted_iota(jnp.int32, (NSEL, TQ), 0)
    tb = (t0 + lax.broadcasted_iota(jnp.int32, (NSEL, TQ), 1)) >> 6
    forced = (j == 0) | (j == tb) | (j == tb - 1)
    causal = j <= tb
    sc = jnp.where(causal, imp + jnp.where(forced, FORCE_BONUS, 0.0), NEG)
    rank = jnp.zeros((NSEL, TQ), F32)
    for i in range(NSEL):
        si = sc[i:i + 1, :]
        rank = rank + jnp.where(j > i, jnp.where(si >= sc, 1.0, 0.0), jnp.where(si > sc, 1.0, 0.0))
    sel = jnp.where((rank < SEL_TOPN) & causal, 1.0, 0.0)

    selpad = jnp.concatenate([sel, jnp.zeros((HD - NSEL, TQ), F32)], 0).astype(BF16)
    eye = jnp.where(lax.broadcasted_iota(jnp.int32, (TQ, TQ), 0)
                    == lax.broadcasted_iota(jnp.int32, (TQ, TQ), 1), 1.0, 0.0).astype(BF16)
    sel_t = _dot_nt(eye, selpad)
    lane = lax.broadcasted_iota(jnp.int32, (TQ, HD), 1)
    bias = jnp.where((lane < NSEL) & (sel_t < 0.5), NEG, 0.0).astype(BF16)
    for h in range(R):
        qa_ref[h * TQ:(h + 1) * TQ, HD:2 * HD] = bias
    q_aug = qa_ref[...]

    rr = lax.broadcasted_iota(jnp.int32, (RT, TK), 0) & (TQ - 1)
    cc = lax.broadcasted_iota(jnp.int32, (RT, TK), 1)

    def k_aug(off):
        blk = (off + lax.broadcasted_iota(jnp.int32, (TK, HD), 0)) >> 6
        onehot = jnp.where(blk == lax.broadcasted_iota(jnp.int32, (TK, HD), 1), 1.0, 0.0).astype(BF16)
        return jnp.concatenate([ks_ref[0, pl.ds(off, TK), :], onehot], 1)

    _flash_init(m_ref, l_ref, acc_ref)

    def sel_body(kt, carry):
        off = pl.multiple_of(kt * TK, TK)
        _flash_tile(q_aug, k_aug(off), vs_ref[0, pl.ds(off, TK), :], None, m_ref, l_ref, acc_ref)
        return carry

    lax.fori_loop(0, qi, sel_body, 0)
    off_d = pl.multiple_of(t0, TK)
    _flash_tile(q_aug, k_aug(off_d), vs_ref[0, pl.ds(off_d, TK), :], cc <= rr, m_ref, l_ref, acc_ref)
    os_ref[...] = acc_ref[...] / l_ref[...]

    _flash_init(m_ref, l_ref, acc_ref)

    @pl.when(qi >= 2)
    def _():
        off = pl.multiple_of(t0 - 2 * TK, TK)
        _flash_tile(q4, kw_ref[0, pl.ds(off, TK), :], vw_ref[0, pl.ds(off, TK), :], cc > rr,
                    m_ref, l_ref, acc_ref)

    @pl.when(qi >= 1)
    def _():
        off = pl.multiple_of(t0 - TK, TK)
        _flash_tile(q4, kw_ref[0, pl.ds(off, TK), :], vw_ref[0, pl.ds(off, TK), :], None,
                    m_ref, l_ref, acc_ref)

    _flash_tile(q4, kw_ref[0, pl.ds(off_d, TK), :], vw_ref[0, pl.ds(off_d, TK), :], cc <= rr,
                m_ref, l_ref, acc_ref)
    o_win = acc_ref[...] / l_ref[...]

    gt = _sigmoid(gt_ref[0])
    for h in range(R):
        rows = slice(h * TQ, (h + 1) * TQ)
        o = (gt[:, 3 * h:3 * h + 1] * oc_ref[rows, :]
             + gt[:, 3 * h + 1:3 * h + 2] * os_ref[rows, :]
             + gt[:, 3 * h + 2:3 * h + 3] * o_win[rows, :])
        o_ref[0, :, h * HD:(h + 1) * HD] = (o * ng_ref[0, :, h * HD:(h + 1) * HD].astype(F32)).astype(o_ref.dtype)


def _nsa(P3, F3, kcv, *, q_blk4, ks_blk, vs_blk, kw_blk, vw_blk, ng_blk4, gt_blk):
    B, S, _ = P3.shape
    G, R, TQ = NSA_GROUPS, NSA_REP, NSA_TQ
    assert WINDOW == 2 * TQ and SEL_BLOCK == 64 and S % TQ == 0
    M = kcv.shape[2]
    kv_spec = lambda blk: pl.BlockSpec((1, S, HD), lambda b, g, i: (b, 0, blk + g))
    return pl.pallas_call(
        _nsa_kernel,
        grid=(B, G, S // TQ),
        in_specs=[pl.BlockSpec((1, TQ, R * HD), lambda b, g, i: (b, i, q_blk4 + g)),
                  kv_spec(ks_blk), kv_spec(vs_blk), kv_spec(kw_blk), kv_spec(vw_blk),
                  pl.BlockSpec((1, 1, M, HD), lambda b, g, i: (b, g, 0, 0)),
                  pl.BlockSpec((1, 1, M, HD), lambda b, g, i: (b, G + g, 0, 0)),
                  pl.BlockSpec((1, TQ, LANES), lambda b, g, i: (b, i, gt_blk + g)),
                  pl.BlockSpec((1, TQ, R * HD), lambda b, g, i: (b, i, ng_blk4 + g))],
        out_specs=pl.BlockSpec((1, TQ, R * HD), lambda b, g, i: (b, i, g)),
        out_shape=jax.ShapeDtypeStruct((B, S, NSA_HEADS * HD), BF16),
        scratch_shapes=[pltpu.VMEM((R * TQ, 2 * HD), BF16),
                        pltpu.VMEM((R * TQ, LANES), F32),
                        pltpu.VMEM((R * TQ, LANES), F32),
                        pltpu.VMEM((R * TQ, HD), F32),
                        pltpu.VMEM((R * TQ, HD), F32),
                        pltpu.VMEM((R * TQ, HD), F32)],
        compiler_params=pltpu.CompilerParams(
            dimension_semantics=("arbitrary", "arbitrary", "arbitrary"), vmem_limit_bytes=VMEM_LIMIT),
        name="nsa",
    )(P3, P3, P3, P3, P3, kcv, kcv, F3, P3)


def _up_kernel(yr_ref, yn_ref, wr_ref, wn_ref, sa_ref, sb_ref, o_ref):
    a = _dot(yr_ref[...], wr_ref[...])
    b = _dot(yn_ref[...], wn_ref[...])
    o_ref[...] = (sa_ref[...].astype(F32) * a + sb_ref[...].astype(F32) * b).astype(o_ref.dtype)


def _up(y_ret, y_nsa, w_ur, w_un, P2, *, tm):
    T, W = y_ret.shape
    D = w_ur.shape[1]
    return pl.pallas_call(
        _up_kernel,
        grid=(T // tm,),
        in_specs=[pl.BlockSpec((tm, W), lambda i: (i, 0)),
                  pl.BlockSpec((tm, W), lambda i: (i, 0)),
                  pl.BlockSpec((W, D), lambda i: (0, 0)),
                  pl.BlockSpec((W, D), lambda i: (0, 0)),
                  pl.BlockSpec((tm, D), lambda i: (i, 0)),
                  pl.BlockSpec((tm, D), lambda i: (i, 1))],
        out_specs=pl.BlockSpec((tm, D), lambda i: (i, 0)),
        out_shape=jax.ShapeDtypeStruct((T, D), BF16),
        compiler_params=pltpu.CompilerParams(vmem_limit_bytes=VMEM_LIMIT),
        name="up",
    )(y_ret, y_nsa, w_ur, w_un, P2, P2)


def _out_kernel(m_ref, w_ref, x_ref, gate_ref, g_ref, o_ref, *, final_norm):
    y = x_ref[0] + gate_ref[0] * _dot(m_ref[0], w_ref[...])
    if final_norm:
        ms = jnp.mean(y * y, -1, keepdims=True)
        y = y * lax.rsqrt(ms + EPS) * g_ref[...]
    o_ref[0] = y


def _out(merged3, w_out, x, gate, g_final, *, ts, final_norm):
    B, S, D = x.shape
    return pl.pallas_call(
        functools.partial(_out_kernel, final_norm=final_norm),
        grid=(B, S // ts),
        in_specs=[pl.BlockSpec((1, ts, D), lambda b, i: (b, i, 0)),
                  pl.BlockSpec((D, D), lambda b, i: (0, 0)),
                  pl.BlockSpec((1, ts, D), lambda b, i: (b, i, 0)),
                  pl.BlockSpec((1, 1, D), lambda b, i: (b, 0, 0)),
                  pl.BlockSpec((1, D), lambda b, i: (0, 0))],
        out_specs=pl.BlockSpec((1, ts, D), lambda b, i: (b, i, 0)),
        out_shape=jax.ShapeDtypeStruct((B, S, D), F32),
        compiler_params=pltpu.CompilerParams(vmem_limit_bytes=VMEM_LIMIT),
        name="out",
    )(merged3, w_out, x, gate.reshape(B, 1, D), g_final.reshape(1, D))


def kernel(x, c, positions, w_ada, b_ada, g_norm, w_in, g_ret, w_ck1, w_ck2, pe_ck,
           w_cv1, w_cv2, pe_cv, w_up_ret, w_up_nsa, w_out, g_final):
    B, S, D = x.shape
    T = B * S
    depth = w_ada.shape[0]
    G = NSA_GROUPS
    RW = RET_HEADS * HD
    NW = NSA_HEADS * HD
    KVW = G * HD
    splits = (RW, RW, RW, RW, NW, KVW, KVW, KVW, KVW, KVW, KVW, NW, NSA_HEADS * 3, D, D)
    names = ("r_q", "r_k", "r_v", "r_g", "n_q", "n_kc", "n_vc", "n_ks", "n_vs", "n_kw", "n_vw",
             "n_g", "n_bg", "m_a", "m_b")
    starts = {}
    off = 0
    for nm, wd in zip(names, splits):
        starts[nm] = (off, wd)
        off += wd

    sig_cols = ("m_a", "m_b")
    rope_cols = ("r_q", "r_k", "n_q", "n_ks", "n_kw")
    silu_cols = ("r_g", "n_g")
    plain_cols = ("r_v", "n_vs", "n_vw")
    order = sig_cols + rope_cols + silu_cols + plain_cols
    blk = {}
    o128 = 0
    for nm in order:
        blk[nm] = o128
        o128 += starts[nm][1] // LANES
    per_tile = PROJ_TN // LANES
    width = lambda cols: sum(starts[nm][1] for nm in cols)
    n_sig = width(sig_cols) // PROJ_TN
    n_rope = width(rope_cols) // PROJ_TN
    n_silu = width(silu_cols) // PROJ_TN
    assert width(sig_cols) % PROJ_TN == 0 and width(rope_cols) % PROJ_TN == 0
    assert width(silu_cols) % PROJ_TN == 0 and width(plain_cols) % PROJ_TN == 0
    assert blk["n_q"] == blk["r_k"] + RW // LANES and blk["r_k"] % per_tile == 0
    scaled_lo = blk["r_k"] // per_tile
    scaled_hi = (blk["n_q"] + NW // LANES) // per_tile
    assert (blk["n_q"] + NW // LANES) % per_tile == 0

    inv = jnp.exp(jnp.arange(0, HD, 2, dtype=F32) * (-math.log(ROPE_THETA) / HD))
    inv2 = jnp.concatenate([inv, inv]).reshape(1, HD)
    cos, sin = _rope_tables(positions.reshape(T, 1), inv2)
    cend = jnp.arange(S // CMP_STRIDE) * CMP_STRIDE + (CMP_BLOCK - 1)
    cend = jnp.minimum(cend, S - 1)
    cos_c = cos.reshape(B, S, HD)[:, cend]
    sin_c = sin.reshape(B, S, HD)[:, cend]
    log_gamma = jnp.log1p(-jnp.exp2(-5.0 - jnp.arange(RET_HEADS, dtype=F32)))

    tm = min(1024, T)
    for l in range(depth):
        w = w_in[l]
        colsel = lambda nm: w[:, starts[nm][0]:starts[nm][0] + starts[nm][1]]
        w_main = jnp.concatenate([colsel(nm) for nm in order], 1).astype(BF16)
        bg = colsel("n_bg").reshape(D, G, NSA_REP * 3)
        bg = jnp.pad(bg, ((0, 0), (0, 0), (0, LANES - NSA_REP * 3))).reshape(D, G * LANES)
        w_f = jnp.concatenate([colsel("n_kc"), colsel("n_vc"), bg], 1).astype(BF16)

        mod = _mod(c, w_ada[l], b_ada[l])
        shift, scl, gate = mod[:, :D], mod[:, D:2 * D], mod[:, 2 * D:]
        h2 = _h(x, g_norm[l], scl, shift).reshape(T, D)

        P2 = _proj(h2, w_main, cos, sin, n_sig=n_sig, n_rope=n_rope, n_silu=n_silu,
                   scaled_lo=scaled_lo, scaled_hi=scaled_hi, tm=tm)
        F2 = _projf(h2, w_f, tm=tm)
        P3 = P2.reshape(B, S, -1)
        F3 = F2.reshape(B, S, -1)

        y_ret = _retention(P3, log_gamma, g_ret[l], q_blk=blk["r_q"], k_blk=blk["r_k"],
                           v_blk=blk["r_v"], rg_blk=blk["r_g"])

        xh = F3[:, :, :2 * KVW].reshape(B, S // CMP_STRIDE, CMP_STRIDE, 2 * G, HD)
        xh = xh.transpose(0, 3, 1, 2, 4).reshape(B, 2 * G, S // CMP_STRIDE, CMP_STRIDE * HD)
        half = CMP_BLOCK // 2
        pe2 = jnp.stack([pe_ck[l].reshape(2, half * HD), pe_cv[l].reshape(2, half * HD)])
        w1s = jnp.stack([w_ck1[l], w_cv1[l]]).astype(BF16)
        w2s = jnp.stack([w_ck2[l], w_cv2[l]]).astype(BF16)
        kcv = _compress(xh, pe2, w1s, w2s, cos_c, sin_c)

        y_nsa = _nsa(P3, F3, kcv, q_blk4=blk["n_q"] // NSA_REP, ks_blk=blk["n_ks"], vs_blk=blk["n_vs"],
                     kw_blk=blk["n_kw"], vw_blk=blk["n_vw"], ng_blk4=blk["n_g"] // NSA_REP,
                     gt_blk=2 * KVW // LANES)

        merged = _up(y_ret.reshape(T, RW), y_nsa.reshape(T, NW), w_up_ret[l].astype(BF16),
                     w_up_nsa[l].astype(BF16), P2, tm=min(512, T))
        x = _out(merged.reshape(B, S, D), w_out[l].astype(BF16), x, gate, g_final,
                 ts=min(256, S), final_norm=(l + 1 == depth))
    return x
```

```python
import functools
import math

import jax
import jax.numpy as jnp
from jax import lax
from jax.experimental import pallas as pl
from jax.experimental.pallas import tpu as pltpu

F32 = jnp.float32
BF16 = jnp.bfloat16

HD = 128
RET_HEADS = 8
NSA_HEADS = 8
NSA_GROUPS = 2
NSA_REP = NSA_HEADS // NSA_GROUPS
CMP_BLOCK = 32
CMP_STRIDE = 16
CMP_HIDDEN = 256
SEL_BLOCK = 64
SEL_TOPN = 16
WINDOW = 512
FORCE_BONUS = 1.0e4
ROPE_THETA = 10000.0
EPS = 1e-6
NEG = -1.0e30

LANES = 128
PROJ_TN = 512
PROJ_TM = 2048
PROJ_RC = 512
RET_CHUNK = 256
NSA_TQ = 256
LOG2E = math.log2(math.e)
VMEM_LIMIT = 56 * 1024 * 1024


def _sigmoid(v):
    return 1.0 / (1.0 + jnp.exp(-v))


def _dot(a, b):
    return jnp.dot(a, b, preferred_element_type=F32)


def _dot_nt(a, b):
    return lax.dot_general(a, b, (((1,), (1,)), ((), ())), preferred_element_type=F32)


def _dot_tn(a, b):
    return lax.dot_general(a, b, (((0,), (0,)), ((), ())), preferred_element_type=F32)


def _rope(v, cos, sin_signed):
    return v * cos + pltpu.roll(v, HD // 2, 1) * sin_signed


def _mod_kernel(c_ref, w_ref, b_ref, o_ref):
    c = c_ref[...]
    sc = c * _sigmoid(c)
    o_ref[...] = jnp.dot(sc, w_ref[...], preferred_element_type=F32,
                         precision=lax.Precision.HIGHEST) + b_ref[...]


def _mod(c, w_ada, b_ada):
    B, D = c.shape
    N = w_ada.shape[1]
    tn = math.gcd(N, 512)
    return pl.pallas_call(
        _mod_kernel,
        grid=(N // tn,),
        in_specs=[pl.BlockSpec((B, D), lambda j: (0, 0)),
                  pl.BlockSpec((D, tn), lambda j: (0, j)),
                  pl.BlockSpec((1, tn), lambda j: (0, j))],
        out_specs=pl.BlockSpec((B, tn), lambda j: (0, j)),
        out_shape=jax.ShapeDtypeStruct((B, N), F32),
        name="mod",
    )(c, w_ada, b_ada.reshape(1, N))


def _cs_kernel(pos_ref, inv_ref, cos_ref, sin_ref):
    ang = pos_ref[...].astype(F32) * inv_ref[...]
    lane = lax.broadcasted_iota(jnp.int32, (1, HD), 1)
    cos_ref[...] = jnp.cos(ang)
    sin_ref[...] = jnp.where(lane < HD // 2, -1.0, 1.0) * jnp.sin(ang)


def _rope_tables(pos_col, inv2):
    T = pos_col.shape[0]
    tm = 1024
    return pl.pallas_call(
        _cs_kernel,
        grid=(T // tm,),
        in_specs=[pl.BlockSpec((tm, 1), lambda i: (i, 0)),
                  pl.BlockSpec((1, HD), lambda i: (0, 0))],
        out_specs=[pl.BlockSpec((tm, HD), lambda i: (i, 0)),
                   pl.BlockSpec((tm, HD), lambda i: (i, 0))],
        out_shape=[jax.ShapeDtypeStruct((T, HD), F32)] * 2,
        name="cs",
    )(pos_col, inv2)


def _h_kernel(x_ref, g_ref, scl_ref, sh_ref, h_ref):
    x = x_ref[0]
    ms = jnp.mean(x * x, -1, keepdims=True)
    y = x * lax.rsqrt(ms + EPS) * g_ref[...]
    h_ref[0] = (y * (1.0 + scl_ref[0]) + sh_ref[0]).astype(BF16)


def _h(x, g_norm, scl, shift):
    B, S, D = x.shape
    ts = 512
    return pl.pallas_call(
        _h_kernel,
        grid=(B, S // ts),
        in_specs=[pl.BlockSpec((1, ts, D), lambda b, i: (b, i, 0)),
                  pl.BlockSpec((1, D), lambda b, i: (0, 0)),
                  pl.BlockSpec((1, 1, D), lambda b, i: (b, 0, 0)),
                  pl.BlockSpec((1, 1, D), lambda b, i: (b, 0, 0))],
        out_specs=pl.BlockSpec((1, ts, D), lambda b, i: (b, i, 0)),
        out_shape=jax.ShapeDtypeStruct((B, S, D), BF16),
        name="h",
    )(x, g_norm.reshape(1, D), scl.reshape(B, 1, D), shift.reshape(B, 1, D))


def _proj_kernel(h_ref, w_ref, cos_ref, sin_ref, o_ref, *, n_sig, n_rope, n_silu, k_lo, q_lo, q_hi):
    j = pl.program_id(1)
    tm = h_ref.shape[0]
    rc = min(PROJ_RC, tm)

    def row_chunks(epilogue):
        for c in range(tm // rc):
            rows = slice(c * rc, (c + 1) * rc)
            epilogue(rows, _dot(h_ref[rows, :], w_ref[...]))

    @pl.when(j < n_sig)
    def _():
        def epi(rows, acc):
            o_ref[rows, :] = _sigmoid(acc).astype(o_ref.dtype)
        row_chunks(epi)

    @pl.when((j >= n_sig) & (j < n_sig + n_rope))
    def _():
        f = jnp.where((j >= k_lo) & (j < q_lo), HD ** -0.5,
                      jnp.where((j >= q_lo) & (j < q_hi), HD ** -0.5 * LOG2E, 1.0)).astype(F32)

        def epi(rows, acc):
            cos = cos_ref[rows, :] * f
            sin = sin_ref[rows, :] * f
            for hh in range(PROJ_TN // HD):
                sl = slice(hh * HD, (hh + 1) * HD)
                o_ref[rows, sl] = _rope(acc[:, sl], cos, sin).astype(o_ref.dtype)
        row_chunks(epi)

    @pl.when((j >= n_sig + n_rope) & (j < n_sig + n_rope + n_silu))
    def _():
        def epi(rows, acc):
            o_ref[rows, :] = (acc * _sigmoid(acc)).astype(o_ref.dtype)
        row_chunks(epi)

    @pl.when(j >= n_sig + n_rope + n_silu)
    def _():
        def epi(rows, acc):
            o_ref[rows, :] = acc.astype(o_ref.dtype)
        row_chunks(epi)


def _proj(h2, w_main, cos, sin, *, n_sig, n_rope, n_silu, k_lo, q_lo, q_hi, tm):
    T, D = h2.shape
    N = w_main.shape[1]
    kern = functools.partial(_proj_kernel, n_sig=n_sig, n_rope=n_rope, n_silu=n_silu,
                             k_lo=k_lo, q_lo=q_lo, q_hi=q_hi)
    return pl.pallas_call(
        kern,
        grid=(T // tm, N // PROJ_TN),
        in_specs=[pl.BlockSpec((tm, D), lambda i, j: (i, 0)),
                  pl.BlockSpec((D, PROJ_TN), lambda i, j: (0, j)),
                  pl.BlockSpec((tm, HD), lambda i, j: (i, 0)),
                  pl.BlockSpec((tm, HD), lambda i, j: (i, 0))],
        out_specs=pl.BlockSpec((tm, PROJ_TN), lambda i, j: (i, j)),
        out_shape=jax.ShapeDtypeStruct((T, N), BF16),
        compiler_params=pltpu.CompilerParams(
            dimension_semantics=("arbitrary", "arbitrary"), vmem_limit_bytes=VMEM_LIMIT),
        name="proj",
    )(h2, w_main, cos, sin)


def _projf_kernel(h_ref, w_ref, o_ref):
    o_ref[...] = _dot(h_ref[...], w_ref[...])


def _projf(h2, w_f, *, tm):
    T, D = h2.shape
    N = w_f.shape[1]
    return pl.pallas_call(
        _projf_kernel,
        grid=(T // tm,),
        in_specs=[pl.BlockSpec((tm, D), lambda i: (i, 0)),
                  pl.BlockSpec((D, N), lambda i: (0, 0))],
        out_specs=pl.BlockSpec((tm, N), lambda i: (i, 0)),
        out_shape=jax.ShapeDtypeStruct((T, N), F32),
        compiler_params=pltpu.CompilerParams(vmem_limit_bytes=VMEM_LIMIT),
        name="projf",
    )(h2, w_f)


def _ret_kernel(lg_ref, q_ref, k_ref, v_ref, rg_ref, g_ref, o_ref):
    S = q_ref.shape[1]
    C = RET_CHUNK
    lg = lg_ref[pl.program_id(1)]
    r = lax.broadcasted_iota(jnp.int32, (C, C), 0)
    c = lax.broadcasted_iota(jnp.int32, (C, C), 1)
    diff = (r - c).astype(F32)
    dmask = jnp.where(diff >= 0, jnp.exp(lg * jnp.maximum(diff, 0.0)), 0.0)
    rowf = lax.broadcasted_iota(jnp.int32, (C, HD), 0).astype(F32)
    zeta = jnp.exp(lg * (C - 1 - rowf))
    xi = jnp.exp(lg * (rowf + 1.0))
    decay = jnp.exp(jnp.full((HD, HD), lg * C, F32))
    g = g_ref[...]
    state = jnp.zeros((HD, HD), F32)
    for n in range(S // C):
        sl = slice(n * C, (n + 1) * C)
        q = q_ref[0, sl, :]
        k = k_ref[0, sl, :]
        v = v_ref[0, sl, :]
        sc = _dot_nt(q, k) * dmask
        o = _dot(sc.astype(BF16), v)
        if n > 0:
            o = o + _dot(q, state.astype(BF16)) * xi
        if n + 1 < S // C:
            kz = (k.astype(F32) * zeta).astype(BF16)
            state = state * decay + _dot_tn(kz, v)
        mu = jnp.mean(o, -1, keepdims=True)
        d = o - mu
        var = jnp.mean(d * d, -1, keepdims=True)
        on = d * lax.rsqrt(var + EPS)
        o_ref[0, sl, :] = (on * g * rg_ref[0, sl, :].astype(F32)).astype(o_ref.dtype)


def _retention(P3, log_gamma, g_ret, *, q_blk, k_blk, v_blk, rg_blk):
    B, S, _ = P3.shape
    H = RET_HEADS

    def col(off):
        return lambda b, h, lg: (b, 0, off + h)

    return pl.pallas_call(
        _ret_kernel,
        grid_spec=pltpu.PrefetchScalarGridSpec(
            num_scalar_prefetch=1,
            grid=(B, H),
            in_specs=[pl.BlockSpec((1, S, HD), col(q_blk)),
                      pl.BlockSpec((1, S, HD), col(k_blk)),
                      pl.BlockSpec((1, S, HD), col(v_blk)),
                      pl.BlockSpec((1, S, HD), col(rg_blk)),
                      pl.BlockSpec((1, HD), lambda b, h, lg: (0, h))],
            out_specs=pl.BlockSpec((1, S, HD), lambda b, h, lg: (b, 0, h)),
        ),
        out_shape=jax.ShapeDtypeStruct((B, S, H * HD), BF16),
        name="ret",
    )(log_gamma, P3, P3, P3, P3, g_ret.reshape(1, H * HD))


def _cmp_kernel(x_ref, pe_ref, w1_ref, w2_ref, cos_ref, sin_ref, o_ref):
    half = (CMP_BLOCK // 2) * HD
    x = x_ref[0, 0]
    xa = (x + pe_ref[0, 0:1, :]).astype(BF16)
    xb = (x + pe_ref[0, 1:2, :]).astype(BF16)
    a = _dot(xa, w1_ref[0, :half, :])
    b = _dot(xb, w1_ref[0, half:, :])
    nrow = a.shape[0]
    pre = a + pltpu.roll(b, nrow - 1, 0)
    hid = pre * _sigmoid(pre)
    out = _dot(hid.astype(BF16), w2_ref[0])
    roped = _rope(out, cos_ref[0], sin_ref[0])
    is_key = pl.program_id(1) < NSA_GROUPS
    o_ref[0, 0] = jnp.where(is_key, roped, out).astype(o_ref.dtype)


def _compress(xh, pe2, w1s, w2s, cos_c, sin_c):
    B, NC, M, W = xh.shape
    G = NSA_GROUPS
    return pl.pallas_call(
        _cmp_kernel,
        grid=(B, NC),
        in_specs=[pl.BlockSpec((1, 1, M, W), lambda b, c: (b, c, 0, 0)),
                  pl.BlockSpec((1, 2, W), lambda b, c: (c // G, 0, 0)),
                  pl.BlockSpec((1, 2 * W, CMP_HIDDEN), lambda b, c: (c // G, 0, 0)),
                  pl.BlockSpec((1, CMP_HIDDEN, HD), lambda b, c: (c // G, 0, 0)),
                  pl.BlockSpec((1, M, HD), lambda b, c: (b, 0, 0)),
                  pl.BlockSpec((1, M, HD), lambda b, c: (b, 0, 0))],
        out_specs=pl.BlockSpec((1, 1, M, HD), lambda b, c: (b, c, 0, 0)),
        out_shape=jax.ShapeDtypeStruct((B, NC, M, HD), BF16),
        name="cmp",
    )(xh, pe2, w1s, w2s, cos_c, sin_c)


def _lane_tile(v, n):
    return v if n == 1 else jnp.concatenate([v] * n, 1)


def _online_tile(q_ref, k, v_aug, bias, m_ref, acc_ref, *, chunk):
    for c in range(q_ref.shape[0] // chunk):
        rows = slice(c * chunk, (c + 1) * chunk)
        s = _dot_nt(q_ref[rows, :], k)
        if bias is not None:
            s = s + bias
        m_prev = m_ref[rows, :]
        m_new = jnp.maximum(m_prev, jnp.max(s, -1, keepdims=True))
        alpha = jnp.exp2(m_prev - m_new)
        p = jnp.exp2(s - _lane_tile(m_new, s.shape[1] // LANES))
        acc_ref[rows, :] = _lane_tile(alpha, 2) * acc_ref[rows, :] + _dot(p.astype(BF16), v_aug)
        m_ref[rows, :] = m_new


def _nsa_kernel(q_ref, ks_ref, vs_ref, kw_ref, vw_ref, kc_ref, vc_ref, gt_ref, ng_ref, o_ref,
                qa_ref, m_ref, acc_ref, oc_ref, kaug_ref, vsaug_ref, vwaug_ref, wb_ref):
    TQ = q_ref.shape[1]
    S = ks_ref.shape[1]
    R = NSA_REP
    RT = R * TQ
    WSPAN = WINDOW + TQ
    NSEL = S // SEL_BLOCK
    qi = pl.program_id(2)
    t0 = qi * TQ

    @pl.when(qi == 0)
    def _():
        blk = lax.broadcasted_iota(jnp.int32, (S, HD), 0) >> 6
        onehot = jnp.where(blk == lax.broadcasted_iota(jnp.int32, (S, HD), 1), 1.0, 0.0)
        kaug_ref[:, 0:HD] = ks_ref[0]
        kaug_ref[:, HD:2 * HD] = onehot.astype(BF16)
        ones = jnp.ones((S, HD), BF16)
        vsaug_ref[:, 0:HD] = vs_ref[0]
        vsaug_ref[:, HD:2 * HD] = ones
        vwaug_ref[:, 0:HD] = vw_ref[0]
        vwaug_ref[:, HD:2 * HD] = ones
        r = lax.broadcasted_iota(jnp.int32, (TQ, WSPAN), 0)
        c = lax.broadcasted_iota(jnp.int32, (TQ, WSPAN), 1)
        wb_ref[0] = jnp.where(c <= r, 0.0, NEG)
        wb_ref[1] = jnp.where(c <= r + TQ, 0.0, NEG)
        wb_ref[2] = jnp.where((c > r) & (c <= r + WINDOW), 0.0, NEG)

    for h in range(R):
        qa_ref[h * TQ:(h + 1) * TQ, 0:HD] = q_ref[0, :, h * HD:(h + 1) * HD]
    q4 = qa_ref[:, 0:HD]

    s = _dot_nt(q4, kc_ref[0, 0])
    trow = t0 + (lax.broadcasted_iota(jnp.int32, (RT, HD), 0) & (TQ - 1))
    ncol = lax.broadcasted_iota(jnp.int32, (RT, HD), 1)
    s = jnp.where(ncol * CMP_STRIDE + (CMP_BLOCK - 1) <= trow, s, NEG)
    mx = jnp.max(s, -1, keepdims=True)
    p = jnp.exp2(s - mx)
    tcol = t0 + (lax.broadcasted_iota(jnp.int32, (RT, 1), 0) & (TQ - 1))
    any_valid = jnp.where(tcol >= CMP_BLOCK - 1, 1.0, 0.0)
    p = p * (any_valid / jnp.sum(p, -1, keepdims=True))
    oc_ref[...] = _dot(p.astype(BF16), vc_ref[0, 0])

    psum = p[0:TQ]
    for h in range(1, R):
        psum = psum + p[h * TQ:(h + 1) * TQ]
    p_hi = psum.astype(BF16)
    rem = psum - p_hi.astype(F32)
    p_mid = rem.astype(BF16)
    p_lo = (rem - p_mid.astype(F32)).astype(BF16)
    jj = lax.broadcasted_iota(jnp.int32, (NSEL, HD), 0)
    nn = lax.broadcasted_iota(jnp.int32, (NSEL, HD), 1)
    ov = jnp.maximum(jnp.minimum(nn * CMP_STRIDE + CMP_BLOCK, (jj + 1) * SEL_BLOCK)
                     - jnp.maximum(nn * CMP_STRIDE, jj * SEL_BLOCK), 0).astype(F32) * (1.0 / CMP_BLOCK)
    ov = ov.astype(BF16)
    imp = (_dot_nt(ov, p_lo) + _dot_nt(ov, p_mid)) + _dot_nt(ov, p_hi)

    j = lax.broadcasted_iota(jnp.int32, (NSEL, TQ), 0)
    tb = (t0 + lax.broadcasted_iota(jnp.int32, (NSEL, TQ), 1)) >> 6
    forced = (j == 0) | (j == tb) | (j == tb - 1)
    causal = j <= tb
    sc = jnp.where(causal, imp + jnp.where(forced, FORCE_BONUS, 0.0), NEG)
    rank = jnp.zeros((NSEL, TQ), F32)
    for i in range(NSEL):
        si = sc[i:i + 1, :]
        rank = rank + jnp.where(j > i, jnp.where(si >= sc, 1.0, 0.0), jnp.where(si > sc, 1.0, 0.0))
    sel = jnp.where((rank < SEL_TOPN) & causal, 1.0, 0.0)

    selpad = jnp.concatenate([sel, jnp.zeros((HD - NSEL, TQ), F32)], 0).astype(BF16)
    eye = jnp.where(lax.broadcasted_iota(jnp.int32, (TQ, TQ), 0)
                    == lax.broadcasted_iota(jnp.int32, (TQ, TQ), 1), 1.0, 0.0).astype(BF16)
    sel_t = _dot_nt(eye, selpad)
    lane = lax.broadcasted_iota(jnp.int32, (TQ, HD), 1)
    bias = jnp.where((lane < NSEL) & (sel_t < 0.5), NEG, 0.0).astype(BF16)
    for h in range(R):
        qa_ref[h * TQ:(h + 1) * TQ, HD:2 * HD] = bias

    m_ref[...] = jnp.full(m_ref.shape, NEG, F32)
    acc_ref[...] = jnp.zeros(acc_ref.shape, F32)

    def sel_body(i, carry):
        off = pl.multiple_of(i * (2 * TQ), 2 * TQ)
        _online_tile(qa_ref, kaug_ref[pl.ds(off, 2 * TQ), :], vsaug_ref[pl.ds(off, 2 * TQ), :], None,
                     m_ref, acc_ref, chunk=TQ)
        return carry

    lax.fori_loop(0, qi >> 1, sel_body, 0)

    @pl.when((qi & 1) == 1)
    def _():
        off = pl.multiple_of(t0 - TQ, TQ)
        _online_tile(qa_ref, kaug_ref[pl.ds(off, 2 * TQ), :], vsaug_ref[pl.ds(off, 2 * TQ), :],
                     wb_ref[1, :, 0:2 * TQ], m_ref, acc_ref, chunk=TQ)

    @pl.when((qi & 1) == 0)
    def _():
        off = pl.multiple_of(t0, TQ)
        _online_tile(qa_ref, kaug_ref[pl.ds(off, TQ), :], vsaug_ref[pl.ds(off, TQ), :],
                     wb_ref[0, :, 0:TQ], m_ref, acc_ref, chunk=TQ)

    offw = pl.multiple_of(jnp.maximum(t0 - WINDOW, 0), TQ)
    kwin = kw_ref[0, pl.ds(offw, WSPAN), :]
    vwin = vwaug_ref[pl.ds(offw, WSPAN), :]
    wbias = wb_ref[jnp.minimum(qi, 2)]
    gt = _sigmoid(gt_ref[0])
    for h in range(R):
        rows = slice(h * TQ, (h + 1) * TQ)
        sw = _dot_nt(qa_ref[rows, 0:HD], kwin) + wbias
        pw = jnp.exp2(sw - jnp.max(sw, -1, keepdims=True))
        ow = _dot(pw.astype(BF16), vwin)
        o = (gt[:, 3 * h:3 * h + 1] * oc_ref[rows, :]
             + gt[:, 3 * h + 1:3 * h + 2] * (acc_ref[rows, 0:HD] / acc_ref[rows, HD:2 * HD])
             + gt[:, 3 * h + 2:3 * h + 3] * (ow[:, 0:HD] / ow[:, HD:2 * HD]))
        o_ref[0, :, h * HD:(h + 1) * HD] = (o * ng_ref[0, :, h * HD:(h + 1) * HD].astype(F32)).astype(o_ref.dtype)


def _nsa(P3, F3, kcv, *, q_blk4, ks_blk, vs_blk, kw_blk, vw_blk, ng_blk4, gt_blk):
    B, S, _ = P3.shape
    G, R, TQ = NSA_GROUPS, NSA_REP, NSA_TQ
    assert WINDOW == 2 * TQ and SEL_BLOCK == 64 and S % (2 * TQ) == 0 and S >= WINDOW + TQ
    M = kcv.shape[2]
    kv_spec = lambda blk: pl.BlockSpec((1, S, HD), lambda b, g, i: (b, 0, blk + g))
    return pl.pallas_call(
        _nsa_kernel,
        grid=(B, G, S // TQ),
        in_specs=[pl.BlockSpec((1, TQ, R * HD), lambda b, g, i: (b, i, q_blk4 + g)),
                  kv_spec(ks_blk), kv_spec(vs_blk), kv_spec(kw_blk), kv_spec(vw_blk),
                  pl.BlockSpec((1, 1, M, HD), lambda b, g, i: (b, g, 0, 0)),
                  pl.BlockSpec((1, 1, M, HD), lambda b, g, i: (b, G + g, 0, 0)),
                  pl.BlockSpec((1, TQ, LANES), lambda b, g, i: (b, i, gt_blk + g)),
                  pl.BlockSpec((1, TQ, R * HD), lambda b, g, i: (b, i, ng_blk4 + g))],
        out_specs=pl.BlockSpec((1, TQ, R * HD), lambda b, g, i: (b, i, g)),
        out_shape=jax.ShapeDtypeStruct((B, S, NSA_HEADS * HD), BF16),
        scratch_shapes=[pltpu.VMEM((R * TQ, 2 * HD), BF16),
                        pltpu.VMEM((R * TQ, LANES), F32),
                        pltpu.VMEM((R * TQ, 2 * HD), F32),
                        pltpu.VMEM((R * TQ, HD), F32),
                        pltpu.VMEM((S, 2 * HD), BF16),
                        pltpu.VMEM((S, 2 * HD), BF16),
                        pltpu.VMEM((S, 2 * HD), BF16),
                        pltpu.VMEM((3, TQ, WINDOW + TQ), F32)],
        compiler_params=pltpu.CompilerParams(
            dimension_semantics=("arbitrary", "arbitrary", "arbitrary"), vmem_limit_bytes=VMEM_LIMIT),
        name="nsa",
    )(P3, P3, P3, P3, P3, kcv, kcv, F3, P3)


def _up_kernel(yr_ref, yn_ref, wr_ref, wn_ref, sa_ref, sb_ref, o_ref):
    a = _dot(yr_ref[...], wr_ref[...])
    b = _dot(yn_ref[...], wn_ref[...])
    o_ref[...] = (sa_ref[...].astype(F32) * a + sb_ref[...].astype(F32) * b).astype(o_ref.dtype)


def _up(y_ret, y_nsa, w_ur, w_un, P2, *, tm):
    T, W = y_ret.shape
    D = w_ur.shape[1]
    return pl.pallas_call(
        _up_kernel,
        grid=(T // tm,),
        in_specs=[pl.BlockSpec((tm, W), lambda i: (i, 0)),
                  pl.BlockSpec((tm, W), lambda i: (i, 0)),
                  pl.BlockSpec((W, D), lambda i: (0, 0)),
                  pl.BlockSpec((W, D), lambda i: (0, 0)),
                  pl.BlockSpec((tm, D), lambda i: (i, 0)),
                  pl.BlockSpec((tm, D), lambda i: (i, 1))],
        out_specs=pl.BlockSpec((tm, D), lambda i: (i, 0)),
        out_shape=jax.ShapeDtypeStruct((T, D), BF16),
        compiler_params=pltpu.CompilerParams(vmem_limit_bytes=VMEM_LIMIT),
        name="up",
    )(y_ret, y_nsa, w_ur, w_un, P2, P2)


def _out_kernel(m_ref, w_ref, x_ref, gate_ref, g_ref, o_ref, *, final_norm):
    y = x_ref[0] + gate_ref[0] * _dot(m_ref[0], w_ref[...])
    if final_norm:
        ms = jnp.mean(y * y, -1, keepdims=True)
        y = y * lax.rsqrt(ms + EPS) * g_ref[...]
    o_ref[0] = y


def _out(merged3, w_out, x, gate, g_final, *, ts, final_norm):
    B, S, D = x.shape
    return pl.pallas_call(
        functools.partial(_out_kernel, final_norm=final_norm),
        grid=(B, S // ts),
        in_specs=[pl.BlockSpec((1, ts, D), lambda b, i: (b, i, 0)),
                  pl.BlockSpec((D, D), lambda b, i: (0, 0)),
                  pl.BlockSpec((1, ts, D), lambda b, i: (b, i, 0)),
                  pl.BlockSpec((1, 1, D), lambda b, i: (b, 0, 0)),
                  pl.BlockSpec((1, D), lambda b, i: (0, 0))],
        out_specs=pl.BlockSpec((1, ts, D), lambda b, i: (b, i, 0)),
        out_shape=jax.ShapeDtypeStruct((B, S, D), F32),
        compiler_params=pltpu.CompilerParams(vmem_limit_bytes=VMEM_LIMIT),
        name="out",
    )(merged3, w_out, x, gate.reshape(B, 1, D), g_final.reshape(1, D))


def kernel(x, c, positions, w_ada, b_ada, g_norm, w_in, g_ret, w_ck1, w_ck2, pe_ck,
           w_cv1, w_cv2, pe_cv, w_up_ret, w_up_nsa, w_out, g_final):
    B, S, D = x.shape
    T = B * S
    depth = w_ada.shape[0]
    G = NSA_GROUPS
    RW = RET_HEADS * HD
    NW = NSA_HEADS * HD
    KVW = G * HD
    splits = (RW, RW, RW, RW, NW, KVW, KVW, KVW, KVW, KVW, KVW, NW, NSA_HEADS * 3, D, D)
    names = ("r_q", "r_k", "r_v", "r_g", "n_q", "n_kc", "n_vc", "n_ks", "n_vs", "n_kw", "n_vw",
             "n_g", "n_bg", "m_a", "m_b")
    starts = {}
    off = 0
    for nm, wd in zip(names, splits):
        starts[nm] = (off, wd)
        off += wd

    sig_cols = ("m_a", "m_b")
    rope_cols = ("r_q", "r_k", "n_q", "n_ks", "n_kw")
    silu_cols = ("r_g", "n_g")
    plain_cols = ("r_v", "n_vs", "n_vw")
    order = sig_cols + rope_cols + silu_cols + plain_cols
    blk = {}
    o128 = 0
    for nm in order:
        blk[nm] = o128
        o128 += starts[nm][1] // LANES
    per_tile = PROJ_TN // LANES
    width = lambda cols: sum(starts[nm][1] for nm in cols)
    n_sig = width(sig_cols) // PROJ_TN
    n_rope = width(rope_cols) // PROJ_TN
    n_silu = width(silu_cols) // PROJ_TN
    assert width(sig_cols) % PROJ_TN == 0 and width(rope_cols) % PROJ_TN == 0
    assert width(silu_cols) % PROJ_TN == 0 and width(plain_cols) % PROJ_TN == 0
    assert blk["n_q"] == blk["r_k"] + RW // LANES and blk["r_k"] % per_tile == 0
    assert blk["n_q"] % per_tile == 0 and (blk["n_q"] + NW // LANES) % per_tile == 0
    k_lo = blk["r_k"] // per_tile
    q_lo = blk["n_q"] // per_tile
    q_hi = (blk["n_q"] + NW // LANES) // per_tile

    inv = jnp.exp(jnp.arange(0, HD, 2, dtype=F32) * (-math.log(ROPE_THETA) / HD))
    inv2 = jnp.concatenate([inv, inv]).reshape(1, HD)
    cos, sin = _rope_tables(positions.reshape(T, 1), inv2)
    cend = jnp.arange(S // CMP_STRIDE) * CMP_STRIDE + (CMP_BLOCK - 1)
    cend = jnp.minimum(cend, S - 1)
    cos_c = cos.reshape(B, S, HD)[:, cend]
    sin_c = sin.reshape(B, S, HD)[:, cend]
    log_gamma = jnp.log1p(-jnp.exp2(-5.0 - jnp.arange(RET_HEADS, dtype=F32)))

    tm = min(1024, T)
    for l in range(depth):
        w = w_in[l]
        colsel = lambda nm: w[:, starts[nm][0]:starts[nm][0] + starts[nm][1]]
        w_main = jnp.concatenate([colsel(nm) for nm in order], 1).astype(BF16)
        bg = colsel("n_bg").reshape(D, G, NSA_REP * 3)
        bg = jnp.pad(bg, ((0, 0), (0, 0), (0, LANES - NSA_REP * 3))).reshape(D, G * LANES)
        w_f = jnp.concatenate([colsel("n_kc"), colsel("n_vc"), bg], 1).astype(BF16)

        mod = _mod(c, w_ada[l], b_ada[l])
        shift, scl, gate = mod[:, :D], mod[:, D:2 * D], mod[:, 2 * D:]
        h2 = _h(x, g_norm[l], scl, shift).reshape(T, D)

        P2 = _proj(h2, w_main, cos, sin, n_sig=n_sig, n_rope=n_rope, n_silu=n_silu,
                   k_lo=k_lo, q_lo=q_lo, q_hi=q_hi, tm=min(PROJ_TM, T))
        F2 = _projf(h2, w_f, tm=tm)
        P3 = P2.reshape(B, S, -1)
        F3 = F2.reshape(B, S, -1)

        y_ret = _retention(P3, log_gamma, g_ret[l], q_blk=blk["r_q"], k_blk=blk["r_k"],
                           v_blk=blk["r_v"], rg_blk=blk["r_g"])

        xh = F3[:, :, :2 * KVW].reshape(B, S // CMP_STRIDE, CMP_STRIDE, 2 * G, HD)
        xh = xh.transpose(0, 3, 1, 2, 4).reshape(B, 2 * G, S // CMP_STRIDE, CMP_STRIDE * HD)
        half = CMP_BLOCK // 2
        pe2 = jnp.stack([pe_ck[l].reshape(2, half * HD), pe_cv[l].reshape(2, half * HD)])
        w1s = jnp.stack([w_ck1[l], w_cv1[l]]).astype(BF16)
        w2s = jnp.stack([w_ck2[l], w_cv2[l]]).astype(BF16)
        kcv = _compress(xh, pe2, w1s, w2s, cos_c, sin_c)

        y_nsa = _nsa(P3, F3, kcv, q_blk4=blk["n_q"] // NSA_REP, ks_blk=blk["n_ks"], vs_blk=blk["n_vs"],
                     kw_blk=blk["n_kw"], vw_blk=blk["n_vw"], ng_blk4=blk["n_g"] // NSA_REP,
                     gt_blk=2 * KVW // LANES)

        merged = _up(y_ret.reshape(T, RW), y_nsa.reshape(T, NW), w_up_ret[l].astype(BF16),
                     w_up_nsa[l].astype(BF16), P2, tm=min(512, T))
        x = _out(merged.reshape(B, S, D), w_out[l].astype(BF16), x, gate, g_final,
                 ts=min(256, S), final_norm=(l + 1 == depth))
    return x
```

```python
import functools
import math

import jax
import jax.numpy as jnp
from jax import lax
from jax.experimental import pallas as pl
from jax.experimental.pallas import tpu as pltpu

F32 = jnp.float32
BF16 = jnp.bfloat16

HD = 128
RET_HEADS = 8
NSA_HEADS = 8
NSA_GROUPS = 2
NSA_REP = NSA_HEADS // NSA_GROUPS
CMP_BLOCK = 32
CMP_STRIDE = 16
CMP_HIDDEN = 256
SEL_BLOCK = 64
SEL_TOPN = 16
WINDOW = 512
FORCE_BONUS = 1.0e4
ROPE_THETA = 10000.0
EPS = 1e-6
NEG = -1.0e30

LANES = 128
PROJ_TN = 512
PROJ_TM = 2048
PROJ_RC = 512
RET_CHUNK = 256
RET_HPS = 4
NSA_TQ = 256
LOG2E = math.log2(math.e)
VMEM_LIMIT = 56 * 1024 * 1024


def _sigmoid(v):
    return 1.0 / (1.0 + jnp.exp(-v))


def _dot(a, b):
    return jnp.dot(a, b, preferred_element_type=F32)


def _dot_nt(a, b):
    return lax.dot_general(a, b, (((1,), (1,)), ((), ())), preferred_element_type=F32)


def _dot_tn(a, b):
    return lax.dot_general(a, b, (((0,), (0,)), ((), ())), preferred_element_type=F32)


def _rope(v, cos, sin_signed):
    return v * cos + pltpu.roll(v, HD // 2, 1) * sin_signed


def _mod_kernel(c_ref, w_ref, b_ref, o_ref):
    c = c_ref[...]
    sc = c * _sigmoid(c)
    o_ref[...] = jnp.dot(sc, w_ref[...], preferred_element_type=F32,
                         precision=lax.Precision.HIGHEST) + b_ref[...]


def _mod(c, w_ada, b_ada):
    B, D = c.shape
    N = w_ada.shape[1]
    tn = math.gcd(N, 512)
    return pl.pallas_call(
        _mod_kernel,
        grid=(N // tn,),
        in_specs=[pl.BlockSpec((B, D), lambda j: (0, 0)),
                  pl.BlockSpec((D, tn), lambda j: (0, j)),
                  pl.BlockSpec((1, tn), lambda j: (0, j))],
        out_specs=pl.BlockSpec((B, tn), lambda j: (0, j)),
        out_shape=jax.ShapeDtypeStruct((B, N), F32),
        name="mod",
    )(c, w_ada, b_ada.reshape(1, N))


def _cs_kernel(pos_ref, inv_ref, cos_ref, sin_ref):
    ang = pos_ref[...].astype(F32) * inv_ref[...]
    lane = lax.broadcasted_iota(jnp.int32, (1, HD), 1)
    cos_ref[...] = jnp.cos(ang)
    sin_ref[...] = jnp.where(lane < HD // 2, -1.0, 1.0) * jnp.sin(ang)


def _rope_tables(pos_col, inv2):
    T = pos_col.shape[0]
    tm = 1024
    return pl.pallas_call(
        _cs_kernel,
        grid=(T // tm,),
        in_specs=[pl.BlockSpec((tm, 1), lambda i: (i, 0)),
                  pl.BlockSpec((1, HD), lambda i: (0, 0))],
        out_specs=[pl.BlockSpec((tm, HD), lambda i: (i, 0)),
                   pl.BlockSpec((tm, HD), lambda i: (i, 0))],
        out_shape=[jax.ShapeDtypeStruct((T, HD), F32)] * 2,
        name="cs",
    )(pos_col, inv2)


def _h_kernel(x_ref, g_ref, scl_ref, sh_ref, h_ref):
    x = x_ref[0]
    ms = jnp.mean(x * x, -1, keepdims=True)
    y = x * lax.rsqrt(ms + EPS) * g_ref[...]
    h_ref[0] = (y * (1.0 + scl_ref[0]) + sh_ref[0]).astype(BF16)


def _h(x, g_norm, scl, shift):
    B, S, D = x.shape
    ts = 512
    return pl.pallas_call(
        _h_kernel,
        grid=(B, S // ts),
        in_specs=[pl.BlockSpec((1, ts, D), lambda b, i: (b, i, 0)),
                  pl.BlockSpec((1, D), lambda b, i: (0, 0)),
                  pl.BlockSpec((1, 1, D), lambda b, i: (b, 0, 0)),
                  pl.BlockSpec((1, 1, D), lambda b, i: (b, 0, 0))],
        out_specs=pl.BlockSpec((1, ts, D), lambda b, i: (b, i, 0)),
        out_shape=jax.ShapeDtypeStruct((B, S, D), BF16),
        name="h",
    )(x, g_norm.reshape(1, D), scl.reshape(B, 1, D), shift.reshape(B, 1, D))


def _tile_cond(j, idxs):
    cond = None
    run_lo = prev = None
    for t in list(idxs) + [None]:
        if run_lo is None:
            run_lo = prev = t
            continue
        if t is not None and t == prev + 1:
            prev = t
            continue
        c = (j >= run_lo) & (j <= prev)
        cond = c if cond is None else (cond | c)
        run_lo = prev = t
    return cond


def _proj_kernel(h_ref, ws_ref, w_ref, cos_ref, sin_ref, o_ref, *, tiles):
    j = pl.program_id(1)
    tm = h_ref.shape[0]
    rc = min(PROJ_RC, tm)
    half = PROJ_TN // 2

    def row_chunks(w, epilogue):
        for c in range(tm // rc):
            rows = slice(c * rc, (c + 1) * rc)
            epilogue(rows, _dot(h_ref[rows, :], w))

    def of_kind(kind):
        return [t for t, (k, _) in enumerate(tiles) if k == kind]

    def rope_cols(rows, acc, cos, sin, lo, hi):
        for hh in range(lo // HD, hi // HD):
            sl = slice(hh * HD, (hh + 1) * HD)
            o_ref[rows, sl] = _rope(acc[:, sl], cos, sin).astype(o_ref.dtype)

    @pl.when(_tile_cond(j, of_kind("sig")))
    def _():
        def epi(rows, acc):
            o_ref[rows, :] = _sigmoid(acc).astype(o_ref.dtype)
        row_chunks(ws_ref[...], epi)

    @pl.when(_tile_cond(j, of_kind("rope")))
    def _():
        f = jnp.float32(1.0)
        for scale in sorted({s for k, s in tiles if k == "rope" and s != 1.0}):
            f = jnp.where(_tile_cond(j, [t for t, (k, s) in enumerate(tiles) if k == "rope" and s == scale]),
                          jnp.float32(scale), f)

        def epi(rows, acc):
            rope_cols(rows, acc, cos_ref[rows, :] * f, sin_ref[rows, :] * f, 0, PROJ_TN)
        row_chunks(w_ref[...].astype(BF16), epi)

    @pl.when(_tile_cond(j, of_kind("rope_plain")))
    def _():
        def epi(rows, acc):
            rope_cols(rows, acc, cos_ref[rows, :], sin_ref[rows, :], 0, half)
            o_ref[rows, half:] = acc[:, half:].astype(o_ref.dtype)
        row_chunks(w_ref[...].astype(BF16), epi)

    @pl.when(_tile_cond(j, of_kind("silu")))
    def _():
        def epi(rows, acc):
            o_ref[rows, :] = (acc * _sigmoid(acc)).astype(o_ref.dtype)
        row_chunks(w_ref[...].astype(BF16), epi)

    @pl.when(_tile_cond(j, of_kind("plain")))
    def _():
        def epi(rows, acc):
            o_ref[rows, :] = acc.astype(o_ref.dtype)
        row_chunks(w_ref[...].astype(BF16), epi)


def _proj(h2, w_sig, w_in, layer, cos, sin, *, tiles, src_tiles, tm):
    T, D = h2.shape
    n_sig = w_sig.shape[1] // PROJ_TN
    n_tiles = len(tiles)
    assert all(src_tiles[t + 1] - src_tiles[t] in (1, 2) for t in range(n_sig, n_tiles - 1))
    skips = [t for t in range(n_sig, n_tiles - 1) if src_tiles[t + 1] - src_tiles[t] == 2]

    def src(j):
        s = jnp.maximum(j - n_sig, 0) + src_tiles[n_sig]
        for t in skips:
            s = s + jnp.where(j > t, 1, 0)
        return s

    kern = functools.partial(_proj_kernel, tiles=tuple(tiles))
    return pl.pallas_call(
        kern,
        grid=(T // tm, n_tiles),
        in_specs=[pl.BlockSpec((tm, D), lambda i, j: (i, 0)),
                  pl.BlockSpec((D, PROJ_TN), lambda i, j: (0, jnp.minimum(j, n_sig - 1))),
                  pl.BlockSpec((None, D, PROJ_TN), lambda i, j: (layer, 0, src(j))),
                  pl.BlockSpec((tm, HD), lambda i, j: (i, 0)),
                  pl.BlockSpec((tm, HD), lambda i, j: (i, 0))],
        out_specs=pl.BlockSpec((tm, PROJ_TN), lambda i, j: (i, j)),
        out_shape=jax.ShapeDtypeStruct((T, n_tiles * PROJ_TN), BF16),
        compiler_params=pltpu.CompilerParams(
            dimension_semantics=("arbitrary", "arbitrary"), vmem_limit_bytes=VMEM_LIMIT),
        name="proj",
    )(h2, w_sig, w_in, cos, sin)


def _projf_kernel(h_ref, w_ref, kv_ref, gt_ref):
    n_kv = kv_ref.shape[0]
    acc = _dot(h_ref[...], w_ref[...])
    for c in range(n_kv):
        kv_ref[c] = acc[:, c * HD:(c + 1) * HD]
    gt_ref[...] = acc[:, n_kv * HD:]


def _projf(h2, w_f, *, n_kv, tm):
    T, D = h2.shape
    N = w_f.shape[1]
    return pl.pallas_call(
        _projf_kernel,
        grid=(T // tm,),
        in_specs=[pl.BlockSpec((tm, D), lambda i: (i, 0)),
                  pl.BlockSpec((D, N), lambda i: (0, 0))],
        out_specs=[pl.BlockSpec((n_kv, tm, HD), lambda i: (0, i, 0)),
                   pl.BlockSpec((tm, N - n_kv * HD), lambda i: (i, 0))],
        out_shape=[jax.ShapeDtypeStruct((n_kv, T, HD), F32),
                   jax.ShapeDtypeStruct((T, N - n_kv * HD), F32)],
        compiler_params=pltpu.CompilerParams(vmem_limit_bytes=VMEM_LIMIT),
        name="projf",
    )(h2, w_f)


def _ret_kernel(lg_ref, q_ref, k_ref, v_ref, rg_ref, g_ref, o_ref):
    S = q_ref.shape[1]
    C = RET_CHUNK
    r = lax.broadcasted_iota(jnp.int32, (C, C), 0)
    c = lax.broadcasted_iota(jnp.int32, (C, C), 1)
    diff = (r - c).astype(F32)
    rowf = lax.broadcasted_iota(jnp.int32, (C, HD), 0).astype(F32)
    for hh in range(RET_HPS):
        cols = slice(hh * HD, (hh + 1) * HD)
        lg = lg_ref[pl.program_id(1) * RET_HPS + hh]
        dmask = jnp.where(diff >= 0, jnp.exp(lg * jnp.maximum(diff, 0.0)), 0.0)
        zeta = jnp.exp(lg * (C - 1 - rowf))
        xi = jnp.exp(lg * (rowf + 1.0))
        decay = jnp.exp(jnp.full((HD, HD), lg * C, F32))
        g = g_ref[:, cols]
        state = jnp.zeros((HD, HD), F32)
        for n in range(S // C):
            sl = slice(n * C, (n + 1) * C)
            q = q_ref[0, sl, cols]
            k = k_ref[0, sl, cols]
            v = v_ref[0, sl, cols]
            sc = _dot_nt(q, k) * dmask
            o = _dot(sc.astype(BF16), v)
            if n > 0:
                o = o + _dot(q, state.astype(BF16)) * xi
            if n + 1 < S // C:
                kz = (k.astype(F32) * zeta).astype(BF16)
                state = state * decay + _dot_tn(kz, v)
            mu = jnp.mean(o, -1, keepdims=True)
            d = o - mu
            var = jnp.mean(d * d, -1, keepdims=True)
            on = d * lax.rsqrt(var + EPS)
            o_ref[0, sl, cols] = (on * g * rg_ref[0, sl, cols].astype(F32)).astype(o_ref.dtype)


def _retention(P3, log_gamma, g_ret, *, q_blk, k_blk, v_blk, rg_blk):
    B, S, _ = P3.shape
    H = RET_HEADS
    W = RET_HPS * HD

    def col(off):
        return lambda b, h, lg: (b, 0, off + h)

    return pl.pallas_call(
        _ret_kernel,
        grid_spec=pltpu.PrefetchScalarGridSpec(
            num_scalar_prefetch=1,
            grid=(B, H // RET_HPS),
            in_specs=[pl.BlockSpec((1, S, W), col(q_blk)),
                      pl.BlockSpec((1, S, W), col(k_blk)),
                      pl.BlockSpec((1, S, W), col(v_blk)),
                      pl.BlockSpec((1, S, W), col(rg_blk)),
                      pl.BlockSpec((1, W), lambda b, h, lg: (0, h))],
            out_specs=pl.BlockSpec((1, S, W), lambda b, h, lg: (b, 0, h)),
        ),
        out_shape=jax.ShapeDtypeStruct((B, S, H * HD), BF16),
        compiler_params=pltpu.CompilerParams(vmem_limit_bytes=VMEM_LIMIT),
        name="ret",
    )(log_gamma, P3, P3, P3, P3, g_ret.reshape(1, H * HD))


def _cmp_kernel(x_ref, pe_ref, w1_ref, w2_ref, cos_ref, sin_ref, o_ref):
    half = (CMP_BLOCK // 2) * HD
    x = x_ref[0, 0]
    xa = (x + pe_ref[0, 0:1, :]).astype(BF16)
    xb = (x + pe_ref[0, 1:2, :]).astype(BF16)
    a = _dot(xa, w1_ref[0, :half, :])
    b = _dot(xb, w1_ref[0, half:, :])
    nrow = a.shape[0]
    pre = a + pltpu.roll(b, nrow - 1, 0)
    hid = pre * _sigmoid(pre)
    out = _dot(hid.astype(BF16), w2_ref[0])
    roped = _rope(out, cos_ref[0], sin_ref[0])
    is_key = pl.program_id(1) < NSA_GROUPS
    o_ref[0, 0] = jnp.where(is_key, roped, out).astype(o_ref.dtype)


def _compress(xh, pe2, w1s, w2s, cos_c, sin_c):
    NC, B, M, W = xh.shape
    G = NSA_GROUPS
    return pl.pallas_call(
        _cmp_kernel,
        grid=(B, NC),
        in_specs=[pl.BlockSpec((1, 1, M, W), lambda b, c: (c, b, 0, 0)),
                  pl.BlockSpec((1, 2, W), lambda b, c: (c // G, 0, 0)),
                  pl.BlockSpec((1, 2 * W, CMP_HIDDEN), lambda b, c: (c // G, 0, 0)),
                  pl.BlockSpec((1, CMP_HIDDEN, HD), lambda b, c: (c // G, 0, 0)),
                  pl.BlockSpec((1, M, HD), lambda b, c: (b, 0, 0)),
                  pl.BlockSpec((1, M, HD), lambda b, c: (b, 0, 0))],
        out_specs=pl.BlockSpec((1, 1, M, HD), lambda b, c: (b, c, 0, 0)),
        out_shape=jax.ShapeDtypeStruct((B, NC, M, HD), BF16),
        name="cmp",
    )(xh, pe2, w1s, w2s, cos_c, sin_c)


def _lane_tile(v, n):
    return v if n == 1 else jnp.concatenate([v] * n, 1)


def _online_tile(q_ref, k, v_aug, bias, m_ref, acc_ref, *, chunk):
    for c in range(q_ref.shape[0] // chunk):
        rows = slice(c * chunk, (c + 1) * chunk)
        s = _dot_nt(q_ref[rows, :], k)
        if bias is not None:
            s = s + bias
        m_prev = m_ref[rows, :]
        m_new = jnp.maximum(m_prev, jnp.max(s, -1, keepdims=True))
        alpha = jnp.exp2(m_prev - m_new)
        p = jnp.exp2(s - _lane_tile(m_new, s.shape[1] // LANES))
        acc_ref[rows, :] = _lane_tile(alpha, 2) * acc_ref[rows, :] + _dot(p.astype(BF16), v_aug)
        m_ref[rows, :] = m_new


def _nsa_kernel(q_ref, ks_ref, vs_ref, kw_ref, vw_ref, kc_ref, vc_ref, gt_ref, ng_ref, o_ref,
                qa_ref, m_ref, acc_ref, oc_ref, kaug_ref, vsaug_ref, vwaug_ref, wb_ref):
    TQ = q_ref.shape[1]
    S = ks_ref.shape[1]
    R = NSA_REP
    RT = R * TQ
    WSPAN = WINDOW + TQ
    NSEL = S // SEL_BLOCK
    qi = pl.program_id(2)
    t0 = qi * TQ

    @pl.when(qi == 0)
    def _():
        blk = lax.broadcasted_iota(jnp.int32, (S, HD), 0) >> 6
        onehot = jnp.where(blk == lax.broadcasted_iota(jnp.int32, (S, HD), 1), 1.0, 0.0)
        kaug_ref[:, 0:HD] = ks_ref[0]
        kaug_ref[:, HD:2 * HD] = onehot.astype(BF16)
        ones = jnp.ones((S, HD), BF16)
        vsaug_ref[:, 0:HD] = vs_ref[0]
        vsaug_ref[:, HD:2 * HD] = ones
        vwaug_ref[:, 0:HD] = vw_ref[0]
        vwaug_ref[:, HD:2 * HD] = ones
        r = lax.broadcasted_iota(jnp.int32, (TQ, WSPAN), 0)
        c = lax.broadcasted_iota(jnp.int32, (TQ, WSPAN), 1)
        wb_ref[0] = jnp.where(c <= r, 0.0, NEG)
        wb_ref[1] = jnp.where(c <= r + TQ, 0.0, NEG)
        wb_ref[2] = jnp.where((c > r) & (c <= r + WINDOW), 0.0, NEG)

    for h in range(R):
        qa_ref[h * TQ:(h + 1) * TQ, 0:HD] = q_ref[0, :, h * HD:(h + 1) * HD]
    q4 = qa_ref[:, 0:HD]

    s = _dot_nt(q4, kc_ref[0, 0])
    trow = t0 + (lax.broadcasted_iota(jnp.int32, (RT, HD), 0) & (TQ - 1))
    ncol = lax.broadcasted_iota(jnp.int32, (RT, HD), 1)
    s = jnp.where(ncol * CMP_STRIDE + (CMP_BLOCK - 1) <= trow, s, NEG)
    mx = jnp.max(s, -1, keepdims=True)
    p = jnp.exp2(s - mx)
    tcol = t0 + (lax.broadcasted_iota(jnp.int32, (RT, 1), 0) & (TQ - 1))
    any_valid = jnp.where(tcol >= CMP_BLOCK - 1, 1.0, 0.0)
    p = p * (any_valid / jnp.sum(p, -1, keepdims=True))
    oc_ref[...] = _dot(p.astype(BF16), vc_ref[0, 0])

    psum = p[0:TQ]
    for h in range(1, R):
        psum = psum + p[h * TQ:(h + 1) * TQ]
    p_hi = psum.astype(BF16)
    rem = psum - p_hi.astype(F32)
    p_mid = rem.astype(BF16)
    p_lo = (rem - p_mid.astype(F32)).astype(BF16)
    jj = lax.broadcasted_iota(jnp.int32, (NSEL, HD), 0)
    nn = lax.broadcasted_iota(jnp.int32, (NSEL, HD), 1)
    ov = jnp.maximum(jnp.minimum(nn * CMP_STRIDE + CMP_BLOCK, (jj + 1) * SEL_BLOCK)
                     - jnp.maximum(nn * CMP_STRIDE, jj * SEL_BLOCK), 0).astype(F32) * (1.0 / CMP_BLOCK)
    ov = ov.astype(BF16)
    imp = (_dot_nt(ov, p_lo) + _dot_nt(ov, p_mid)) + _dot_nt(ov, p_hi)

    j = lax.broadcasted_iota(jnp.int32, (NSEL, TQ), 0)
    tb = (t0 + lax.broadcasted_iota(jnp.int32, (NSEL, TQ), 1)) >> 6
    forced = (j == 0) | (j == tb) | (j == tb - 1)
    causal = j <= tb
    sc = jnp.where(causal, imp + jnp.where(forced, FORCE_BONUS, 0.0), NEG)
    rank = jnp.zeros((NSEL, TQ), F32)
    for i in range(NSEL):
        si = sc[i:i + 1, :]
        rank = rank + jnp.where(j > i, jnp.where(si >= sc, 1.0, 0.0), jnp.where(si > sc, 1.0, 0.0))
    sel = jnp.where((rank < SEL_TOPN) & causal, 1.0, 0.0)

    selpad = jnp.concatenate([sel, jnp.zeros((HD - NSEL, TQ), F32)], 0).astype(BF16)
    eye = jnp.where(lax.broadcasted_iota(jnp.int32, (TQ, TQ), 0)
                    == lax.broadcasted_iota(jnp.int32, (TQ, TQ), 1), 1.0, 0.0).astype(BF16)
    sel_t = _dot_nt(eye, selpad)
    lane = lax.broadcasted_iota(jnp.int32, (TQ, HD), 1)
    bias = jnp.where((lane < NSEL) & (sel_t < 0.5), NEG, 0.0).astype(BF16)
    for h in range(R):
        qa_ref[h * TQ:(h + 1) * TQ, HD:2 * HD] = bias

    m_ref[...] = jnp.full(m_ref.shape, NEG, F32)
    acc_ref[...] = jnp.zeros(acc_ref.shape, F32)

    def sel_body(i, carry):
        off = pl.multiple_of(i * (2 * TQ), 2 * TQ)
        _online_tile(qa_ref, kaug_ref[pl.ds(off, 2 * TQ), :], vsaug_ref[pl.ds(off, 2 * TQ), :], None,
                     m_ref, acc_ref, chunk=TQ)
        return carry

    lax.fori_loop(0, qi >> 1, sel_body, 0)

    @pl.when((qi & 1) == 1)
    def _():
        off = pl.multiple_of(t0 - TQ, TQ)
        _online_tile(qa_ref, kaug_ref[pl.ds(off, 2 * TQ), :], vsaug_ref[pl.ds(off, 2 * TQ), :],
                     wb_ref[1, :, 0:2 * TQ], m_ref, acc_ref, chunk=TQ)

    @pl.when((qi & 1) == 0)
    def _():
        off = pl.multiple_of(t0, TQ)
        _online_tile(qa_ref, kaug_ref[pl.ds(off, TQ), :], vsaug_ref[pl.ds(off, TQ), :],
                     wb_ref[0, :, 0:TQ], m_ref, acc_ref, chunk=TQ)

    offw = pl.multiple_of(jnp.maximum(t0 - WINDOW, 0), TQ)
    kwin = kw_ref[0, pl.ds(offw, WSPAN), :]
    vwin = vwaug_ref[pl.ds(offw, WSPAN), :]
    wbias = wb_ref[jnp.minimum(qi, 2)]
    gt = _sigmoid(gt_ref[0])
    for h in range(R):
        rows = slice(h * TQ, (h + 1) * TQ)
        sw = _dot_nt(qa_ref[rows, 0:HD], kwin) + wbias
        pw = jnp.exp2(sw - jnp.max(sw, -1, keepdims=True))
        ow = _dot(pw.astype(BF16), vwin)
        o = (gt[:, 3 * h:3 * h + 1] * oc_ref[rows, :]
             + gt[:, 3 * h + 1:3 * h + 2] * (acc_ref[rows, 0:HD] / acc_ref[rows, HD:2 * HD])
             + gt[:, 3 * h + 2:3 * h + 3] * (ow[:, 0:HD] / ow[:, HD:2 * HD]))
        o_ref[0, :, h * HD:(h + 1) * HD] = (o * ng_ref[0, :, h * HD:(h + 1) * HD].astype(F32)).astype(o_ref.dtype)


def _nsa(P3, F3, kcv, *, q_blk4, ks_blk, vs_blk, kw_blk, vw_blk, ng_blk4, gt_blk):
    B, S, _ = P3.shape
    G, R, TQ = NSA_GROUPS, NSA_REP, NSA_TQ
    assert WINDOW == 2 * TQ and SEL_BLOCK == 64 and S % (2 * TQ) == 0 and S >= WINDOW + TQ
    M = kcv.shape[2]
    kv_spec = lambda blk: pl.BlockSpec((1, S, HD), lambda b, g, i: (b, 0, blk + g))
    return pl.pallas_call(
        _nsa_kernel,
        grid=(B, G, S // TQ),
        in_specs=[pl.BlockSpec((1, TQ, R * HD), lambda b, g, i: (b, i, q_blk4 + g)),
                  kv_spec(ks_blk), kv_spec(vs_blk), kv_spec(kw_blk), kv_spec(vw_blk),
                  pl.BlockSpec((1, 1, M, HD), lambda b, g, i: (b, g, 0, 0)),
                  pl.BlockSpec((1, 1, M, HD), lambda b, g, i: (b, G + g, 0, 0)),
                  pl.BlockSpec((1, TQ, LANES), lambda b, g, i: (b, i, gt_blk + g)),
                  pl.BlockSpec((1, TQ, R * HD), lambda b, g, i: (b, i, ng_blk4 + g))],
        out_specs=pl.BlockSpec((1, TQ, R * HD), lambda b, g, i: (b, i, g)),
        out_shape=jax.ShapeDtypeStruct((B, S, NSA_HEADS * HD), BF16),
        scratch_shapes=[pltpu.VMEM((R * TQ, 2 * HD), BF16),
                        pltpu.VMEM((R * TQ, LANES), F32),
                        pltpu.VMEM((R * TQ, 2 * HD), F32),
                        pltpu.VMEM((R * TQ, HD), F32),
                        pltpu.VMEM((S, 2 * HD), BF16),
                        pltpu.VMEM((S, 2 * HD), BF16),
                        pltpu.VMEM((S, 2 * HD), BF16),
                        pltpu.VMEM((3, TQ, WINDOW + TQ), F32)],
        compiler_params=pltpu.CompilerParams(
            dimension_semantics=("arbitrary", "arbitrary", "arbitrary"), vmem_limit_bytes=VMEM_LIMIT),
        name="nsa",
    )(P3, P3, P3, P3, P3, kcv, kcv, F3, P3)


def _up_kernel(yr_ref, yn_ref, wr_ref, wn_ref, sa_ref, sb_ref, o_ref):
    a = _dot(yr_ref[...], wr_ref[...])
    b = _dot(yn_ref[...], wn_ref[...])
    o_ref[...] = (sa_ref[...].astype(F32) * a + sb_ref[...].astype(F32) * b).astype(o_ref.dtype)


def _up(y_ret, y_nsa, w_ur, w_un, P2, *, tm):
    T, W = y_ret.shape
    D = w_ur.shape[1]
    return pl.pallas_call(
        _up_kernel,
        grid=(T // tm,),
        in_specs=[pl.BlockSpec((tm, W), lambda i: (i, 0)),
                  pl.BlockSpec((tm, W), lambda i: (i, 0)),
                  pl.BlockSpec((W, D), lambda i: (0, 0)),
                  pl.BlockSpec((W, D), lambda i: (0, 0)),
                  pl.BlockSpec((tm, D), lambda i: (i, 0)),
                  pl.BlockSpec((tm, D), lambda i: (i, 1))],
        out_specs=pl.BlockSpec((tm, D), lambda i: (i, 0)),
        out_shape=jax.ShapeDtypeStruct((T, D), BF16),
        compiler_params=pltpu.CompilerParams(vmem_limit_bytes=VMEM_LIMIT),
        name="up",
    )(y_ret, y_nsa, w_ur, w_un, P2, P2)


def _out_kernel(m_ref, w_ref, x_ref, gate_ref, g_ref, o_ref, *, final_norm):
    y = x_ref[0] + gate_ref[0] * _dot(m_ref[0], w_ref[...])
    if final_norm:
        ms = jnp.mean(y * y, -1, keepdims=True)
        y = y * lax.rsqrt(ms + EPS) * g_ref[...]
    o_ref[0] = y


def _out(merged3, w_out, x, gate, g_final, *, ts, final_norm):
    B, S, D = x.shape
    return pl.pallas_call(
        functools.partial(_out_kernel, final_norm=final_norm),
        grid=(B, S // ts),
        in_specs=[pl.BlockSpec((1, ts, D), lambda b, i: (b, i, 0)),
                  pl.BlockSpec((D, D), lambda b, i: (0, 0)),
                  pl.BlockSpec((1, ts, D), lambda b, i: (b, i, 0)),
                  pl.BlockSpec((1, 1, D), lambda b, i: (b, 0, 0)),
                  pl.BlockSpec((1, D), lambda b, i: (0, 0))],
        out_specs=pl.BlockSpec((1, ts, D), lambda b, i: (b, i, 0)),
        out_shape=jax.ShapeDtypeStruct((B, S, D), F32),
        compiler_params=pltpu.CompilerParams(vmem_limit_bytes=VMEM_LIMIT),
        name="out",
    )(merged3, w_out, x, gate.reshape(B, 1, D), g_final.reshape(1, D))


def kernel(x, c, positions, w_ada, b_ada, g_norm, w_in, g_ret, w_ck1, w_ck2, pe_ck,
           w_cv1, w_cv2, pe_cv, w_up_ret, w_up_nsa, w_out, g_final):
    B, S, D = x.shape
    T = B * S
    depth = w_ada.shape[0]
    G = NSA_GROUPS
    RW = RET_HEADS * HD
    NW = NSA_HEADS * HD
    KVW = G * HD
    per_tile = PROJ_TN // LANES

    cols = (("r_q", RW, "rope", 1.0), ("r_k", RW, "rope", HD ** -0.5), ("r_v", RW, "plain", 1.0),
            ("r_g", RW, "silu", 1.0), ("n_q", NW, "rope", HD ** -0.5 * LOG2E),
            ("n_kc", KVW, "f32", 1.0), ("n_vc", KVW, "f32", 1.0),
            ("n_ks", KVW, "rope", 1.0), ("n_vs", KVW, "plain", 1.0),
            ("n_kw", KVW, "rope", 1.0), ("n_vw", KVW, "plain", 1.0), ("n_g", NW, "silu", 1.0))
    start = {}
    kinds = []
    off = 0
    for nm, wd, kind, scale in cols:
        start[nm] = off
        kinds += [(nm, kind, scale)] * (wd // LANES)
        off += wd
    start["n_bg"] = off
    start["m_a"] = off + NSA_HEADS * 3
    assert off % PROJ_TN == 0 and (2 * D) % PROJ_TN == 0

    n_sig = 2 * D // PROJ_TN
    tiles = [("sig", 1.0)] * n_sig
    src_tiles = [0] * n_sig
    blk = {"m_a": 0, "m_b": D // LANES}
    for t in range(off // PROJ_TN):
        group = kinds[t * per_tile:(t + 1) * per_tile]
        tkinds = [k for _, k, _ in group]
        if all(k == "f32" for k in tkinds):
            continue
        if len(set(tkinds)) == 1:
            assert len({s for _, _, s in group}) == 1
            tiles.append((tkinds[0], group[0][2]))
        else:
            half = per_tile // 2
            assert tkinds == ["rope"] * half + ["plain"] * half and all(s == 1.0 for _, _, s in group)
            tiles.append(("rope_plain", 1.0))
        for i, (nm, _, _) in enumerate(group):
            blk.setdefault(nm, (len(tiles) - 1) * per_tile + i)
        src_tiles.append(t)

    inv = jnp.exp(jnp.arange(0, HD, 2, dtype=F32) * (-math.log(ROPE_THETA) / HD))
    inv2 = jnp.concatenate([inv, inv]).reshape(1, HD)
    cos, sin = _rope_tables(positions.reshape(T, 1), inv2)
    cend = jnp.arange(S // CMP_STRIDE) * CMP_STRIDE + (CMP_BLOCK - 1)
    cend = jnp.minimum(cend, S - 1)
    cos_c = cos.reshape(B, S, HD)[:, cend]
    sin_c = sin.reshape(B, S, HD)[:, cend]
    log_gamma = jnp.log1p(-jnp.exp2(-5.0 - jnp.arange(RET_HEADS, dtype=F32)))

    for l in range(depth):
        w = w_in[l]
        w_sig = w[:, start["m_a"]:start["m_a"] + 2 * D].astype(BF16)
        bg = w[:, start["n_bg"]:start["n_bg"] + NSA_HEADS * 3].reshape(D, G, NSA_REP * 3)
        bg = jnp.pad(bg, ((0, 0), (0, 0), (0, LANES - NSA_REP * 3))).reshape(D, G * LANES)
        w_f = jnp.concatenate([w[:, start["n_kc"]:start["n_kc"] + 2 * KVW], bg], 1).astype(BF16)

        mod = _mod(c, w_ada[l], b_ada[l])
        shift, scl, gate = mod[:, :D], mod[:, D:2 * D], mod[:, 2 * D:]
        h2 = _h(x, g_norm[l], scl, shift).reshape(T, D)

        P2 = _proj(h2, w_sig, w_in, l, cos, sin, tiles=tiles, src_tiles=src_tiles, tm=min(PROJ_TM, T))
        kv, gts = _projf(h2, w_f, n_kv=2 * G, tm=min(1024, T))
        P3 = P2.reshape(B, S, -1)

        hps = RET_HPS
        y_ret = _retention(P3, log_gamma, g_ret[l], q_blk=blk["r_q"] // hps, k_blk=blk["r_k"] // hps,
                           v_blk=blk["r_v"] // hps, rg_blk=blk["r_g"] // hps)

        xh = kv.reshape(2 * G, B, S // CMP_STRIDE, CMP_STRIDE * HD)
        half = CMP_BLOCK // 2
        pe2 = jnp.stack([pe_ck[l].reshape(2, half * HD), pe_cv[l].reshape(2, half * HD)])
        w1s = jnp.stack([w_ck1[l], w_cv1[l]]).astype(BF16)
        w2s = jnp.stack([w_ck2[l], w_cv2[l]]).astype(BF16)
        kcv = _compress(xh, pe2, w1s, w2s, cos_c, sin_c)

        y_nsa = _nsa(P3, gts.reshape(B, S, G * LANES), kcv, q_blk4=blk["n_q"] // NSA_REP,
                     ks_blk=blk["n_ks"], vs_blk=blk["n_vs"], kw_blk=blk["n_kw"], vw_blk=blk["n_vw"],
                     ng_blk4=blk["n_g"] // NSA_REP, gt_blk=0)

        merged = _up(y_ret.reshape(T, RW), y_nsa.reshape(T, NW), w_up_ret[l].astype(BF16),
                     w_up_nsa[l].astype(BF16), P2, tm=min(512, T))
        x = _out(merged.reshape(B, S, D), w_out[l].astype(BF16), x, gate, g_final,
                 ts=min(256, S), final_norm=(l + 1 == depth))
    return x
```

```python
import functools
import math

import jax
import jax.numpy as jnp
from jax import lax
from jax.experimental import pallas as pl
from jax.experimental.pallas import tpu as pltpu

F32 = jnp.float32
BF16 = jnp.bfloat16

HD = 128
RET_HEADS = 8
NSA_HEADS = 8
NSA_GROUPS = 2
NSA_REP = NSA_HEADS // NSA_GROUPS
CMP_BLOCK = 32
CMP_STRIDE = 16
CMP_HIDDEN = 256
SEL_BLOCK = 64
SEL_TOPN = 16
WINDOW = 512
FORCE_BONUS = 1.0e4
ROPE_THETA = 10000.0
EPS = 1e-6
NEG = -1.0e30

LANES = 128
PROJ_TN = 512
PROJ_TM = 2048
PROJ_RC = 512
RET_CHUNK = 256
RET_HPS = 4
NSA_TQ = 256
LOG2E = math.log2(math.e)
VMEM_LIMIT = 56 * 1024 * 1024


def _sigmoid(v):
    return 1.0 / (1.0 + jnp.exp(-v))


def _dot(a, b):
    return jnp.dot(a, b, preferred_element_type=F32)


def _dot_nt(a, b):
    return lax.dot_general(a, b, (((1,), (1,)), ((), ())), preferred_element_type=F32)


def _dot_tn(a, b):
    return lax.dot_general(a, b, (((0,), (0,)), ((), ())), preferred_element_type=F32)


def _rope(v, cos, sin_signed):
    return v * cos + pltpu.roll(v, HD // 2, 1) * sin_signed


def _mod_kernel(c_ref, w_ref, b_ref, o_ref):
    c = c_ref[...]
    sc = c * _sigmoid(c)
    o_ref[...] = jnp.dot(sc, w_ref[...], preferred_element_type=F32,
                         precision=lax.Precision.HIGHEST) + b_ref[...]


def _mod(c, w_ada, b_ada):
    B, D = c.shape
    N = w_ada.shape[1]
    tn = math.gcd(N, 512)
    return pl.pallas_call(
        _mod_kernel,
        grid=(N // tn,),
        in_specs=[pl.BlockSpec((B, D), lambda j: (0, 0)),
                  pl.BlockSpec((D, tn), lambda j: (0, j)),
                  pl.BlockSpec((1, tn), lambda j: (0, j))],
        out_specs=pl.BlockSpec((B, tn), lambda j: (0, j)),
        out_shape=jax.ShapeDtypeStruct((B, N), F32),
        name="mod",
    )(c, w_ada, b_ada.reshape(1, N))


def _cs_kernel(pos_ref, inv_ref, cos_ref, sin_ref):
    ang = pos_ref[...].astype(F32) * inv_ref[...]
    lane = lax.broadcasted_iota(jnp.int32, (1, HD), 1)
    cos_ref[...] = jnp.cos(ang)
    sin_ref[...] = jnp.where(lane < HD // 2, -1.0, 1.0) * jnp.sin(ang)


def _rope_tables(pos_col, inv2):
    T = pos_col.shape[0]
    tm = 1024
    return pl.pallas_call(
        _cs_kernel,
        grid=(T // tm,),
        in_specs=[pl.BlockSpec((tm, 1), lambda i: (i, 0)),
                  pl.BlockSpec((1, HD), lambda i: (0, 0))],
        out_specs=[pl.BlockSpec((tm, HD), lambda i: (i, 0)),
                   pl.BlockSpec((tm, HD), lambda i: (i, 0))],
        out_shape=[jax.ShapeDtypeStruct((T, HD), F32)] * 2,
        name="cs",
    )(pos_col, inv2)


def _h_kernel(x_ref, g_ref, scl_ref, sh_ref, h_ref):
    x = x_ref[0]
    ms = jnp.mean(x * x, -1, keepdims=True)
    y = x * lax.rsqrt(ms + EPS) * g_ref[...]
    h_ref[0] = (y * (1.0 + scl_ref[0]) + sh_ref[0]).astype(BF16)


def _h(x, g_norm, scl, shift):
    B, S, D = x.shape
    ts = 512
    return pl.pallas_call(
        _h_kernel,
        grid=(B, S // ts),
        in_specs=[pl.BlockSpec((1, ts, D), lambda b, i: (b, i, 0)),
                  pl.BlockSpec((1, D), lambda b, i: (0, 0)),
                  pl.BlockSpec((1, 1, D), lambda b, i: (b, 0, 0)),
                  pl.BlockSpec((1, 1, D), lambda b, i: (b, 0, 0))],
        out_specs=pl.BlockSpec((1, ts, D), lambda b, i: (b, i, 0)),
        out_shape=jax.ShapeDtypeStruct((B, S, D), BF16),
        name="h",
    )(x, g_norm.reshape(1, D), scl.reshape(B, 1, D), shift.reshape(B, 1, D))


def _tile_cond(j, idxs):
    cond = None
    run_lo = prev = None
    for t in list(idxs) + [None]:
        if run_lo is None:
            run_lo = prev = t
            continue
        if t is not None and t == prev + 1:
            prev = t
            continue
        c = (j >= run_lo) & (j <= prev)
        cond = c if cond is None else (cond | c)
        run_lo = prev = t
    return cond


def _proj_kernel(h_ref, ws_ref, w_ref, cos_ref, sin_ref, o_ref, *, tiles):
    j = pl.program_id(1)
    tm = h_ref.shape[0]
    rc = min(PROJ_RC, tm)
    half = PROJ_TN // 2

    def row_chunks(w, epilogue):
        for c in range(tm // rc):
            rows = slice(c * rc, (c + 1) * rc)
            epilogue(rows, _dot_nt(h_ref[rows, :], w))

    def of_kind(kind):
        return [t for t, (k, _) in enumerate(tiles) if k == kind]

    def rope_cols(rows, acc, cos, sin, lo, hi):
        for hh in range(lo // HD, hi // HD):
            sl = slice(hh * HD, (hh + 1) * HD)
            o_ref[rows, sl] = _rope(acc[:, sl], cos, sin).astype(o_ref.dtype)

    @pl.when(_tile_cond(j, of_kind("sig")))
    def _():
        def epi(rows, acc):
            o_ref[rows, :] = _sigmoid(acc).astype(o_ref.dtype)
        row_chunks(ws_ref[...].astype(BF16), epi)

    @pl.when(_tile_cond(j, of_kind("rope")))
    def _():
        f = jnp.float32(1.0)
        for scale in sorted({s for k, s in tiles if k == "rope" and s != 1.0}):
            f = jnp.where(_tile_cond(j, [t for t, (k, s) in enumerate(tiles) if k == "rope" and s == scale]),
                          jnp.float32(scale), f)

        def epi(rows, acc):
            rope_cols(rows, acc, cos_ref[rows, :] * f, sin_ref[rows, :] * f, 0, PROJ_TN)
        row_chunks(w_ref[...].astype(BF16), epi)

    @pl.when(_tile_cond(j, of_kind("rope_plain")))
    def _():
        def epi(rows, acc):
            rope_cols(rows, acc, cos_ref[rows, :], sin_ref[rows, :], 0, half)
            o_ref[rows, half:] = acc[:, half:].astype(o_ref.dtype)
        row_chunks(w_ref[...].astype(BF16), epi)

    @pl.when(_tile_cond(j, of_kind("silu")))
    def _():
        def epi(rows, acc):
            o_ref[rows, :] = (acc * _sigmoid(acc)).astype(o_ref.dtype)
        row_chunks(w_ref[...].astype(BF16), epi)

    @pl.when(_tile_cond(j, of_kind("plain")))
    def _():
        def epi(rows, acc):
            o_ref[rows, :] = acc.astype(o_ref.dtype)
        row_chunks(w_ref[...].astype(BF16), epi)


def _proj(h2, w_sig_t, w_in_t, layer, cos, sin, *, tiles, src_tiles, tm):
    T, D = h2.shape
    n_sig = w_sig_t.shape[0] // PROJ_TN
    n_tiles = len(tiles)
    assert all(src_tiles[t + 1] - src_tiles[t] in (1, 2) for t in range(n_sig, n_tiles - 1))
    skips = [t for t in range(n_sig, n_tiles - 1) if src_tiles[t + 1] - src_tiles[t] == 2]

    def src(j):
        s = jnp.maximum(j - n_sig, 0) + src_tiles[n_sig]
        for t in skips:
            s = s + jnp.where(j > t, 1, 0)
        return s

    kern = functools.partial(_proj_kernel, tiles=tuple(tiles))
    return pl.pallas_call(
        kern,
        grid=(T // tm, n_tiles),
        in_specs=[pl.BlockSpec((tm, D), lambda i, j: (i, 0)),
                  pl.BlockSpec((PROJ_TN, D), lambda i, j: (jnp.minimum(j, n_sig - 1), 0)),
                  pl.BlockSpec((None, PROJ_TN, D), lambda i, j: (layer, src(j), 0)),
                  pl.BlockSpec((tm, HD), lambda i, j: (i, 0)),
                  pl.BlockSpec((tm, HD), lambda i, j: (i, 0))],
        out_specs=pl.BlockSpec((tm, PROJ_TN), lambda i, j: (i, j)),
        out_shape=jax.ShapeDtypeStruct((T, n_tiles * PROJ_TN), BF16),
        compiler_params=pltpu.CompilerParams(
            dimension_semantics=("arbitrary", "arbitrary"), vmem_limit_bytes=VMEM_LIMIT),
        name="proj",
    )(h2, w_sig_t, w_in_t, cos, sin)


def _projf_kernel(h_ref, w_ref, kv_ref, gt_ref):
    n_kv = kv_ref.shape[0]
    acc = _dot_nt(h_ref[...], w_ref[...])
    for c in range(n_kv):
        kv_ref[c] = acc[:, c * HD:(c + 1) * HD]
    gt_ref[...] = acc[:, n_kv * HD:]


def _projf(h2, w_f_t, *, n_kv, tm):
    T, D = h2.shape
    N = w_f_t.shape[0]
    return pl.pallas_call(
        _projf_kernel,
        grid=(T // tm,),
        in_specs=[pl.BlockSpec((tm, D), lambda i: (i, 0)),
                  pl.BlockSpec((N, D), lambda i: (0, 0))],
        out_specs=[pl.BlockSpec((n_kv, tm, HD), lambda i: (0, i, 0)),
                   pl.BlockSpec((tm, N - n_kv * HD), lambda i: (i, 0))],
        out_shape=[jax.ShapeDtypeStruct((n_kv, T, HD), F32),
                   jax.ShapeDtypeStruct((T, N - n_kv * HD), F32)],
        compiler_params=pltpu.CompilerParams(vmem_limit_bytes=VMEM_LIMIT),
        name="projf",
    )(h2, w_f_t)


def _ret_kernel(lg_ref, q_ref, k_ref, v_ref, rg_ref, g_ref, o_ref):
    S = q_ref.shape[1]
    C = RET_CHUNK
    r = lax.broadcasted_iota(jnp.int32, (C, C), 0)
    c = lax.broadcasted_iota(jnp.int32, (C, C), 1)
    diff = (r - c).astype(F32)
    rowf = lax.broadcasted_iota(jnp.int32, (C, HD), 0).astype(F32)
    for hh in range(RET_HPS):
        cols = slice(hh * HD, (hh + 1) * HD)
        lg = lg_ref[pl.program_id(1) * RET_HPS + hh]
        dmask = jnp.where(diff >= 0, jnp.exp(lg * jnp.maximum(diff, 0.0)), 0.0)
        zeta = jnp.exp(lg * (C - 1 - rowf))
        xi = jnp.exp(lg * (rowf + 1.0))
        decay = jnp.exp(jnp.full((HD, HD), lg * C, F32))
        g = g_ref[:, cols]
        state = jnp.zeros((HD, HD), F32)
        for n in range(S // C):
            sl = slice(n * C, (n + 1) * C)
            q = q_ref[0, sl, cols]
            k = k_ref[0, sl, cols]
            v = v_ref[0, sl, cols]
            sc = _dot_nt(q, k) * dmask
            o = _dot(sc.astype(BF16), v)
            if n > 0:
                o = o + _dot(q, state.astype(BF16)) * xi
            if n + 1 < S // C:
                kz = (k.astype(F32) * zeta).astype(BF16)
                state = state * decay + _dot_tn(kz, v)
            mu = jnp.mean(o, -1, keepdims=True)
            d = o - mu
            var = jnp.mean(d * d, -1, keepdims=True)
            on = d * lax.rsqrt(var + EPS)
            o_ref[0, sl, cols] = (on * g * rg_ref[0, sl, cols].astype(F32)).astype(o_ref.dtype)


def _retention(P3, log_gamma, g_ret, *, q_blk, k_blk, v_blk, rg_blk):
    B, S, _ = P3.shape
    H = RET_HEADS
    W = RET_HPS * HD

    def col(off):
        return lambda b, h, lg: (b, 0, off + h)

    return pl.pallas_call(
        _ret_kernel,
        grid_spec=pltpu.PrefetchScalarGridSpec(
            num_scalar_prefetch=1,
            grid=(B, H // RET_HPS),
            in_specs=[pl.BlockSpec((1, S, W), col(q_blk)),
                      pl.BlockSpec((1, S, W), col(k_blk)),
                      pl.BlockSpec((1, S, W), col(v_blk)),
                      pl.BlockSpec((1, S, W), col(rg_blk)),
                      pl.BlockSpec((1, W), lambda b, h, lg: (0, h))],
            out_specs=pl.BlockSpec((1, S, W), lambda b, h, lg: (b, 0, h)),
        ),
        out_shape=jax.ShapeDtypeStruct((B, S, H * HD), BF16),
        compiler_params=pltpu.CompilerParams(vmem_limit_bytes=VMEM_LIMIT),
        name="ret",
    )(log_gamma, P3, P3, P3, P3, g_ret.reshape(1, H * HD))


def _cmp_kernel(x_ref, pe_ref, w1_ref, w2_ref, cos_ref, sin_ref, o_ref):
    half = (CMP_BLOCK // 2) * HD
    M = x_ref.shape[2] // CMP_STRIDE
    x = jnp.concatenate([x_ref[0, 0, pl.ds(p, M, stride=CMP_STRIDE), :] for p in range(CMP_STRIDE)], 1)
    xa = (x + pe_ref[0, 0:1, :]).astype(BF16)
    xb = (x + pe_ref[0, 1:2, :]).astype(BF16)
    a = _dot(xa, w1_ref[0, :half, :])
    b = _dot(xb, w1_ref[0, half:, :])
    nrow = a.shape[0]
    pre = a + pltpu.roll(b, nrow - 1, 0)
    hid = pre * _sigmoid(pre)
    out = _dot(hid.astype(BF16), w2_ref[0])
    roped = _rope(out, cos_ref[0], sin_ref[0])
    is_key = pl.program_id(1) < NSA_GROUPS
    o_ref[0, 0] = jnp.where(is_key, roped, out).astype(o_ref.dtype)


def _compress(xh, pe2, w1s, w2s, cos_c, sin_c):
    NC, B, S, _ = xh.shape
    G = NSA_GROUPS
    M = S // CMP_STRIDE
    W = CMP_STRIDE * HD
    return pl.pallas_call(
        _cmp_kernel,
        grid=(B, NC),
        in_specs=[pl.BlockSpec((1, 1, S, HD), lambda b, c: (c, b, 0, 0)),
                  pl.BlockSpec((1, 2, W), lambda b, c: (c // G, 0, 0)),
                  pl.BlockSpec((1, 2 * W, CMP_HIDDEN), lambda b, c: (c // G, 0, 0)),
                  pl.BlockSpec((1, CMP_HIDDEN, HD), lambda b, c: (c // G, 0, 0)),
                  pl.BlockSpec((1, M, HD), lambda b, c: (b, 0, 0)),
                  pl.BlockSpec((1, M, HD), lambda b, c: (b, 0, 0))],
        out_specs=pl.BlockSpec((1, 1, M, HD), lambda b, c: (b, c, 0, 0)),
        out_shape=jax.ShapeDtypeStruct((B, NC, M, HD), BF16),
        name="cmp",
    )(xh, pe2, w1s, w2s, cos_c, sin_c)


def _lane_tile(v, n):
    return v if n == 1 else jnp.concatenate([v] * n, 1)


def _online_tile(q_ref, k, v_aug, bias, m_ref, acc_ref, *, chunk):
    for c in range(q_ref.shape[0] // chunk):
        rows = slice(c * chunk, (c + 1) * chunk)
        s = _dot_nt(q_ref[rows, :], k)
        if bias is not None:
            s = s + bias
        m_prev = m_ref[rows, :]
        m_new = jnp.maximum(m_prev, jnp.max(s, -1, keepdims=True))
        alpha = jnp.exp2(m_prev - m_new)
        p = jnp.exp2(s - _lane_tile(m_new, s.shape[1] // LANES))
        acc_ref[rows, :] = _lane_tile(alpha, 2) * acc_ref[rows, :] + _dot(p.astype(BF16), v_aug)
        m_ref[rows, :] = m_new


def _nsa_kernel(q_ref, ks_ref, vs_ref, kw_ref, vw_ref, kc_ref, vc_ref, gt_ref, ng_ref, o_ref,
                qa_ref, m_ref, acc_ref, oc_ref, kaug_ref, vsaug_ref, vwaug_ref, wb_ref):
    TQ = q_ref.shape[1]
    S = ks_ref.shape[1]
    R = NSA_REP
    RT = R * TQ
    WSPAN = WINDOW + TQ
    NSEL = S // SEL_BLOCK
    qi = pl.program_id(2)
    t0 = qi * TQ

    @pl.when(qi == 0)
    def _():
        blk = lax.broadcasted_iota(jnp.int32, (S, HD), 0) >> 6
        onehot = jnp.where(blk == lax.broadcasted_iota(jnp.int32, (S, HD), 1), 1.0, 0.0)
        kaug_ref[:, 0:HD] = ks_ref[0]
        kaug_ref[:, HD:2 * HD] = onehot.astype(BF16)
        ones = jnp.ones((S, HD), BF16)
        vsaug_ref[:, 0:HD] = vs_ref[0]
        vsaug_ref[:, HD:2 * HD] = ones
        vwaug_ref[:, 0:HD] = vw_ref[0]
        vwaug_ref[:, HD:2 * HD] = ones
        r = lax.broadcasted_iota(jnp.int32, (TQ, WSPAN), 0)
        c = lax.broadcasted_iota(jnp.int32, (TQ, WSPAN), 1)
        wb_ref[0] = jnp.where(c <= r, 0.0, NEG)
        wb_ref[1] = jnp.where(c <= r + TQ, 0.0, NEG)
        wb_ref[2] = jnp.where((c > r) & (c <= r + WINDOW), 0.0, NEG)

    for h in range(R):
        qa_ref[h * TQ:(h + 1) * TQ, 0:HD] = q_ref[0, :, h * HD:(h + 1) * HD]
    q4 = qa_ref[:, 0:HD]

    s = _dot_nt(q4, kc_ref[0, 0])
    trow = t0 + (lax.broadcasted_iota(jnp.int32, (RT, HD), 0) & (TQ - 1))
    ncol = lax.broadcasted_iota(jnp.int32, (RT, HD), 1)
    s = jnp.where(ncol * CMP_STRIDE + (CMP_BLOCK - 1) <= trow, s, NEG)
    mx = jnp.max(s, -1, keepdims=True)
    p = jnp.exp2(s - mx)
    tcol = t0 + (lax.broadcasted_iota(jnp.int32, (RT, 1), 0) & (TQ - 1))
    any_valid = jnp.where(tcol >= CMP_BLOCK - 1, 1.0, 0.0)
    p = p * (any_valid / jnp.sum(p, -1, keepdims=True))
    oc_ref[...] = _dot(p.astype(BF16), vc_ref[0, 0])

    psum = p[0:TQ]
    for h in range(1, R):
        psum = psum + p[h * TQ:(h + 1) * TQ]
    p_hi = psum.astype(BF16)
    rem = psum - p_hi.astype(F32)
    p_mid = rem.astype(BF16)
    p_lo = (rem - p_mid.astype(F32)).astype(BF16)
    jj = lax.broadcasted_iota(jnp.int32, (NSEL, HD), 0)
    nn = lax.broadcasted_iota(jnp.int32, (NSEL, HD), 1)
    ov = jnp.maximum(jnp.minimum(nn * CMP_STRIDE + CMP_BLOCK, (jj + 1) * SEL_BLOCK)
                     - jnp.maximum(nn * CMP_STRIDE, jj * SEL_BLOCK), 0).astype(F32) * (1.0 / CMP_BLOCK)
    ov = ov.astype(BF16)
    imp = (_dot_nt(ov, p_lo) + _dot_nt(ov, p_mid)) + _dot_nt(ov, p_hi)

    j = lax.broadcasted_iota(jnp.int32, (NSEL, TQ), 0)
    tb = (t0 + lax.broadcasted_iota(jnp.int32, (NSEL, TQ), 1)) >> 6
    forced = (j == 0) | (j == tb) | (j == tb - 1)
    causal = j <= tb
    sc = jnp.where(causal, imp + jnp.where(forced, FORCE_BONUS, 0.0), NEG)
    rank = jnp.zeros((NSEL, TQ), F32)
    for i in range(NSEL):
        si = sc[i:i + 1, :]
        rank = rank + jnp.where(j > i, jnp.where(si >= sc, 1.0, 0.0), jnp.where(si > sc, 1.0, 0.0))
    sel = jnp.where((rank < SEL_TOPN) & causal, 1.0, 0.0)

    selpad = jnp.concatenate([sel, jnp.zeros((HD - NSEL, TQ), F32)], 0).astype(BF16)
    eye = jnp.where(lax.broadcasted_iota(jnp.int32, (TQ, TQ), 0)
                    == lax.broadcasted_iota(jnp.int32, (TQ, TQ), 1), 1.0, 0.0).astype(BF16)
    sel_t = _dot_nt(eye, selpad)
    lane = lax.broadcasted_iota(jnp.int32, (TQ, HD), 1)
    bias = jnp.where((lane < NSEL) & (sel_t < 0.5), NEG, 0.0).astype(BF16)
    for h in range(R):
        qa_ref[h * TQ:(h + 1) * TQ, HD:2 * HD] = bias

    m_ref[...] = jnp.full(m_ref.shape, NEG, F32)
    acc_ref[...] = jnp.zeros(acc_ref.shape, F32)

    def sel_body(i, carry):
        off = pl.multiple_of(i * (2 * TQ), 2 * TQ)
        _online_tile(qa_ref, kaug_ref[pl.ds(off, 2 * TQ), :], vsaug_ref[pl.ds(off, 2 * TQ), :], None,
                     m_ref, acc_ref, chunk=TQ)
        return carry

    lax.fori_loop(0, qi >> 1, sel_body, 0)

    @pl.when((qi & 1) == 1)
    def _():
        off = pl.multiple_of(t0 - TQ, TQ)
        _online_tile(qa_ref, kaug_ref[pl.ds(off, 2 * TQ), :], vsaug_ref[pl.ds(off, 2 * TQ), :],
                     wb_ref[1, :, 0:2 * TQ], m_ref, acc_ref, chunk=TQ)

    @pl.when((qi & 1) == 0)
    def _():
        off = pl.multiple_of(t0, TQ)
        _online_tile(qa_ref, kaug_ref[pl.ds(off, TQ), :], vsaug_ref[pl.ds(off, TQ), :],
                     wb_ref[0, :, 0:TQ], m_ref, acc_ref, chunk=TQ)

    offw = pl.multiple_of(jnp.maximum(t0 - WINDOW, 0), TQ)
    kwin = kw_ref[0, pl.ds(offw, WSPAN), :]
    vwin = vwaug_ref[pl.ds(offw, WSPAN), :]
    wbias = wb_ref[jnp.minimum(qi, 2)]
    gt = _sigmoid(gt_ref[0])
    for h in range(R):
        rows = slice(h * TQ, (h + 1) * TQ)
        sw = _dot_nt(qa_ref[rows, 0:HD], kwin) + wbias
        pw = jnp.exp2(sw - jnp.max(sw, -1, keepdims=True))
        ow = _dot(pw.astype(BF16), vwin)
        o = (gt[:, 3 * h:3 * h + 1] * oc_ref[rows, :]
             + gt[:, 3 * h + 1:3 * h + 2] * (acc_ref[rows, 0:HD] / acc_ref[rows, HD:2 * HD])
             + gt[:, 3 * h + 2:3 * h + 3] * (ow[:, 0:HD] / ow[:, HD:2 * HD]))
        o_ref[0, :, h * HD:(h + 1) * HD] = (o * ng_ref[0, :, h * HD:(h + 1) * HD].astype(F32)).astype(o_ref.dtype)


def _nsa(P3, F3, kcv, *, q_blk4, ks_blk, vs_blk, kw_blk, vw_blk, ng_blk4, gt_blk):
    B, S, _ = P3.shape
    G, R, TQ = NSA_GROUPS, NSA_REP, NSA_TQ
    assert WINDOW == 2 * TQ and SEL_BLOCK == 64 and S % (2 * TQ) == 0 and S >= WINDOW + TQ
    M = kcv.shape[2]
    kv_spec = lambda blk: pl.BlockSpec((1, S, HD), lambda b, g, i: (b, 0, blk + g))
    return pl.pallas_call(
        _nsa_kernel,
        grid=(B, G, S // TQ),
        in_specs=[pl.BlockSpec((1, TQ, R * HD), lambda b, g, i: (b, i, q_blk4 + g)),
                  kv_spec(ks_blk), kv_spec(vs_blk), kv_spec(kw_blk), kv_spec(vw_blk),
                  pl.BlockSpec((1, 1, M, HD), lambda b, g, i: (b, g, 0, 0)),
                  pl.BlockSpec((1, 1, M, HD), lambda b, g, i: (b, G + g, 0, 0)),
                  pl.BlockSpec((1, TQ, LANES), lambda b, g, i: (b, i, gt_blk + g)),
                  pl.BlockSpec((1, TQ, R * HD), lambda b, g, i: (b, i, ng_blk4 + g))],
        out_specs=pl.BlockSpec((1, TQ, R * HD), lambda b, g, i: (b, i, g)),
        out_shape=jax.ShapeDtypeStruct((B, S, NSA_HEADS * HD), BF16),
        scratch_shapes=[pltpu.VMEM((R * TQ, 2 * HD), BF16),
                        pltpu.VMEM((R * TQ, LANES), F32),
                        pltpu.VMEM((R * TQ, 2 * HD), F32),
                        pltpu.VMEM((R * TQ, HD), F32),
                        pltpu.VMEM((S, 2 * HD), BF16),
                        pltpu.VMEM((S, 2 * HD), BF16),
                        pltpu.VMEM((S, 2 * HD), BF16),
                        pltpu.VMEM((3, TQ, WINDOW + TQ), F32)],
        compiler_params=pltpu.CompilerParams(
            dimension_semantics=("arbitrary", "arbitrary", "arbitrary"), vmem_limit_bytes=VMEM_LIMIT),
        name="nsa",
    )(P3, P3, P3, P3, P3, kcv, kcv, F3, P3)


def _up_kernel(yr_ref, yn_ref, wr_ref, wn_ref, sa_ref, sb_ref, o_ref):
    a = _dot(yr_ref[...], wr_ref[...])
    b = _dot(yn_ref[...], wn_ref[...])
    o_ref[...] = (sa_ref[...].astype(F32) * a + sb_ref[...].astype(F32) * b).astype(o_ref.dtype)


def _up(y_ret, y_nsa, w_ur, w_un, P2, *, tm):
    T, W = y_ret.shape
    D = w_ur.shape[1]
    return pl.pallas_call(
        _up_kernel,
        grid=(T // tm,),
        in_specs=[pl.BlockSpec((tm, W), lambda i: (i, 0)),
                  pl.BlockSpec((tm, W), lambda i: (i, 0)),
                  pl.BlockSpec((W, D), lambda i: (0, 0)),
                  pl.BlockSpec((W, D), lambda i: (0, 0)),
                  pl.BlockSpec((tm, D), lambda i: (i, 0)),
                  pl.BlockSpec((tm, D), lambda i: (i, 1))],
        out_specs=pl.BlockSpec((tm, D), lambda i: (i, 0)),
        out_shape=jax.ShapeDtypeStruct((T, D), BF16),
        compiler_params=pltpu.CompilerParams(vmem_limit_bytes=VMEM_LIMIT),
        name="up",
    )(y_ret, y_nsa, w_ur, w_un, P2, P2)


def _out_kernel(m_ref, w_ref, x_ref, gate_ref, g_ref, o_ref, *, final_norm):
    y = x_ref[0] + gate_ref[0] * _dot(m_ref[0], w_ref[...])
    if final_norm:
        ms = jnp.mean(y * y, -1, keepdims=True)
        y = y * lax.rsqrt(ms + EPS) * g_ref[...]
    o_ref[0] = y


def _out(merged3, w_out, x, gate, g_final, *, ts, final_norm):
    B, S, D = x.shape
    return pl.pallas_call(
        functools.partial(_out_kernel, final_norm=final_norm),
        grid=(B, S // ts),
        in_specs=[pl.BlockSpec((1, ts, D), lambda b, i: (b, i, 0)),
                  pl.BlockSpec((D, D), lambda b, i: (0, 0)),
                  pl.BlockSpec((1, ts, D), lambda b, i: (b, i, 0)),
                  pl.BlockSpec((1, 1, D), lambda b, i: (b, 0, 0)),
                  pl.BlockSpec((1, D), lambda b, i: (0, 0))],
        out_specs=pl.BlockSpec((1, ts, D), lambda b, i: (b, i, 0)),
        out_shape=jax.ShapeDtypeStruct((B, S, D), F32),
        compiler_params=pltpu.CompilerParams(vmem_limit_bytes=VMEM_LIMIT),
        name="out",
    )(merged3, w_out, x, gate.reshape(B, 1, D), g_final.reshape(1, D))


def kernel(x, c, positions, w_ada, b_ada, g_norm, w_in, g_ret, w_ck1, w_ck2, pe_ck,
           w_cv1, w_cv2, pe_cv, w_up_ret, w_up_nsa, w_out, g_final):
    B, S, D = x.shape
    T = B * S
    depth = w_ada.shape[0]
    G = NSA_GROUPS
    RW = RET_HEADS * HD
    NW = NSA_HEADS * HD
    KVW = G * HD
    per_tile = PROJ_TN // LANES

    cols = (("r_q", RW, "rope", 1.0), ("r_k", RW, "rope", HD ** -0.5), ("r_v", RW, "plain", 1.0),
            ("r_g", RW, "silu", 1.0), ("n_q", NW, "rope", HD ** -0.5 * LOG2E),
            ("n_kc", KVW, "f32", 1.0), ("n_vc", KVW, "f32", 1.0),
            ("n_ks", KVW, "rope", 1.0), ("n_vs", KVW, "plain", 1.0),
            ("n_kw", KVW, "rope", 1.0), ("n_vw", KVW, "plain", 1.0), ("n_g", NW, "silu", 1.0))
    start = {}
    kinds = []
    off = 0
    for nm, wd, kind, scale in cols:
        start[nm] = off
        kinds += [(nm, kind, scale)] * (wd // LANES)
        off += wd
    start["n_bg"] = off
    start["m_a"] = off + NSA_HEADS * 3
    assert off % PROJ_TN == 0 and (2 * D) % PROJ_TN == 0

    n_sig = 2 * D // PROJ_TN
    tiles = [("sig", 1.0)] * n_sig
    src_tiles = [0] * n_sig
    blk = {"m_a": 0, "m_b": D // LANES}
    for t in range(off // PROJ_TN):
        group = kinds[t * per_tile:(t + 1) * per_tile]
        tkinds = [k for _, k, _ in group]
        if all(k == "f32" for k in tkinds):
            continue
        if len(set(tkinds)) == 1:
            assert len({s for _, _, s in group}) == 1
            tiles.append((tkinds[0], group[0][2]))
        else:
            half = per_tile // 2
            assert tkinds == ["rope"] * half + ["plain"] * half and all(s == 1.0 for _, _, s in group)
            tiles.append(("rope_plain", 1.0))
        for i, (nm, _, _) in enumerate(group):
            blk.setdefault(nm, (len(tiles) - 1) * per_tile + i)
        src_tiles.append(t)

    inv = jnp.exp(jnp.arange(0, HD, 2, dtype=F32) * (-math.log(ROPE_THETA) / HD))
    inv2 = jnp.concatenate([inv, inv]).reshape(1, HD)
    cos, sin = _rope_tables(positions.reshape(T, 1), inv2)
    cend = jnp.arange(S // CMP_STRIDE) * CMP_STRIDE + (CMP_BLOCK - 1)
    cend = jnp.minimum(cend, S - 1)
    cos_c = cos.reshape(B, S, HD)[:, cend]
    sin_c = sin.reshape(B, S, HD)[:, cend]
    log_gamma = jnp.log1p(-jnp.exp2(-5.0 - jnp.arange(RET_HEADS, dtype=F32)))
    w_in_t = jnp.swapaxes(w_in, 1, 2)

    for l in range(depth):
        wt = w_in_t[l]
        w_sig = wt[start["m_a"]:start["m_a"] + 2 * D]
        bg = wt[start["n_bg"]:start["n_bg"] + NSA_HEADS * 3].reshape(G, NSA_REP * 3, D)
        bg = jnp.pad(bg, ((0, 0), (0, LANES - NSA_REP * 3), (0, 0))).reshape(G * LANES, D)
        w_f = jnp.concatenate([wt[start["n_kc"]:start["n_kc"] + 2 * KVW], bg], 0).astype(BF16)

        mod = _mod(c, w_ada[l], b_ada[l])
        shift, scl, gate = mod[:, :D], mod[:, D:2 * D], mod[:, 2 * D:]
        h2 = _h(x, g_norm[l], scl, shift).reshape(T, D)

        P2 = _proj(h2, w_sig, w_in_t, l, cos, sin, tiles=tiles, src_tiles=src_tiles, tm=min(PROJ_TM, T))
        kv, gts = _projf(h2, w_f, n_kv=2 * G, tm=min(1024, T))
        P3 = P2.reshape(B, S, -1)

        hps = RET_HPS
        y_ret = _retention(P3, log_gamma, g_ret[l], q_blk=blk["r_q"] // hps, k_blk=blk["r_k"] // hps,
                           v_blk=blk["r_v"] // hps, rg_blk=blk["r_g"] // hps)

        xh = kv.reshape(2 * G, B, S, HD)
        half = CMP_BLOCK // 2
        pe2 = jnp.stack([pe_ck[l].reshape(2, half * HD), pe_cv[l].reshape(2, half * HD)])
        w1s = jnp.stack([w_ck1[l], w_cv1[l]]).astype(BF16)
        w2s = jnp.stack([w_ck2[l], w_cv2[l]]).astype(BF16)
        kcv = _compress(xh, pe2, w1s, w2s, cos_c, sin_c)

        y_nsa = _nsa(P3, gts.reshape(B, S, G * LANES), kcv, q_blk4=blk["n_q"] // NSA_REP,
                     ks_blk=blk["n_ks"], vs_blk=blk["n_vs"], kw_blk=blk["n_kw"], vw_blk=blk["n_vw"],
                     ng_blk4=blk["n_g"] // NSA_REP, gt_blk=0)

        merged = _up(y_ret.reshape(T, RW), y_nsa.reshape(T, NW), w_up_ret[l].astype(BF16),
                     w_up_nsa[l].astype(BF16), P2, tm=min(512, T))
        x = _out(merged.reshape(B, S, D), w_out[l].astype(BF16), x, gate, g_final,
                 ts=min(256, S), final_norm=(l + 1 == depth))
    return x
```

```python
import functools
import math

import jax
import jax.numpy as jnp
from jax import lax
from jax.experimental import pallas as pl
from jax.experimental.pallas import tpu as pltpu

F32 = jnp.float32
BF16 = jnp.bfloat16

HD = 128
RET_HEADS = 8
NSA_HEADS = 8
NSA_GROUPS = 2
NSA_REP = NSA_HEADS // NSA_GROUPS
CMP_BLOCK = 32
CMP_STRIDE = 16
CMP_HIDDEN = 256
SEL_BLOCK = 64
SEL_TOPN = 16
WINDOW = 512
FORCE_BONUS = 1.0e4
ROPE_THETA = 10000.0
EPS = 1e-6
NEG = -1.0e30

LANES = 128
PROJ_TN = 512
PROJ_TM = 2048
PROJ_RC = 512
RET_CHUNK = 256
RET_HPS = 4
NSA_TQ = 256
LOG2E = math.log2(math.e)
VMEM_LIMIT = 56 * 1024 * 1024


def _sigmoid(v):
    return 1.0 / (1.0 + jnp.exp(-v))


def _dot(a, b):
    return jnp.dot(a, b, preferred_element_type=F32)


def _dot_nt(a, b):
    return lax.dot_general(a, b, (((1,), (1,)), ((), ())), preferred_element_type=F32)


def _dot_tn(a, b):
    return lax.dot_general(a, b, (((0,), (0,)), ((), ())), preferred_element_type=F32)


def _rope(v, cos, sin_signed):
    return v * cos + pltpu.roll(v, HD // 2, 1) * sin_signed


def _mod_kernel(c_ref, w_ref, b_ref, o_ref):
    c = c_ref[...]
    sc = c * _sigmoid(c)
    o_ref[...] = jnp.dot(sc, w_ref[...], preferred_element_type=F32,
                         precision=lax.Precision.HIGHEST) + b_ref[...]


def _mod(c, w_ada, b_ada):
    B, D = c.shape
    N = w_ada.shape[1]
    tn = math.gcd(N, 512)
    return pl.pallas_call(
        _mod_kernel,
        grid=(N // tn,),
        in_specs=[pl.BlockSpec((B, D), lambda j: (0, 0)),
                  pl.BlockSpec((D, tn), lambda j: (0, j)),
                  pl.BlockSpec((1, tn), lambda j: (0, j))],
        out_specs=pl.BlockSpec((B, tn), lambda j: (0, j)),
        out_shape=jax.ShapeDtypeStruct((B, N), F32),
        name="mod",
    )(c, w_ada, b_ada.reshape(1, N))


def _cs_kernel(pos_ref, inv_ref, cos_ref, sin_ref):
    ang = pos_ref[...].astype(F32) * inv_ref[...]
    lane = lax.broadcasted_iota(jnp.int32, (1, HD), 1)
    cos_ref[...] = jnp.cos(ang)
    sin_ref[...] = jnp.where(lane < HD // 2, -1.0, 1.0) * jnp.sin(ang)


def _rope_tables(pos_col, inv2):
    T = pos_col.shape[0]
    tm = 1024
    return pl.pallas_call(
        _cs_kernel,
        grid=(T // tm,),
        in_specs=[pl.BlockSpec((tm, 1), lambda i: (i, 0)),
                  pl.BlockSpec((1, HD), lambda i: (0, 0))],
        out_specs=[pl.BlockSpec((tm, HD), lambda i: (i, 0)),
                   pl.BlockSpec((tm, HD), lambda i: (i, 0))],
        out_shape=[jax.ShapeDtypeStruct((T, HD), F32)] * 2,
        name="cs",
    )(pos_col, inv2)


def _h_kernel(x_ref, g_ref, scl_ref, sh_ref, h_ref):
    x = x_ref[0]
    ms = jnp.mean(x * x, -1, keepdims=True)
    y = x * lax.rsqrt(ms + EPS) * g_ref[...]
    h_ref[0] = (y * (1.0 + scl_ref[0]) + sh_ref[0]).astype(BF16)


def _h(x, g_norm, scl, shift):
    B, S, D = x.shape
    ts = 512
    return pl.pallas_call(
        _h_kernel,
        grid=(B, S // ts),
        in_specs=[pl.BlockSpec((1, ts, D), lambda b, i: (b, i, 0)),
                  pl.BlockSpec((1, D), lambda b, i: (0, 0)),
                  pl.BlockSpec((1, 1, D), lambda b, i: (b, 0, 0)),
                  pl.BlockSpec((1, 1, D), lambda b, i: (b, 0, 0))],
        out_specs=pl.BlockSpec((1, ts, D), lambda b, i: (b, i, 0)),
        out_shape=jax.ShapeDtypeStruct((B, S, D), BF16),
        name="h",
    )(x, g_norm.reshape(1, D), scl.reshape(B, 1, D), shift.reshape(B, 1, D))


def _tile_cond(j, idxs):
    cond = None
    run_lo = prev = None
    for t in list(idxs) + [None]:
        if run_lo is None:
            run_lo = prev = t
            continue
        if t is not None and t == prev + 1:
            prev = t
            continue
        c = (j >= run_lo) & (j <= prev)
        cond = c if cond is None else (cond | c)
        run_lo = prev = t
    return cond


def _proj_kernel(h_ref, ws_ref, w_ref, cos_ref, sin_ref, o_ref, *, tiles):
    j = pl.program_id(1)
    tm = h_ref.shape[0]
    rc = min(PROJ_RC, tm)
    half = PROJ_TN // 2

    def row_chunks(w, epilogue):
        for c in range(tm // rc):
            rows = slice(c * rc, (c + 1) * rc)
            epilogue(rows, _dot_nt(h_ref[rows, :], w))

    def of_kind(kind):
        return [t for t, (k, _) in enumerate(tiles) if k == kind]

    def rope_cols(rows, acc, cos, sin, lo, hi):
        for hh in range(lo // HD, hi // HD):
            sl = slice(hh * HD, (hh + 1) * HD)
            o_ref[rows, sl] = _rope(acc[:, sl], cos, sin).astype(o_ref.dtype)

    @pl.when(_tile_cond(j, of_kind("sig")))
    def _():
        def epi(rows, acc):
            o_ref[rows, :] = _sigmoid(acc).astype(o_ref.dtype)
        row_chunks(ws_ref[...].astype(BF16), epi)

    @pl.when(_tile_cond(j, of_kind("rope")))
    def _():
        f = jnp.float32(1.0)
        for scale in sorted({s for k, s in tiles if k == "rope" and s != 1.0}):
            f = jnp.where(_tile_cond(j, [t for t, (k, s) in enumerate(tiles) if k == "rope" and s == scale]),
                          jnp.float32(scale), f)

        def epi(rows, acc):
            rope_cols(rows, acc, cos_ref[rows, :] * f, sin_ref[rows, :] * f, 0, PROJ_TN)
        row_chunks(w_ref[...].astype(BF16), epi)

    @pl.when(_tile_cond(j, of_kind("rope_plain")))
    def _():
        def epi(rows, acc):
            rope_cols(rows, acc, cos_ref[rows, :], sin_ref[rows, :], 0, half)
            o_ref[rows, half:] = acc[:, half:].astype(o_ref.dtype)
        row_chunks(w_ref[...].astype(BF16), epi)

    @pl.when(_tile_cond(j, of_kind("silu")))
    def _():
        def epi(rows, acc):
            o_ref[rows, :] = (acc * _sigmoid(acc)).astype(o_ref.dtype)
        row_chunks(w_ref[...].astype(BF16), epi)

    @pl.when(_tile_cond(j, of_kind("plain")))
    def _():
        def epi(rows, acc):
            o_ref[rows, :] = acc.astype(o_ref.dtype)
        row_chunks(w_ref[...].astype(BF16), epi)


def _proj(h2, w_sig_t, w_in_t, layer, cos, sin, *, tiles, src_tiles, tm):
    T, D = h2.shape
    n_sig = w_sig_t.shape[0] // PROJ_TN
    n_tiles = len(tiles)
    assert all(src_tiles[t + 1] - src_tiles[t] in (1, 2) for t in range(n_sig, n_tiles - 1))
    skips = [t for t in range(n_sig, n_tiles - 1) if src_tiles[t + 1] - src_tiles[t] == 2]

    def src(j):
        s = jnp.maximum(j - n_sig, 0) + src_tiles[n_sig]
        for t in skips:
            s = s + jnp.where(j > t, 1, 0)
        return s

    kern = functools.partial(_proj_kernel, tiles=tuple(tiles))
    return pl.pallas_call(
        kern,
        grid=(T // tm, n_tiles),
        in_specs=[pl.BlockSpec((tm, D), lambda i, j: (i, 0)),
                  pl.BlockSpec((PROJ_TN, D), lambda i, j: (jnp.minimum(j, n_sig - 1), 0)),
                  pl.BlockSpec((None, PROJ_TN, D), lambda i, j: (layer, src(j), 0)),
                  pl.BlockSpec((tm, HD), lambda i, j: (i, 0)),
                  pl.BlockSpec((tm, HD), lambda i, j: (i, 0))],
        out_specs=pl.BlockSpec((tm, PROJ_TN), lambda i, j: (i, j)),
        out_shape=jax.ShapeDtypeStruct((T, n_tiles * PROJ_TN), BF16),
        compiler_params=pltpu.CompilerParams(
            dimension_semantics=("arbitrary", "arbitrary"), vmem_limit_bytes=VMEM_LIMIT),
        name="proj",
    )(h2, w_sig_t, w_in_t, cos, sin)


def _projf_kernel(h_ref, w_ref, kv_ref, gt_ref):
    n_kv = kv_ref.shape[0]
    acc = _dot_nt(h_ref[...], w_ref[...].astype(BF16))
    for c in range(n_kv):
        kv_ref[c] = acc[:, c * HD:(c + 1) * HD]
    gt_ref[...] = acc[:, n_kv * HD:]


def _projf(h2, w_f_t, *, n_kv, tm):
    T, D = h2.shape
    N = w_f_t.shape[0]
    return pl.pallas_call(
        _projf_kernel,
        grid=(T // tm,),
        in_specs=[pl.BlockSpec((tm, D), lambda i: (i, 0)),
                  pl.BlockSpec((N, D), lambda i: (0, 0))],
        out_specs=[pl.BlockSpec((n_kv, tm, HD), lambda i: (0, i, 0)),
                   pl.BlockSpec((tm, N - n_kv * HD), lambda i: (i, 0))],
        out_shape=[jax.ShapeDtypeStruct((n_kv, T, HD), F32),
                   jax.ShapeDtypeStruct((T, N - n_kv * HD), F32)],
        compiler_params=pltpu.CompilerParams(vmem_limit_bytes=VMEM_LIMIT),
        name="projf",
    )(h2, w_f_t)


def _ret_kernel(lg_ref, q_ref, k_ref, v_ref, rg_ref, g_ref, o_ref):
    S = q_ref.shape[1]
    C = RET_CHUNK
    r = lax.broadcasted_iota(jnp.int32, (C, C), 0)
    c = lax.broadcasted_iota(jnp.int32, (C, C), 1)
    diff = (r - c).astype(F32)
    rowf = lax.broadcasted_iota(jnp.int32, (C, HD), 0).astype(F32)
    for hh in range(RET_HPS):
        cols = slice(hh * HD, (hh + 1) * HD)
        lg = lg_ref[pl.program_id(1) * RET_HPS + hh]
        dmask = jnp.where(diff >= 0, jnp.exp(lg * jnp.maximum(diff, 0.0)), 0.0)
        zeta = jnp.exp(lg * (C - 1 - rowf))
        xi = jnp.exp(lg * (rowf + 1.0))
        decay = jnp.exp(jnp.full((HD, HD), lg * C, F32))
        g = g_ref[:, cols]
        state = jnp.zeros((HD, HD), F32)
        for n in range(S // C):
            sl = slice(n * C, (n + 1) * C)
            q = q_ref[0, sl, cols]
            k = k_ref[0, sl, cols]
            v = v_ref[0, sl, cols]
            sc = _dot_nt(q, k) * dmask
            o = _dot(sc.astype(BF16), v)
            if n > 0:
                o = o + _dot(q, state.astype(BF16)) * xi
            if n + 1 < S // C:
                kz = (k.astype(F32) * zeta).astype(BF16)
                state = state * decay + _dot_tn(kz, v)
            mu = jnp.mean(o, -1, keepdims=True)
            d = o - mu
            var = jnp.mean(d * d, -1, keepdims=True)
            on = d * lax.rsqrt(var + EPS)
            o_ref[0, sl, cols] = (on * g * rg_ref[0, sl, cols].astype(F32)).astype(o_ref.dtype)


def _retention(P3, log_gamma, g_ret, *, q_blk, k_blk, v_blk, rg_blk):
    B, S, _ = P3.shape
    H = RET_HEADS
    W = RET_HPS * HD

    def col(off):
        return lambda b, h, lg: (b, 0, off + h)

    return pl.pallas_call(
        _ret_kernel,
        grid_spec=pltpu.PrefetchScalarGridSpec(
            num_scalar_prefetch=1,
            grid=(B, H // RET_HPS),
            in_specs=[pl.BlockSpec((1, S, W), col(q_blk)),
                      pl.BlockSpec((1, S, W), col(k_blk)),
                      pl.BlockSpec((1, S, W), col(v_blk)),
                      pl.BlockSpec((1, S, W), col(rg_blk)),
                      pl.BlockSpec((1, W), lambda b, h, lg: (0, h))],
            out_specs=pl.BlockSpec((1, S, W), lambda b, h, lg: (b, 0, h)),
        ),
        out_shape=jax.ShapeDtypeStruct((B, S, H * HD), BF16),
        compiler_params=pltpu.CompilerParams(vmem_limit_bytes=VMEM_LIMIT),
        name="ret",
    )(log_gamma, P3, P3, P3, P3, g_ret.reshape(1, H * HD))


def _cmp_kernel(x_ref, pe_ref, w1_ref, w2_ref, cos_ref, sin_ref, o_ref):
    half = (CMP_BLOCK // 2) * HD
    M = x_ref.shape[2] // CMP_STRIDE
    x = jnp.concatenate([x_ref[0, 0, pl.ds(p, M, stride=CMP_STRIDE), :] for p in range(CMP_STRIDE)], 1)
    xa = (x + pe_ref[0, 0:1, :]).astype(BF16)
    xb = (x + pe_ref[0, 1:2, :]).astype(BF16)
    a = _dot(xa, w1_ref[0, :half, :])
    b = _dot(xb, w1_ref[0, half:, :])
    nrow = a.shape[0]
    pre = a + pltpu.roll(b, nrow - 1, 0)
    hid = pre * _sigmoid(pre)
    out = _dot(hid.astype(BF16), w2_ref[0])
    roped = _rope(out, cos_ref[0], sin_ref[0])
    is_key = pl.program_id(1) < NSA_GROUPS
    o_ref[0, 0] = jnp.where(is_key, roped, out).astype(o_ref.dtype)


def _compress(xh, pe2, w1s, w2s, cos_c, sin_c):
    NC, B, S, _ = xh.shape
    G = NSA_GROUPS
    M = S // CMP_STRIDE
    W = CMP_STRIDE * HD
    return pl.pallas_call(
        _cmp_kernel,
        grid=(B, NC),
        in_specs=[pl.BlockSpec((1, 1, S, HD), lambda b, c: (c, b, 0, 0)),
                  pl.BlockSpec((1, 2, W), lambda b, c: (c // G, 0, 0)),
                  pl.BlockSpec((1, 2 * W, CMP_HIDDEN), lambda b, c: (c // G, 0, 0)),
                  pl.BlockSpec((1, CMP_HIDDEN, HD), lambda b, c: (c // G, 0, 0)),
                  pl.BlockSpec((1, M, HD), lambda b, c: (b, 0, 0)),
                  pl.BlockSpec((1, M, HD), lambda b, c: (b, 0, 0))],
        out_specs=pl.BlockSpec((1, 1, M, HD), lambda b, c: (b, c, 0, 0)),
        out_shape=jax.ShapeDtypeStruct((B, NC, M, HD), BF16),
        name="cmp",
    )(xh, pe2, w1s, w2s, cos_c, sin_c)


def _nsa_kernel(q_ref, ks_ref, vs_ref, kw_ref, vw_ref, kc_ref, vc_ref, gt_ref, ng_ref, o_ref,
                qa_ref, ow_ref, acc_ref, oc_ref, kaug_ref, vsaug_ref, vwaug_ref, wb_ref):
    TQ = q_ref.shape[1]
    S = ks_ref.shape[1]
    R = NSA_REP
    RT = R * TQ
    WSPAN = WINDOW + TQ
    NSEL = S // SEL_BLOCK
    qi = pl.program_id(2)
    t0 = qi * TQ

    @pl.when(qi == 0)
    def _():
        blk = lax.broadcasted_iota(jnp.int32, (S, HD), 0) >> 6
        onehot = jnp.where(blk == lax.broadcasted_iota(jnp.int32, (S, HD), 1), 1.0, 0.0)
        kaug_ref[:, 0:HD] = ks_ref[0]
        kaug_ref[:, HD:2 * HD] = onehot.astype(BF16)
        ones = jnp.ones((S, HD), BF16)
        vsaug_ref[:, 0:HD] = vs_ref[0]
        vsaug_ref[:, HD:2 * HD] = ones
        vwaug_ref[:, 0:HD] = vw_ref[0]
        vwaug_ref[:, HD:2 * HD] = ones
        r = lax.broadcasted_iota(jnp.int32, (TQ, WSPAN), 0)
        c = lax.broadcasted_iota(jnp.int32, (TQ, WSPAN), 1)
        wb_ref[0] = jnp.where(c <= r, 0.0, NEG)
        wb_ref[1] = jnp.where(c <= r + TQ, 0.0, NEG)
        wb_ref[2] = jnp.where((c > r) & (c <= r + WINDOW), 0.0, NEG)

    for h in range(R):
        qa_ref[h * TQ:(h + 1) * TQ, 0:HD] = q_ref[0, :, h * HD:(h + 1) * HD]
    q4 = qa_ref[:, 0:HD]

    offw = pl.multiple_of(jnp.maximum(t0 - WINDOW, 0), TQ)
    kwin = kw_ref[0, pl.ds(offw, WSPAN), :]
    vwin = vwaug_ref[pl.ds(offw, WSPAN), :]
    wbias = wb_ref[jnp.minimum(qi, 2)]

    def win_scores(h):
        return _dot_nt(q_ref[0, :, h * HD:(h + 1) * HD], kwin) + wbias

    def win_finish(h, sw):
        pw = jnp.exp2(sw - jnp.max(sw, -1, keepdims=True))
        ow = _dot(pw.astype(BF16), vwin)
        ow_ref[h * TQ:(h + 1) * TQ, :] = ow[:, 0:HD] / ow[:, HD:2 * HD]

    sw0 = win_scores(0)

    s = _dot_nt(q4, kc_ref[0, 0])
    trow = t0 + (lax.broadcasted_iota(jnp.int32, (RT, HD), 0) & (TQ - 1))
    ncol = lax.broadcasted_iota(jnp.int32, (RT, HD), 1)
    s = jnp.where(ncol * CMP_STRIDE + (CMP_BLOCK - 1) <= trow, s, NEG)
    mx = jnp.max(s, -1, keepdims=True)
    p = jnp.exp2(s - mx)
    tcol = t0 + (lax.broadcasted_iota(jnp.int32, (RT, 1), 0) & (TQ - 1))
    any_valid = jnp.where(tcol >= CMP_BLOCK - 1, 1.0, 0.0)
    p = p * (any_valid / jnp.sum(p, -1, keepdims=True))
    sw1 = win_scores(1)
    win_finish(0, sw0)
    oc_ref[...] = _dot(p.astype(BF16), vc_ref[0, 0])

    psum = p[0:TQ]
    for h in range(1, R):
        psum = psum + p[h * TQ:(h + 1) * TQ]
    p_hi = psum.astype(BF16)
    rem = psum - p_hi.astype(F32)
    p_mid = rem.astype(BF16)
    p_lo = (rem - p_mid.astype(F32)).astype(BF16)
    jj = lax.broadcasted_iota(jnp.int32, (NSEL, HD), 0)
    nn = lax.broadcasted_iota(jnp.int32, (NSEL, HD), 1)
    ov = jnp.maximum(jnp.minimum(nn * CMP_STRIDE + CMP_BLOCK, (jj + 1) * SEL_BLOCK)
                     - jnp.maximum(nn * CMP_STRIDE, jj * SEL_BLOCK), 0).astype(F32) * (1.0 / CMP_BLOCK)
    ov = ov.astype(BF16)
    imp = (_dot_nt(ov, p_lo) + _dot_nt(ov, p_mid)) + _dot_nt(ov, p_hi)
    sw2 = win_scores(2)
    win_finish(1, sw1)

    j = lax.broadcasted_iota(jnp.int32, (NSEL, TQ), 0)
    tb = (t0 + lax.broadcasted_iota(jnp.int32, (NSEL, TQ), 1)) >> 6
    forced = (j == 0) | (j == tb) | (j == tb - 1)
    causal = j <= tb
    sc = jnp.where(causal, imp + jnp.where(forced, FORCE_BONUS, 0.0), NEG)
    rank = jnp.zeros((NSEL, TQ), F32)
    for i in range(NSEL):
        si = sc[i:i + 1, :]
        rank = rank + jnp.where(j > i, jnp.where(si >= sc, 1.0, 0.0), jnp.where(si > sc, 1.0, 0.0))
    sel = jnp.where((rank < SEL_TOPN) & causal, 1.0, 0.0)
    sw3 = win_scores(3)
    win_finish(2, sw2)

    selpad = jnp.concatenate([sel, jnp.zeros((HD - NSEL, TQ), F32)], 0).astype(BF16)
    eye = jnp.where(lax.broadcasted_iota(jnp.int32, (TQ, TQ), 0)
                    == lax.broadcasted_iota(jnp.int32, (TQ, TQ), 1), 1.0, 0.0).astype(BF16)
    sel_t = _dot_nt(eye, selpad)
    lane = lax.broadcasted_iota(jnp.int32, (TQ, HD), 1)
    bias = jnp.where((lane < NSEL) & (sel_t < 0.5), NEG, 0.0).astype(BF16)
    for h in range(R):
        qa_ref[h * TQ:(h + 1) * TQ, HD:2 * HD] = bias
    win_finish(3, sw3)

    for v in range(S // TQ):
        @pl.when(qi == v)
        def _(v=v):
            past = v * TQ
            for h in range(R):
                rows = slice(h * TQ, (h + 1) * TQ)
                qh = qa_ref[rows, :]
                sd = _dot_nt(qh, kaug_ref[past:past + TQ, :]) + wb_ref[0, :, 0:TQ]
                mx = jnp.max(sd, -1, keepdims=True)
                if past:
                    sp = _dot_nt(qh, kaug_ref[0:past, :])
                    mx = jnp.maximum(mx, jnp.max(sp, -1, keepdims=True))
                pv = _dot(jnp.exp2(sd - mx).astype(BF16), vsaug_ref[past:past + TQ, :])
                if past:
                    pv = pv + _dot(jnp.exp2(sp - mx).astype(BF16), vsaug_ref[0:past, :])
                acc_ref[rows, :] = pv

    gt = _sigmoid(gt_ref[0])
    for h in range(R):
        rows = slice(h * TQ, (h + 1) * TQ)
        o = (gt[:, 3 * h:3 * h + 1] * oc_ref[rows, :]
             + gt[:, 3 * h + 1:3 * h + 2] * (acc_ref[rows, 0:HD] / acc_ref[rows, HD:2 * HD])
             + gt[:, 3 * h + 2:3 * h + 3] * ow_ref[rows, :])
        o_ref[0, :, h * HD:(h + 1) * HD] = (o * ng_ref[0, :, h * HD:(h + 1) * HD].astype(F32)).astype(o_ref.dtype)


def _nsa(P3, F3, kcv, *, q_blk4, ks_blk, vs_blk, kw_blk, vw_blk, ng_blk4, gt_blk):
    B, S, _ = P3.shape
    G, R, TQ = NSA_GROUPS, NSA_REP, NSA_TQ
    assert WINDOW == 2 * TQ and SEL_BLOCK == 64 and S % (2 * TQ) == 0 and S >= WINDOW + TQ
    M = kcv.shape[2]
    kv_spec = lambda blk: pl.BlockSpec((1, S, HD), lambda b, g, i: (b, 0, blk + g))
    return pl.pallas_call(
        _nsa_kernel,
        grid=(B, G, S // TQ),
        in_specs=[pl.BlockSpec((1, TQ, R * HD), lambda b, g, i: (b, i, q_blk4 + g)),
                  kv_spec(ks_blk), kv_spec(vs_blk), kv_spec(kw_blk), kv_spec(vw_blk),
                  pl.BlockSpec((1, 1, M, HD), lambda b, g, i: (b, g, 0, 0)),
                  pl.BlockSpec((1, 1, M, HD), lambda b, g, i: (b, G + g, 0, 0)),
                  pl.BlockSpec((1, TQ, LANES), lambda b, g, i: (b, i, gt_blk + g)),
                  pl.BlockSpec((1, TQ, R * HD), lambda b, g, i: (b, i, ng_blk4 + g))],
        out_specs=pl.BlockSpec((1, TQ, R * HD), lambda b, g, i: (b, i, g)),
        out_shape=jax.ShapeDtypeStruct((B, S, NSA_HEADS * HD), BF16),
        scratch_shapes=[pltpu.VMEM((R * TQ, 2 * HD), BF16),
                        pltpu.VMEM((R * TQ, HD), F32),
                        pltpu.VMEM((R * TQ, 2 * HD), F32),
                        pltpu.VMEM((R * TQ, HD), F32),
                        pltpu.VMEM((S, 2 * HD), BF16),
                        pltpu.VMEM((S, 2 * HD), BF16),
                        pltpu.VMEM((S, 2 * HD), BF16),
                        pltpu.VMEM((3, TQ, WINDOW + TQ), F32)],
        compiler_params=pltpu.CompilerParams(
            dimension_semantics=("arbitrary", "arbitrary", "arbitrary"), vmem_limit_bytes=VMEM_LIMIT),
        name="nsa",
    )(P3, P3, P3, P3, P3, kcv, kcv, F3, P3)


def _up_kernel(yr_ref, yn_ref, wr_ref, wn_ref, sa_ref, sb_ref, o_ref):
    a = _dot(yr_ref[...], wr_ref[...])
    b = _dot(yn_ref[...], wn_ref[...])
    o_ref[...] = (sa_ref[...].astype(F32) * a + sb_ref[...].astype(F32) * b).astype(o_ref.dtype)


def _up(y_ret, y_nsa, w_ur, w_un, P2, *, tm):
    T, W = y_ret.shape
    D = w_ur.shape[1]
    return pl.pallas_call(
        _up_kernel,
        grid=(T // tm,),
        in_specs=[pl.BlockSpec((tm, W), lambda i: (i, 0)),
                  pl.BlockSpec((tm, W), lambda i: (i, 0)),
                  pl.BlockSpec((W, D), lambda i: (0, 0)),
                  pl.BlockSpec((W, D), lambda i: (0, 0)),
                  pl.BlockSpec((tm, D), lambda i: (i, 0)),
                  pl.BlockSpec((tm, D), lambda i: (i, 1))],
        out_specs=pl.BlockSpec((tm, D), lambda i: (i, 0)),
        out_shape=jax.ShapeDtypeStruct((T, D), BF16),
        compiler_params=pltpu.CompilerParams(vmem_limit_bytes=VMEM_LIMIT),
        name="up",
    )(y_ret, y_nsa, w_ur, w_un, P2, P2)


def _out_kernel(m_ref, w_ref, x_ref, gate_ref, g_ref, o_ref, *, final_norm):
    y = x_ref[0] + gate_ref[0] * _dot(m_ref[0], w_ref[...])
    if final_norm:
        ms = jnp.mean(y * y, -1, keepdims=True)
        y = y * lax.rsqrt(ms + EPS) * g_ref[...]
    o_ref[0] = y


def _out(merged3, w_out, x, gate, g_final, *, ts, final_norm):
    B, S, D = x.shape
    return pl.pallas_call(
        functools.partial(_out_kernel, final_norm=final_norm),
        grid=(B, S // ts),
        in_specs=[pl.BlockSpec((1, ts, D), lambda b, i: (b, i, 0)),
                  pl.BlockSpec((D, D), lambda b, i: (0, 0)),
                  pl.BlockSpec((1, ts, D), lambda b, i: (b, i, 0)),
                  pl.BlockSpec((1, 1, D), lambda b, i: (b, 0, 0)),
                  pl.BlockSpec((1, D), lambda b, i: (0, 0))],
        out_specs=pl.BlockSpec((1, ts, D), lambda b, i: (b, i, 0)),
        out_shape=jax.ShapeDtypeStruct((B, S, D), F32),
        compiler_params=pltpu.CompilerParams(vmem_limit_bytes=VMEM_LIMIT),
        name="out",
    )(merged3, w_out, x, gate.reshape(B, 1, D), g_final.reshape(1, D))


def kernel(x, c, positions, w_ada, b_ada, g_norm, w_in, g_ret, w_ck1, w_ck2, pe_ck,
           w_cv1, w_cv2, pe_cv, w_up_ret, w_up_nsa, w_out, g_final):
    B, S, D = x.shape
    T = B * S
    depth = w_ada.shape[0]
    G = NSA_GROUPS
    RW = RET_HEADS * HD
    NW = NSA_HEADS * HD
    KVW = G * HD
    per_tile = PROJ_TN // LANES

    cols = (("r_q", RW, "rope", 1.0), ("r_k", RW, "rope", HD ** -0.5), ("r_v", RW, "plain", 1.0),
            ("r_g", RW, "silu", 1.0), ("n_q", NW, "rope", HD ** -0.5 * LOG2E),
            ("n_kc", KVW, "f32", 1.0), ("n_vc", KVW, "f32", 1.0),
            ("n_ks", KVW, "rope", 1.0), ("n_vs", KVW, "plain", 1.0),
            ("n_kw", KVW, "rope", 1.0), ("n_vw", KVW, "plain", 1.0), ("n_g", NW, "silu", 1.0))
    start = {}
    kinds = []
    off = 0
    for nm, wd, kind, scale in cols:
        start[nm] = off
        kinds += [(nm, kind, scale)] * (wd // LANES)
        off += wd
    start["n_bg"] = off
    start["m_a"] = off + NSA_HEADS * 3
    assert off % PROJ_TN == 0 and (2 * D) % PROJ_TN == 0

    n_sig = 2 * D // PROJ_TN
    tiles = [("sig", 1.0)] * n_sig
    src_tiles = [0] * n_sig
    blk = {"m_a": 0, "m_b": D // LANES}
    for t in range(off // PROJ_TN):
        group = kinds[t * per_tile:(t + 1) * per_tile]
        tkinds = [k for _, k, _ in group]
        if all(k == "f32" for k in tkinds):
            continue
        if len(set(tkinds)) == 1:
            assert len({s for _, _, s in group}) == 1
            tiles.append((tkinds[0], group[0][2]))
        else:
            half = per_tile // 2
            assert tkinds == ["rope"] * half + ["plain"] * half and all(s == 1.0 for _, _, s in group)
            tiles.append(("rope_plain", 1.0))
        for i, (nm, _, _) in enumerate(group):
            blk.setdefault(nm, (len(tiles) - 1) * per_tile + i)
        src_tiles.append(t)

    inv = jnp.exp(jnp.arange(0, HD, 2, dtype=F32) * (-math.log(ROPE_THETA) / HD))
    inv2 = jnp.concatenate([inv, inv]).reshape(1, HD)
    cos, sin = _rope_tables(positions.reshape(T, 1), inv2)
    cend = jnp.arange(S // CMP_STRIDE) * CMP_STRIDE + (CMP_BLOCK - 1)
    cend = jnp.minimum(cend, S - 1)
    cos_c = cos.reshape(B, S, HD)[:, cend]
    sin_c = sin.reshape(B, S, HD)[:, cend]
    log_gamma = jnp.log1p(-jnp.exp2(-5.0 - jnp.arange(RET_HEADS, dtype=F32)))
    w_in_t = jnp.swapaxes(w_in, 1, 2)

    for l in range(depth):
        wt = w_in_t[l]
        w_sig = wt[start["m_a"]:start["m_a"] + 2 * D]
        bg = wt[start["n_bg"]:start["n_bg"] + NSA_HEADS * 3].reshape(G, NSA_REP * 3, D)
        bg = jnp.pad(bg, ((0, 0), (0, LANES - NSA_REP * 3), (0, 0))).reshape(G * LANES, D)
        w_f = jnp.concatenate([wt[start["n_kc"]:start["n_kc"] + 2 * KVW], bg], 0)

        mod = _mod(c, w_ada[l], b_ada[l])
        shift, scl, gate = mod[:, :D], mod[:, D:2 * D], mod[:, 2 * D:]
        h2 = _h(x, g_norm[l], scl, shift).reshape(T, D)

        P2 = _proj(h2, w_sig, w_in_t, l, cos, sin, tiles=tiles, src_tiles=src_tiles, tm=min(PROJ_TM, T))
        kv, gts = _projf(h2, w_f, n_kv=2 * G, tm=min(1024, T))
        P3 = P2.reshape(B, S, -1)

        hps = RET_HPS
        y_ret = _retention(P3, log_gamma, g_ret[l], q_blk=blk["r_q"] // hps, k_blk=blk["r_k"] // hps,
                           v_blk=blk["r_v"] // hps, rg_blk=blk["r_g"] // hps)

        xh = kv.reshape(2 * G, B, S, HD)
        half = CMP_BLOCK // 2
        pe2 = jnp.stack([pe_ck[l].reshape(2, half * HD), pe_cv[l].reshape(2, half * HD)])
        w1s = jnp.stack([w_ck1[l], w_cv1[l]]).astype(BF16)
        w2s = jnp.stack([w_ck2[l], w_cv2[l]]).astype(BF16)
        kcv = _compress(xh, pe2, w1s, w2s, cos_c, sin_c)

        y_nsa = _nsa(P3, gts.reshape(B, S, G * LANES), kcv, q_blk4=blk["n_q"] // NSA_REP,
                     ks_blk=blk["n_ks"], vs_blk=blk["n_vs"], kw_blk=blk["n_kw"], vw_blk=blk["n_vw"],
                     ng_blk4=blk["n_g"] // NSA_REP, gt_blk=0)

        merged = _up(y_ret.reshape(T, RW), y_nsa.reshape(T, NW), w_up_ret[l].astype(BF16),
                     w_up_nsa[l].astype(BF16), P2, tm=min(512, T))
        x = _out(merged.reshape(B, S, D), w_out[l].astype(BF16), x, gate, g_final,
                 ts=min(256, S), final_norm=(l + 1 == depth))
    return x
```

```python
import functools
import math

import jax
import jax.numpy as jnp
from jax import lax
from jax.experimental import pallas as pl
from jax.experimental.pallas import tpu as pltpu

F32 = jnp.float32
BF16 = jnp.bfloat16

HD = 128
RET_HEADS = 8
NSA_HEADS = 8
NSA_GROUPS = 2
NSA_REP = NSA_HEADS // NSA_GROUPS
CMP_BLOCK = 32
CMP_STRIDE = 16
CMP_HIDDEN = 256
SEL_BLOCK = 64
SEL_TOPN = 16
WINDOW = 512
FORCE_BONUS = 1.0e4
ROPE_THETA = 10000.0
EPS = 1e-6
NEG = -1.0e30

LANES = 128
PROJ_TN = 512
PROJ_TM = 2048
PROJ_RC = 512
RET_CHUNK = 256
RET_HPS = 4
NSA_TQ = 256
LOG2E = math.log2(math.e)
VMEM_LIMIT = 56 * 1024 * 1024


def _sigmoid(v):
    return 1.0 / (1.0 + jnp.exp(-v))


def _dot(a, b):
    return jnp.dot(a, b, preferred_element_type=F32)


def _dot_nt(a, b):
    return lax.dot_general(a, b, (((1,), (1,)), ((), ())), preferred_element_type=F32)


def _dot_tn(a, b):
    return lax.dot_general(a, b, (((0,), (0,)), ((), ())), preferred_element_type=F32)


def _rope(v, cos, sin_signed):
    return v * cos + pltpu.roll(v, HD // 2, 1) * sin_signed


def _mod_kernel(c_ref, w_ref, b_ref, o_ref):
    c = c_ref[...]
    sc = c * _sigmoid(c)
    o_ref[...] = jnp.dot(sc, w_ref[...], preferred_element_type=F32,
                         precision=lax.Precision.HIGHEST) + b_ref[...]


def _mod(c, w_ada, b_ada):
    B, D = c.shape
    N = w_ada.shape[1]
    tn = math.gcd(N, 512)
    return pl.pallas_call(
        _mod_kernel,
        grid=(N // tn,),
        in_specs=[pl.BlockSpec((B, D), lambda j: (0, 0)),
                  pl.BlockSpec((D, tn), lambda j: (0, j)),
                  pl.BlockSpec((1, tn), lambda j: (0, j))],
        out_specs=pl.BlockSpec((B, tn), lambda j: (0, j)),
        out_shape=jax.ShapeDtypeStruct((B, N), F32),
        name="mod",
    )(c, w_ada, b_ada.reshape(1, N))


def _h_kernel(x_ref, g_ref, scl_ref, sh_ref, pos_ref, inv_ref, h_ref, cos_ref, sin_ref):
    x = x_ref[0]
    ms = jnp.mean(x * x, -1, keepdims=True)
    y = x * lax.rsqrt(ms + EPS) * g_ref[...]
    h_ref[0] = (y * (1.0 + scl_ref[0]) + sh_ref[0]).astype(BF16)
    ang = pos_ref[0].astype(F32) * inv_ref[...]
    lane = lax.broadcasted_iota(jnp.int32, (1, HD), 1)
    cos_ref[0] = jnp.cos(ang)
    sin_ref[0] = jnp.where(lane < HD // 2, -1.0, 1.0) * jnp.sin(ang)


def _h(x, g_norm, scl, shift, positions, inv2):
    B, S, D = x.shape
    ts = 512
    return pl.pallas_call(
        _h_kernel,
        grid=(B, S // ts),
        in_specs=[pl.BlockSpec((1, ts, D), lambda b, i: (b, i, 0)),
                  pl.BlockSpec((1, D), lambda b, i: (0, 0)),
                  pl.BlockSpec((1, 1, D), lambda b, i: (b, 0, 0)),
                  pl.BlockSpec((1, 1, D), lambda b, i: (b, 0, 0)),
                  pl.BlockSpec((1, ts, 1), lambda b, i: (b, i, 0)),
                  pl.BlockSpec((1, HD), lambda b, i: (0, 0))],
        out_specs=[pl.BlockSpec((1, ts, D), lambda b, i: (b, i, 0)),
                   pl.BlockSpec((1, ts, HD), lambda b, i: (b, i, 0)),
                   pl.BlockSpec((1, ts, HD), lambda b, i: (b, i, 0))],
        out_shape=[jax.ShapeDtypeStruct((B, S, D), BF16),
                   jax.ShapeDtypeStruct((B, S, HD), F32),
                   jax.ShapeDtypeStruct((B, S, HD), F32)],
        name="h",
    )(x, g_norm.reshape(1, D), scl.reshape(B, 1, D), shift.reshape(B, 1, D),
      positions.reshape(B, S, 1), inv2)


def _tile_cond(j, idxs):
    cond = None
    run_lo = prev = None
    for t in list(idxs) + [None]:
        if run_lo is None:
            run_lo = prev = t
            continue
        if t is not None and t == prev + 1:
            prev = t
            continue
        c = (j >= run_lo) & (j <= prev)
        cond = c if cond is None else (cond | c)
        run_lo = prev = t
    return cond


def _proj_kernel(h_ref, ws_ref, w_ref, cos_ref, sin_ref, o_ref, *, tiles):
    j = pl.program_id(1)
    tm = h_ref.shape[0]
    rc = min(PROJ_RC, tm)
    half = PROJ_TN // 2

    def row_chunks(w, epilogue):
        for c in range(tm // rc):
            rows = slice(c * rc, (c + 1) * rc)
            epilogue(rows, _dot_nt(h_ref[rows, :], w))

    def of_kind(kind):
        return [t for t, (k, _) in enumerate(tiles) if k == kind]

    def rope_cols(rows, acc, cos, sin, lo, hi):
        for hh in range(lo // HD, hi // HD):
            sl = slice(hh * HD, (hh + 1) * HD)
            o_ref[rows, sl] = _rope(acc[:, sl], cos, sin).astype(o_ref.dtype)

    @pl.when(_tile_cond(j, of_kind("sig")))
    def _():
        def epi(rows, acc):
            o_ref[rows, :] = _sigmoid(acc).astype(o_ref.dtype)
        row_chunks(ws_ref[...].astype(BF16), epi)

    @pl.when(_tile_cond(j, of_kind("rope")))
    def _():
        f = jnp.float32(1.0)
        for scale in sorted({s for k, s in tiles if k == "rope" and s != 1.0}):
            f = jnp.where(_tile_cond(j, [t for t, (k, s) in enumerate(tiles) if k == "rope" and s == scale]),
                          jnp.float32(scale), f)

        def epi(rows, acc):
            rope_cols(rows, acc, cos_ref[rows, :] * f, sin_ref[rows, :] * f, 0, PROJ_TN)
        row_chunks(w_ref[...].astype(BF16), epi)

    @pl.when(_tile_cond(j, of_kind("rope_plain")))
    def _():
        def epi(rows, acc):
            rope_cols(rows, acc, cos_ref[rows, :], sin_ref[rows, :], 0, half)
            o_ref[rows, half:] = acc[:, half:].astype(o_ref.dtype)
        row_chunks(w_ref[...].astype(BF16), epi)

    @pl.when(_tile_cond(j, of_kind("silu")))
    def _():
        def epi(rows, acc):
            o_ref[rows, :] = (acc * _sigmoid(acc)).astype(o_ref.dtype)
        row_chunks(w_ref[...].astype(BF16), epi)

    @pl.when(_tile_cond(j, of_kind("plain")))
    def _():
        def epi(rows, acc):
            o_ref[rows, :] = acc.astype(o_ref.dtype)
        row_chunks(w_ref[...].astype(BF16), epi)


def _proj(h2, w_sig_t, w_in_t, layer, cos, sin, *, tiles, src_tiles, tm):
    T, D = h2.shape
    n_sig = w_sig_t.shape[0] // PROJ_TN
    n_tiles = len(tiles)
    assert all(src_tiles[t + 1] - src_tiles[t] in (1, 2) for t in range(n_sig, n_tiles - 1))
    skips = [t for t in range(n_sig, n_tiles - 1) if src_tiles[t + 1] - src_tiles[t] == 2]

    def src(j):
        s = jnp.maximum(j - n_sig, 0) + src_tiles[n_sig]
        for t in skips:
            s = s + jnp.where(j > t, 1, 0)
        return s

    kern = functools.partial(_proj_kernel, tiles=tuple(tiles))
    return pl.pallas_call(
        kern,
        grid=(T // tm, n_tiles),
        in_specs=[pl.BlockSpec((tm, D), lambda i, j: (i, 0)),
                  pl.BlockSpec((PROJ_TN, D), lambda i, j: (jnp.minimum(j, n_sig - 1), 0)),
                  pl.BlockSpec((None, PROJ_TN, D), lambda i, j: (layer, src(j), 0)),
                  pl.BlockSpec((tm, HD), lambda i, j: (i, 0)),
                  pl.BlockSpec((tm, HD), lambda i, j: (i, 0))],
        out_specs=pl.BlockSpec((tm, PROJ_TN), lambda i, j: (i, j)),
        out_shape=jax.ShapeDtypeStruct((T, n_tiles * PROJ_TN), BF16),
        compiler_params=pltpu.CompilerParams(
            dimension_semantics=("arbitrary", "arbitrary"), vmem_limit_bytes=VMEM_LIMIT),
        name="proj",
    )(h2, w_sig_t, w_in_t, cos, sin)


def _projf_kernel(h_ref, w_ref, kv_ref, gt_ref):
    n_kv = kv_ref.shape[0]
    acc = _dot_nt(h_ref[...], w_ref[...].astype(BF16))
    for c in range(n_kv):
        kv_ref[c] = acc[:, c * HD:(c + 1) * HD]
    gt_ref[...] = acc[:, n_kv * HD:]


def _projf(h2, w_f_t, *, n_kv, tm):
    T, D = h2.shape
    N = w_f_t.shape[0]
    return pl.pallas_call(
        _projf_kernel,
        grid=(T // tm,),
        in_specs=[pl.BlockSpec((tm, D), lambda i: (i, 0)),
                  pl.BlockSpec((N, D), lambda i: (0, 0))],
        out_specs=[pl.BlockSpec((n_kv, tm, HD), lambda i: (0, i, 0)),
                   pl.BlockSpec((tm, N - n_kv * HD), lambda i: (i, 0))],
        out_shape=[jax.ShapeDtypeStruct((n_kv, T, HD), F32),
                   jax.ShapeDtypeStruct((T, N - n_kv * HD), F32)],
        compiler_params=pltpu.CompilerParams(vmem_limit_bytes=VMEM_LIMIT),
        name="projf",
    )(h2, w_f_t)


def _ret_kernel(lg_ref, q_ref, k_ref, v_ref, rg_ref, g_ref, o_ref):
    S = q_ref.shape[1]
    C = RET_CHUNK
    r = lax.broadcasted_iota(jnp.int32, (C, C), 0)
    c = lax.broadcasted_iota(jnp.int32, (C, C), 1)
    diff = (r - c).astype(F32)
    rowf = lax.broadcasted_iota(jnp.int32, (C, HD), 0).astype(F32)
    for hh in range(RET_HPS):
        cols = slice(hh * HD, (hh + 1) * HD)
        lg = lg_ref[pl.program_id(1) * RET_HPS + hh]
        dmask = jnp.where(diff >= 0, jnp.exp(lg * jnp.maximum(diff, 0.0)), 0.0)
        zeta = jnp.exp(lg * (C - 1 - rowf))
        xi = jnp.exp(lg * (rowf + 1.0))
        decay = jnp.exp(jnp.full((HD, HD), lg * C, F32))
        g = g_ref[:, cols]
        state = jnp.zeros((HD, HD), F32)
        for n in range(S // C):
            sl = slice(n * C, (n + 1) * C)
            q = q_ref[0, sl, cols]
            k = k_ref[0, sl, cols]
            v = v_ref[0, sl, cols]
            sc = _dot_nt(q, k) * dmask
            o = _dot(sc.astype(BF16), v)
            if n > 0:
                o = o + _dot(q, state.astype(BF16)) * xi
            if n + 1 < S // C:
                kz = (k.astype(F32) * zeta).astype(BF16)
                state = state * decay + _dot_tn(kz, v)
            mu = jnp.mean(o, -1, keepdims=True)
            d = o - mu
            var = jnp.mean(d * d, -1, keepdims=True)
            on = d * lax.rsqrt(var + EPS)
            o_ref[0, sl, cols] = (on * g * rg_ref[0, sl, cols].astype(F32)).astype(o_ref.dtype)


def _retention(P3, log_gamma, g_ret, *, q_blk, k_blk, v_blk, rg_blk):
    B, S, _ = P3.shape
    H = RET_HEADS
    W = RET_HPS * HD

    def col(off):
        return lambda b, h, lg: (b, 0, off + h)

    return pl.pallas_call(
        _ret_kernel,
        grid_spec=pltpu.PrefetchScalarGridSpec(
            num_scalar_prefetch=1,
            grid=(B, H // RET_HPS),
            in_specs=[pl.BlockSpec((1, S, W), col(q_blk)),
                      pl.BlockSpec((1, S, W), col(k_blk)),
                      pl.BlockSpec((1, S, W), col(v_blk)),
                      pl.BlockSpec((1, S, W), col(rg_blk)),
                      pl.BlockSpec((1, W), lambda b, h, lg: (0, h))],
            out_specs=pl.BlockSpec((1, S, W), lambda b, h, lg: (b, 0, h)),
        ),
        out_shape=jax.ShapeDtypeStruct((B, S, H * HD), BF16),
        compiler_params=pltpu.CompilerParams(vmem_limit_bytes=VMEM_LIMIT),
        name="ret",
    )(log_gamma, P3, P3, P3, P3, g_ret.reshape(1, H * HD))


def _cmp_kernel(x_ref, pe_ref, w1_ref, w2_ref, cos_ref, sin_ref, o_ref):
    half = (CMP_BLOCK // 2) * HD
    M = x_ref.shape[2] // CMP_STRIDE
    x = jnp.concatenate([x_ref[0, 0, pl.ds(p, M, stride=CMP_STRIDE), :] for p in range(CMP_STRIDE)], 1)
    xa = (x + pe_ref[0, 0:1, :]).astype(BF16)
    xb = (x + pe_ref[0, 1:2, :]).astype(BF16)
    a = _dot(xa, w1_ref[0, :half, :])
    b = _dot(xb, w1_ref[0, half:, :])
    nrow = a.shape[0]
    pre = a + pltpu.roll(b, nrow - 1, 0)
    hid = pre * _sigmoid(pre)
    out = _dot(hid.astype(BF16), w2_ref[0])
    roped = _rope(out, cos_ref[0], sin_ref[0])
    is_key = pl.program_id(1) < NSA_GROUPS
    o_ref[0, 0] = jnp.where(is_key, roped, out).astype(o_ref.dtype)


def _compress(xh, pe2, w1s, w2s, cos_c, sin_c):
    NC, B, S, _ = xh.shape
    G = NSA_GROUPS
    M = S // CMP_STRIDE
    W = CMP_STRIDE * HD
    return pl.pallas_call(
        _cmp_kernel,
        grid=(B, NC),
        in_specs=[pl.BlockSpec((1, 1, S, HD), lambda b, c: (c, b, 0, 0)),
                  pl.BlockSpec((1, 2, W), lambda b, c: (c // G, 0, 0)),
                  pl.BlockSpec((1, 2 * W, CMP_HIDDEN), lambda b, c: (c // G, 0, 0)),
                  pl.BlockSpec((1, CMP_HIDDEN, HD), lambda b, c: (c // G, 0, 0)),
                  pl.BlockSpec((1, M, HD), lambda b, c: (b, 0, 0)),
                  pl.BlockSpec((1, M, HD), lambda b, c: (b, 0, 0))],
        out_specs=pl.BlockSpec((1, 1, M, HD), lambda b, c: (b, c, 0, 0)),
        out_shape=jax.ShapeDtypeStruct((B, NC, M, HD), BF16),
        name="cmp",
    )(xh, pe2, w1s, w2s, cos_c, sin_c)


def _nsa_kernel(q_ref, ks_ref, vs_ref, kw_ref, vw_ref, kc_ref, vc_ref, gt_ref, ng_ref, o_ref,
                qa_ref, ow_ref, acc_ref, oc_ref, gs_ref, kaug_ref, vsaug_ref, vwaug_ref, wb_ref):
    TQ = q_ref.shape[1]
    S = ks_ref.shape[1]
    R = NSA_REP
    RT = R * TQ
    WSPAN = WINDOW + TQ
    NSEL = S // SEL_BLOCK
    qi = pl.program_id(2)
    t0 = qi * TQ

    @pl.when(qi == 0)
    def _():
        blk = lax.broadcasted_iota(jnp.int32, (S, HD), 0) >> 6
        onehot = jnp.where(blk == lax.broadcasted_iota(jnp.int32, (S, HD), 1), 1.0, 0.0)
        kaug_ref[:, 0:HD] = ks_ref[0]
        kaug_ref[:, HD:2 * HD] = onehot.astype(BF16)
        ones = jnp.ones((S, HD), BF16)
        vsaug_ref[:, 0:HD] = vs_ref[0]
        vsaug_ref[:, HD:2 * HD] = ones
        vwaug_ref[:, 0:HD] = vw_ref[0]
        vwaug_ref[:, HD:2 * HD] = ones
        r = lax.broadcasted_iota(jnp.int32, (TQ, WSPAN), 0)
        c = lax.broadcasted_iota(jnp.int32, (TQ, WSPAN), 1)
        wb_ref[0] = jnp.where(c <= r, 0.0, NEG)
        wb_ref[1] = jnp.where(c <= r + TQ, 0.0, NEG)
        wb_ref[2] = jnp.where((c > r) & (c <= r + WINDOW), 0.0, NEG)

    for h in range(R):
        qa_ref[h * TQ:(h + 1) * TQ, 0:HD] = q_ref[0, :, h * HD:(h + 1) * HD]
    q4 = qa_ref[:, 0:HD]

    offw = pl.multiple_of(jnp.maximum(t0 - WINDOW, 0), TQ)
    kwin = kw_ref[0, pl.ds(offw, WSPAN), :]
    vwin = vwaug_ref[pl.ds(offw, WSPAN), :]
    wbias = wb_ref[jnp.minimum(qi, 2)]

    def win_scores(h):
        return _dot_nt(q_ref[0, :, h * HD:(h + 1) * HD], kwin) + wbias

    gt = _sigmoid(gt_ref[0])

    def gate(h, branch):
        c = 3 * h + branch
        return jnp.broadcast_to(gt[:, c:c + 1], (TQ, HD))

    def win_finish(h, sw):
        pw = jnp.exp2(sw - jnp.max(sw, -1, keepdims=True))
        ow = _dot(pw.astype(BF16), vwin)
        ow_ref[h * TQ:(h + 1) * TQ, :] = gate(h, 2) * (ow[:, 0:HD] / ow[:, HD:2 * HD])

    for h in range(R):
        gs_ref[h * TQ:(h + 1) * TQ, :] = gate(h, 1)
    sw0 = win_scores(0)

    s = _dot_nt(q4, kc_ref[0, 0])
    trow = t0 + (lax.broadcasted_iota(jnp.int32, (RT, HD), 0) & (TQ - 1))
    ncol = lax.broadcasted_iota(jnp.int32, (RT, HD), 1)
    s = jnp.where(ncol * CMP_STRIDE + (CMP_BLOCK - 1) <= trow, s, NEG)
    mx = jnp.max(s, -1, keepdims=True)
    p = jnp.exp2(s - mx)
    tcol = t0 + (lax.broadcasted_iota(jnp.int32, (RT, 1), 0) & (TQ - 1))
    any_valid = jnp.where(tcol >= CMP_BLOCK - 1, 1.0, 0.0)
    p = p * (any_valid / jnp.sum(p, -1, keepdims=True))
    sw1 = win_scores(1)
    win_finish(0, sw0)
    o_cmp = _dot(p.astype(BF16), vc_ref[0, 0])
    for h in range(R):
        oc_ref[h * TQ:(h + 1) * TQ, :] = gate(h, 0) * o_cmp[h * TQ:(h + 1) * TQ]

    psum = p[0:TQ]
    for h in range(1, R):
        psum = psum + p[h * TQ:(h + 1) * TQ]
    p_hi = psum.astype(BF16)
    rem = psum - p_hi.astype(F32)
    p_mid = rem.astype(BF16)
    p_lo = (rem - p_mid.astype(F32)).astype(BF16)
    jj = lax.broadcasted_iota(jnp.int32, (NSEL, HD), 0)
    nn = lax.broadcasted_iota(jnp.int32, (NSEL, HD), 1)
    ov = jnp.maximum(jnp.minimum(nn * CMP_STRIDE + CMP_BLOCK, (jj + 1) * SEL_BLOCK)
                     - jnp.maximum(nn * CMP_STRIDE, jj * SEL_BLOCK), 0).astype(F32) * (1.0 / CMP_BLOCK)
    ov = ov.astype(BF16)
    imp = (_dot_nt(ov, p_lo) + _dot_nt(ov, p_mid)) + _dot_nt(ov, p_hi)
    sw2 = win_scores(2)
    win_finish(1, sw1)

    j = lax.broadcasted_iota(jnp.int32, (NSEL, TQ), 0)
    tb = (t0 + lax.broadcasted_iota(jnp.int32, (NSEL, TQ), 1)) >> 6
    forced = (j == 0) | (j == tb) | (j == tb - 1)
    causal = j <= tb
    sc = jnp.where(causal, imp + jnp.where(forced, FORCE_BONUS, 0.0), NEG)
    rank = jnp.zeros((NSEL, TQ), F32)
    for i in range(NSEL):
        si = sc[i:i + 1, :]
        rank = rank + jnp.where(j > i, jnp.where(si >= sc, 1.0, 0.0), jnp.where(si > sc, 1.0, 0.0))
    sel = jnp.where((rank < SEL_TOPN) & causal, 1.0, 0.0)
    sw3 = win_scores(3)
    win_finish(2, sw2)

    selpad = jnp.concatenate([sel, jnp.zeros((HD - NSEL, TQ), F32)], 0).astype(BF16)
    eye = jnp.where(lax.broadcasted_iota(jnp.int32, (TQ, TQ), 0)
                    == lax.broadcasted_iota(jnp.int32, (TQ, TQ), 1), 1.0, 0.0).astype(BF16)
    sel_t = _dot_nt(eye, selpad)
    lane = lax.broadcasted_iota(jnp.int32, (TQ, HD), 1)
    bias = jnp.where((lane < NSEL) & (sel_t < 0.5), NEG, 0.0).astype(BF16)
    for h in range(R):
        qa_ref[h * TQ:(h + 1) * TQ, HD:2 * HD] = bias
    win_finish(3, sw3)

    for v in range(S // TQ):
        @pl.when(qi == v)
        def _(v=v):
            past = v * TQ
            for h in range(R):
                rows = slice(h * TQ, (h + 1) * TQ)
                qh = qa_ref[rows, :]
                sd = _dot_nt(qh, kaug_ref[past:past + TQ, :]) + wb_ref[0, :, 0:TQ]
                mx = jnp.max(sd, -1, keepdims=True)
                if past:
                    sp = _dot_nt(qh, kaug_ref[0:past, :])
                    mx = jnp.maximum(mx, jnp.max(sp, -1, keepdims=True))
                pv = _dot(jnp.exp2(sd - mx).astype(BF16), vsaug_ref[past:past + TQ, :])
                if past:
                    pv = pv + _dot(jnp.exp2(sp - mx).astype(BF16), vsaug_ref[0:past, :])
                acc_ref[rows, :] = pv

    for h in range(R):
        rows = slice(h * TQ, (h + 1) * TQ)
        o = (oc_ref[rows, :] + gs_ref[rows, :] * (acc_ref[rows, 0:HD] / acc_ref[rows, HD:2 * HD])
             + ow_ref[rows, :])
        o_ref[0, :, h * HD:(h + 1) * HD] = (o * ng_ref[0, :, h * HD:(h + 1) * HD].astype(F32)).astype(o_ref.dtype)


def _nsa(P3, F3, kcv, *, q_blk4, ks_blk, vs_blk, kw_blk, vw_blk, ng_blk4, gt_blk):
    B, S, _ = P3.shape
    G, R, TQ = NSA_GROUPS, NSA_REP, NSA_TQ
    assert WINDOW == 2 * TQ and SEL_BLOCK == 64 and S % (2 * TQ) == 0 and S >= WINDOW + TQ
    M = kcv.shape[2]
    kv_spec = lambda blk: pl.BlockSpec((1, S, HD), lambda b, g, i: (b, 0, blk + g))
    return pl.pallas_call(
        _nsa_kernel,
        grid=(B, G, S // TQ),
        in_specs=[pl.BlockSpec((1, TQ, R * HD), lambda b, g, i: (b, i, q_blk4 + g)),
                  kv_spec(ks_blk), kv_spec(vs_blk), kv_spec(kw_blk), kv_spec(vw_blk),
                  pl.BlockSpec((1, 1, M, HD), lambda b, g, i: (b, g, 0, 0)),
                  pl.BlockSpec((1, 1, M, HD), lambda b, g, i: (b, G + g, 0, 0)),
                  pl.BlockSpec((1, TQ, LANES), lambda b, g, i: (b, i, gt_blk + g)),
                  pl.BlockSpec((1, TQ, R * HD), lambda b, g, i: (b, i, ng_blk4 + g))],
        out_specs=pl.BlockSpec((1, TQ, R * HD), lambda b, g, i: (b, i, g)),
        out_shape=jax.ShapeDtypeStruct((B, S, NSA_HEADS * HD), BF16),
        scratch_shapes=[pltpu.VMEM((R * TQ, 2 * HD), BF16),
                        pltpu.VMEM((R * TQ, HD), F32),
                        pltpu.VMEM((R * TQ, 2 * HD), F32),
                        pltpu.VMEM((R * TQ, HD), F32),
                        pltpu.VMEM((R * TQ, HD), F32),
                        pltpu.VMEM((S, 2 * HD), BF16),
                        pltpu.VMEM((S, 2 * HD), BF16),
                        pltpu.VMEM((S, 2 * HD), BF16),
                        pltpu.VMEM((3, TQ, WINDOW + TQ), F32)],
        compiler_params=pltpu.CompilerParams(
            dimension_semantics=("arbitrary", "arbitrary", "arbitrary"), vmem_limit_bytes=VMEM_LIMIT),
        name="nsa",
    )(P3, P3, P3, P3, P3, kcv, kcv, F3, P3)


def _up_kernel(yr_ref, yn_ref, wr_ref, wn_ref, sa_ref, sb_ref, o_ref):
    a = _dot(yr_ref[...], wr_ref[...])
    b = _dot(yn_ref[...], wn_ref[...])
    o_ref[...] = (sa_ref[...].astype(F32) * a + sb_ref[...].astype(F32) * b).astype(o_ref.dtype)


def _up(y_ret, y_nsa, w_ur, w_un, P2, *, tm):
    T, W = y_ret.shape
    D = w_ur.shape[1]
    return pl.pallas_call(
        _up_kernel,
        grid=(T // tm,),
        in_specs=[pl.BlockSpec((tm, W), lambda i: (i, 0)),
                  pl.BlockSpec((tm, W), lambda i: (i, 0)),
                  pl.BlockSpec((W, D), lambda i: (0, 0)),
                  pl.BlockSpec((W, D), lambda i: (0, 0)),
                  pl.BlockSpec((tm, D), lambda i: (i, 0)),
                  pl.BlockSpec((tm, D), lambda i: (i, 1))],
        out_specs=pl.BlockSpec((tm, D), lambda i: (i, 0)),
        out_shape=jax.ShapeDtypeStruct((T, D), BF16),
        compiler_params=pltpu.CompilerParams(vmem_limit_bytes=VMEM_LIMIT),
        name="up",
    )(y_ret, y_nsa, w_ur, w_un, P2, P2)


def _out_kernel(m_ref, w_ref, x_ref, gate_ref, g_ref, o_ref, *, final_norm):
    y = x_ref[0] + gate_ref[0] * _dot(m_ref[0], w_ref[...])
    if final_norm:
        ms = jnp.mean(y * y, -1, keepdims=True)
        y = y * lax.rsqrt(ms + EPS) * g_ref[...]
    o_ref[0] = y


def _out(merged3, w_out, x, gate, g_final, *, ts, final_norm):
    B, S, D = x.shape
    return pl.pallas_call(
        functools.partial(_out_kernel, final_norm=final_norm),
        grid=(B, S // ts),
        in_specs=[pl.BlockSpec((1, ts, D), lambda b, i: (b, i, 0)),
                  pl.BlockSpec((D, D), lambda b, i: (0, 0)),
                  pl.BlockSpec((1, ts, D), lambda b, i: (b, i, 0)),
                  pl.BlockSpec((1, 1, D), lambda b, i: (b, 0, 0)),
                  pl.BlockSpec((1, D), lambda b, i: (0, 0))],
        out_specs=pl.BlockSpec((1, ts, D), lambda b, i: (b, i, 0)),
        out_shape=jax.ShapeDtypeStruct((B, S, D), F32),
        compiler_params=pltpu.CompilerParams(vmem_limit_bytes=VMEM_LIMIT),
        name="out",
    )(merged3, w_out, x, gate.reshape(B, 1, D), g_final.reshape(1, D))


def kernel(x, c, positions, w_ada, b_ada, g_norm, w_in, g_ret, w_ck1, w_ck2, pe_ck,
           w_cv1, w_cv2, pe_cv, w_up_ret, w_up_nsa, w_out, g_final):
    B, S, D = x.shape
    T = B * S
    depth = w_ada.shape[0]
    G = NSA_GROUPS
    RW = RET_HEADS * HD
    NW = NSA_HEADS * HD
    KVW = G * HD
    per_tile = PROJ_TN // LANES

    cols = (("r_q", RW, "rope", 1.0), ("r_k", RW, "rope", HD ** -0.5), ("r_v", RW, "plain", 1.0),
            ("r_g", RW, "silu", 1.0), ("n_q", NW, "rope", HD ** -0.5 * LOG2E),
            ("n_kc", KVW, "f32", 1.0), ("n_vc", KVW, "f32", 1.0),
            ("n_ks", KVW, "rope", 1.0), ("n_vs", KVW, "plain", 1.0),
            ("n_kw", KVW, "rope", 1.0), ("n_vw", KVW, "plain", 1.0), ("n_g", NW, "silu", 1.0))
    start = {}
    kinds = []
    off = 0
    for nm, wd, kind, scale in cols:
        start[nm] = off
        kinds += [(nm, kind, scale)] * (wd // LANES)
        off += wd
    start["n_bg"] = off
    start["m_a"] = off + NSA_HEADS * 3
    assert off % PROJ_TN == 0 and (2 * D) % PROJ_TN == 0

    n_sig = 2 * D // PROJ_TN
    tiles = [("sig", 1.0)] * n_sig
    src_tiles = [0] * n_sig
    blk = {"m_a": 0, "m_b": D // LANES}
    for t in range(off // PROJ_TN):
        group = kinds[t * per_tile:(t + 1) * per_tile]
        tkinds = [k for _, k, _ in group]
        if all(k == "f32" for k in tkinds):
            continue
        if len(set(tkinds)) == 1:
            assert len({s for _, _, s in group}) == 1
            tiles.append((tkinds[0], group[0][2]))
        else:
            half = per_tile // 2
            assert tkinds == ["rope"] * half + ["plain"] * half and all(s == 1.0 for _, _, s in group)
            tiles.append(("rope_plain", 1.0))
        for i, (nm, _, _) in enumerate(group):
            blk.setdefault(nm, (len(tiles) - 1) * per_tile + i)
        src_tiles.append(t)

    inv = jnp.exp(jnp.arange(0, HD, 2, dtype=F32) * (-math.log(ROPE_THETA) / HD))
    inv2 = jnp.concatenate([inv, inv]).reshape(1, HD)
    cend = jnp.arange(S // CMP_STRIDE) * CMP_STRIDE + (CMP_BLOCK - 1)
    cend = jnp.minimum(cend, S - 1)
    log_gamma = jnp.log1p(-jnp.exp2(-5.0 - jnp.arange(RET_HEADS, dtype=F32)))
    w_in_t = jnp.swapaxes(w_in, 1, 2)

    for l in range(depth):
        wt = w_in_t[l]
        w_sig = wt[start["m_a"]:start["m_a"] + 2 * D]
        bg = wt[start["n_bg"]:start["n_bg"] + NSA_HEADS * 3].reshape(G, NSA_REP * 3, D)
        bg = jnp.pad(bg, ((0, 0), (0, LANES - NSA_REP * 3), (0, 0))).reshape(G * LANES, D)
        w_f = jnp.concatenate([wt[start["n_kc"]:start["n_kc"] + 2 * KVW], bg], 0)

        mod = _mod(c, w_ada[l], b_ada[l])
        shift, scl, gate = mod[:, :D], mod[:, D:2 * D], mod[:, 2 * D:]
        h3, cos3, sin3 = _h(x, g_norm[l], scl, shift, positions, inv2)
        h2, cos, sin = h3.reshape(T, D), cos3.reshape(T, HD), sin3.reshape(T, HD)
        cos_c, sin_c = cos3[:, cend], sin3[:, cend]

        P2 = _proj(h2, w_sig, w_in_t, l, cos, sin, tiles=tiles, src_tiles=src_tiles, tm=min(PROJ_TM, T))
        kv, gts = _projf(h2, w_f, n_kv=2 * G, tm=min(1024, T))
        P3 = P2.reshape(B, S, -1)

        hps = RET_HPS
        y_ret = _retention(P3, log_gamma, g_ret[l], q_blk=blk["r_q"] // hps, k_blk=blk["r_k"] // hps,
                           v_blk=blk["r_v"] // hps, rg_blk=blk["r_g"] // hps)

        xh = kv.reshape(2 * G, B, S, HD)
        half = CMP_BLOCK // 2
        pe2 = jnp.stack([pe_ck[l].reshape(2, half * HD), pe_cv[l].reshape(2, half * HD)])
        w1s = jnp.stack([w_ck1[l], w_cv1[l]]).astype(BF16)
        w2s = jnp.stack([w_ck2[l], w_cv2[l]]).astype(BF16)
        kcv = _compress(xh, pe2, w1s, w2s, cos_c, sin_c)

        y_nsa = _nsa(P3, gts.reshape(B, S, G * LANES), kcv, q_blk4=blk["n_q"] // NSA_REP,
                     ks_blk=blk["n_ks"], vs_blk=blk["n_vs"], kw_blk=blk["n_kw"], vw_blk=blk["n_vw"],
                     ng_blk4=blk["n_g"] // NSA_REP, gt_blk=0)

        merged = _up(y_ret.reshape(T, RW), y_nsa.reshape(T, NW), w_up_ret[l].astype(BF16),
                     w_up_nsa[l].astype(BF16), P2, tm=min(512, T))
        x = _out(merged.reshape(B, S, D), w_out[l].astype(BF16), x, gate, g_final,
                 ts=min(512, S), final_norm=(l + 1 == depth))
    return x
```

```python
import functools
import math

import jax
import jax.numpy as jnp
from jax import lax
from jax.experimental import pallas as pl
from jax.experimental.pallas import tpu as pltpu

F32 = jnp.float32
BF16 = jnp.bfloat16

HD = 128
RET_HEADS = 8
NSA_HEADS = 8
NSA_GROUPS = 2
NSA_REP = NSA_HEADS // NSA_GROUPS
CMP_BLOCK = 32
CMP_STRIDE = 16
CMP_HIDDEN = 256
SEL_BLOCK = 64
SEL_TOPN = 16
WINDOW = 512
FORCE_BONUS = 1.0e4
ROPE_THETA = 10000.0
EPS = 1e-6
NEG = -1.0e30

LANES = 128
PROJ_TN = 512
PROJ_TM = 2048
PROJ_RC = 512
H_RC = 256
RET_CHUNK = 256
RET_HPS = 4
NSA_TQ = 256
LOG2E = math.log2(math.e)
VMEM_LIMIT = 56 * 1024 * 1024


def _sigmoid(v):
    return 1.0 / (1.0 + jnp.exp(-v))


def _dot(a, b):
    return jnp.dot(a, b, preferred_element_type=F32)


def _dot_nt(a, b):
    return lax.dot_general(a, b, (((1,), (1,)), ((), ())), preferred_element_type=F32)


def _dot_tn(a, b):
    return lax.dot_general(a, b, (((0,), (0,)), ((), ())), preferred_element_type=F32)


def _rope(v, cos, sin_signed):
    return v * cos + pltpu.roll(v, HD // 2, 1) * sin_signed


def _mod_kernel(c_ref, w_ref, b_ref, o_ref):
    c = c_ref[...]
    sc = c * _sigmoid(c)
    o_ref[...] = jnp.dot(sc, w_ref[...], preferred_element_type=F32,
                         precision=lax.Precision.HIGHEST) + b_ref[...]


def _mod(c, w_ada, b_ada):
    B, D = c.shape
    N = w_ada.shape[1]
    tn = math.gcd(N, 512)
    return pl.pallas_call(
        _mod_kernel,
        grid=(N // tn,),
        in_specs=[pl.BlockSpec((B, D), lambda j: (0, 0)),
                  pl.BlockSpec((D, tn), lambda j: (0, j)),
                  pl.BlockSpec((1, tn), lambda j: (0, j))],
        out_specs=pl.BlockSpec((B, tn), lambda j: (0, j)),
        out_shape=jax.ShapeDtypeStruct((B, N), F32),
        name="mod",
    )(c, w_ada, b_ada.reshape(1, N))


def _h_kernel(x_ref, g_ref, scl_ref, sh_ref, pos_ref, inv_ref, wf_ref,
              h_ref, cos_ref, sin_ref, kv_ref, gt_ref, wfb_ref):
    @pl.when((pl.program_id(0) == 0) & (pl.program_id(1) == 0))
    def _():
        wfb_ref[...] = wf_ref[...].astype(BF16)

    n_kv = kv_ref.shape[0]
    ts = x_ref.shape[1]
    rc = min(H_RC, ts)
    gmul = g_ref[...] * (1.0 + scl_ref[0])
    for r in range(ts // rc):
        rows = slice(r * rc, (r + 1) * rc)
        x = x_ref[0, rows, :]
        ms = jnp.mean(x * x, -1, keepdims=True)
        hb = (x * lax.rsqrt(ms + EPS) * gmul + sh_ref[0]).astype(BF16)
        h_ref[0, rows, :] = hb
        acc = _dot_nt(hb, wfb_ref[...])
        for c in range(n_kv):
            kv_ref[c, 0, rows, :] = acc[:, c * HD:(c + 1) * HD]
        gt_ref[0, rows, :] = acc[:, n_kv * HD:]
    ang = pos_ref[0].astype(F32) * inv_ref[...]
    lane = lax.broadcasted_iota(jnp.int32, (1, HD), 1)
    cos_ref[0] = jnp.cos(ang)
    sin_ref[0] = jnp.where(lane < HD // 2, -1.0, 1.0) * jnp.sin(ang)


def _h(x, g_norm, scl, shift, positions, inv2, w_f_t, *, n_kv):
    B, S, D = x.shape
    N = w_f_t.shape[0]
    ts = 512
    return pl.pallas_call(
        _h_kernel,
        grid=(B, S // ts),
        in_specs=[pl.BlockSpec((1, ts, D), lambda b, i: (b, i, 0)),
                  pl.BlockSpec((1, D), lambda b, i: (0, 0)),
                  pl.BlockSpec((1, 1, D), lambda b, i: (b, 0, 0)),
                  pl.BlockSpec((1, 1, D), lambda b, i: (b, 0, 0)),
                  pl.BlockSpec((1, ts, 1), lambda b, i: (b, i, 0)),
                  pl.BlockSpec((1, HD), lambda b, i: (0, 0)),
                  pl.BlockSpec((N, D), lambda b, i: (0, 0))],
        out_specs=[pl.BlockSpec((1, ts, D), lambda b, i: (b, i, 0)),
                   pl.BlockSpec((1, ts, HD), lambda b, i: (b, i, 0)),
                   pl.BlockSpec((1, ts, HD), lambda b, i: (b, i, 0)),
                   pl.BlockSpec((n_kv, 1, ts, HD), lambda b, i: (0, b, i, 0)),
                   pl.BlockSpec((1, ts, N - n_kv * HD), lambda b, i: (b, i, 0))],
        out_shape=[jax.ShapeDtypeStruct((B, S, D), BF16),
                   jax.ShapeDtypeStruct((B, S, HD), F32),
                   jax.ShapeDtypeStruct((B, S, HD), F32),
                   jax.ShapeDtypeStruct((n_kv, B, S, HD), F32),
                   jax.ShapeDtypeStruct((B, S, N - n_kv * HD), F32)],
        scratch_shapes=[pltpu.VMEM((N, D), BF16)],
        compiler_params=pltpu.CompilerParams(
            dimension_semantics=("arbitrary", "arbitrary"), vmem_limit_bytes=VMEM_LIMIT),
        name="h",
    )(x, g_norm.reshape(1, D), scl.reshape(B, 1, D), shift.reshape(B, 1, D),
      positions.reshape(B, S, 1), inv2, w_f_t)


def _tile_cond(j, idxs):
    cond = None
    run_lo = prev = None
    for t in list(idxs) + [None]:
        if run_lo is None:
            run_lo = prev = t
            continue
        if t is not None and t == prev + 1:
            prev = t
            continue
        c = (j >= run_lo) & (j <= prev)
        cond = c if cond is None else (cond | c)
        run_lo = prev = t
    return cond


def _proj_kernel(h_ref, ws_ref, w_ref, cos_ref, sin_ref, o_ref, *, tiles):
    j = pl.program_id(1)
    tm = h_ref.shape[0]
    rc = min(PROJ_RC, tm)
    half = PROJ_TN // 2

    def row_chunks(w, epilogue):
        for c in range(tm // rc):
            rows = slice(c * rc, (c + 1) * rc)
            epilogue(rows, _dot_nt(h_ref[rows, :], w))

    def of_kind(kind):
        return [t for t, (k, _) in enumerate(tiles) if k == kind]

    def rope_cols(rows, acc, cos, sin, lo, hi):
        for hh in range(lo // HD, hi // HD):
            sl = slice(hh * HD, (hh + 1) * HD)
            o_ref[rows, sl] = _rope(acc[:, sl], cos, sin).astype(o_ref.dtype)

    @pl.when(_tile_cond(j, of_kind("sig")))
    def _():
        def epi(rows, acc):
            o_ref[rows, :] = _sigmoid(acc).astype(o_ref.dtype)
        row_chunks(ws_ref[...].astype(BF16), epi)

    @pl.when(_tile_cond(j, of_kind("rope")))
    def _():
        f = jnp.float32(1.0)
        for scale in sorted({s for k, s in tiles if k == "rope" and s != 1.0}):
            f = jnp.where(_tile_cond(j, [t for t, (k, s) in enumerate(tiles) if k == "rope" and s == scale]),
                          jnp.float32(scale), f)

        def epi(rows, acc):
            rope_cols(rows, acc, cos_ref[rows, :] * f, sin_ref[rows, :] * f, 0, PROJ_TN)
        row_chunks(w_ref[...].astype(BF16), epi)

    @pl.when(_tile_cond(j, of_kind("rope_plain")))
    def _():
        def epi(rows, acc):
            rope_cols(rows, acc, cos_ref[rows, :], sin_ref[rows, :], 0, half)
            o_ref[rows, half:] = acc[:, half:].astype(o_ref.dtype)
        row_chunks(w_ref[...].astype(BF16), epi)

    @pl.when(_tile_cond(j, of_kind("silu")))
    def _():
        def epi(rows, acc):
            o_ref[rows, :] = (acc * _sigmoid(acc)).astype(o_ref.dtype)
        row_chunks(w_ref[...].astype(BF16), epi)

    @pl.when(_tile_cond(j, of_kind("plain")))
    def _():
        def epi(rows, acc):
            o_ref[rows, :] = acc.astype(o_ref.dtype)
        row_chunks(w_ref[...].astype(BF16), epi)


def _proj(h2, w_sig_t, w_in_t, layer, cos, sin, *, tiles, src_tiles, tm):
    T, D = h2.shape
    n_sig = w_sig_t.shape[0] // PROJ_TN
    n_tiles = len(tiles)
    assert all(src_tiles[t + 1] - src_tiles[t] in (1, 2) for t in range(n_sig, n_tiles - 1))
    skips = [t for t in range(n_sig, n_tiles - 1) if src_tiles[t + 1] - src_tiles[t] == 2]

    def src(j):
        s = jnp.maximum(j - n_sig, 0) + src_tiles[n_sig]
        for t in skips:
            s = s + jnp.where(j > t, 1, 0)
        return s

    kern = functools.partial(_proj_kernel, tiles=tuple(tiles))
    return pl.pallas_call(
        kern,
        grid=(T // tm, n_tiles),
        in_specs=[pl.BlockSpec((tm, D), lambda i, j: (i, 0)),
                  pl.BlockSpec((PROJ_TN, D), lambda i, j: (jnp.minimum(j, n_sig - 1), 0)),
                  pl.BlockSpec((None, PROJ_TN, D), lambda i, j: (layer, src(j), 0)),
                  pl.BlockSpec((tm, HD), lambda i, j: (i, 0)),
                  pl.BlockSpec((tm, HD), lambda i, j: (i, 0))],
        out_specs=pl.BlockSpec((tm, PROJ_TN), lambda i, j: (i, j)),
        out_shape=jax.ShapeDtypeStruct((T, n_tiles * PROJ_TN), BF16),
        compiler_params=pltpu.CompilerParams(
            dimension_semantics=("arbitrary", "arbitrary"), vmem_limit_bytes=VMEM_LIMIT),
        name="proj",
    )(h2, w_sig_t, w_in_t, cos, sin)


def _ret_kernel(lg_ref, q_ref, k_ref, v_ref, rg_ref, g_ref, o_ref):
    S = q_ref.shape[1]
    C = RET_CHUNK
    r = lax.broadcasted_iota(jnp.int32, (C, C), 0)
    c = lax.broadcasted_iota(jnp.int32, (C, C), 1)
    diff = (r - c).astype(F32)
    rowf = lax.broadcasted_iota(jnp.int32, (C, HD), 0).astype(F32)
    for hh in range(RET_HPS):
        cols = slice(hh * HD, (hh + 1) * HD)
        lg = lg_ref[pl.program_id(1) * RET_HPS + hh]
        dmask = jnp.where(diff >= 0, jnp.exp(lg * jnp.maximum(diff, 0.0)), 0.0)
        zeta = jnp.exp(lg * (C - 1 - rowf))
        xi = jnp.exp(lg * (rowf + 1.0))
        decay = jnp.exp(jnp.full((HD, HD), lg * C, F32))
        g = g_ref[:, cols]
        state = jnp.zeros((HD, HD), F32)
        for n in range(S // C):
            sl = slice(n * C, (n + 1) * C)
            q = q_ref[0, sl, cols]
            k = k_ref[0, sl, cols]
            v = v_ref[0, sl, cols]
            sc = _dot_nt(q, k) * dmask
            o = _dot(sc.astype(BF16), v)
            if n > 0:
                o = o + _dot(q, state.astype(BF16)) * xi
            if n + 1 < S // C:
                kz = (k.astype(F32) * zeta).astype(BF16)
                state = state * decay + _dot_tn(kz, v)
            mu = jnp.mean(o, -1, keepdims=True)
            d = o - mu
            var = jnp.mean(d * d, -1, keepdims=True)
            on = d * lax.rsqrt(var + EPS)
            o_ref[0, sl, cols] = (on * g * rg_ref[0, sl, cols].astype(F32)).astype(o_ref.dtype)


def _retention(P3, log_gamma, g_ret, *, q_blk, k_blk, v_blk, rg_blk):
    B, S, _ = P3.shape
    H = RET_HEADS
    W = RET_HPS * HD

    def col(off):
        return lambda b, h, lg: (b, 0, off + h)

    return pl.pallas_call(
        _ret_kernel,
        grid_spec=pltpu.PrefetchScalarGridSpec(
            num_scalar_prefetch=1,
            grid=(B, H // RET_HPS),
            in_specs=[pl.BlockSpec((1, S, W), col(q_blk)),
                      pl.BlockSpec((1, S, W), col(k_blk)),
                      pl.BlockSpec((1, S, W), col(v_blk)),
                      pl.BlockSpec((1, S, W), col(rg_blk)),
                      pl.BlockSpec((1, W), lambda b, h, lg: (0, h))],
            out_specs=pl.BlockSpec((1, S, W), lambda b, h, lg: (b, 0, h)),
        ),
        out_shape=jax.ShapeDtypeStruct((B, S, H * HD), BF16),
        compiler_params=pltpu.CompilerParams(vmem_limit_bytes=VMEM_LIMIT),
        name="ret",
    )(log_gamma, P3, P3, P3, P3, g_ret.reshape(1, H * HD))


def _cmp_kernel(x_ref, pe_ref, w1_ref, w2_ref, cos_ref, sin_ref, o_ref):
    half = (CMP_BLOCK // 2) * HD
    M = x_ref.shape[2] // CMP_STRIDE
    x = jnp.concatenate([x_ref[0, 0, pl.ds(p, M, stride=CMP_STRIDE), :] for p in range(CMP_STRIDE)], 1)
    xa = (x + pe_ref[0, 0:1, :]).astype(BF16)
    xb = (x + pe_ref[0, 1:2, :]).astype(BF16)
    a = _dot(xa, w1_ref[0, :half, :])
    b = _dot(xb, w1_ref[0, half:, :])
    nrow = a.shape[0]
    pre = a + pltpu.roll(b, nrow - 1, 0)
    hid = pre * _sigmoid(pre)
    out = _dot(hid.astype(BF16), w2_ref[0])
    roped = _rope(out, cos_ref[0], sin_ref[0])
    is_key = pl.program_id(1) < NSA_GROUPS
    o_ref[0, 0] = jnp.where(is_key, roped, out).astype(o_ref.dtype)


def _compress(xh, pe2, w1s, w2s, cos_c, sin_c):
    NC, B, S, _ = xh.shape
    G = NSA_GROUPS
    M = S // CMP_STRIDE
    W = CMP_STRIDE * HD
    return pl.pallas_call(
        _cmp_kernel,
        grid=(B, NC),
        in_specs=[pl.BlockSpec((1, 1, S, HD), lambda b, c: (c, b, 0, 0)),
                  pl.BlockSpec((1, 2, W), lambda b, c: (c // G, 0, 0)),
                  pl.BlockSpec((1, 2 * W, CMP_HIDDEN), lambda b, c: (c // G, 0, 0)),
                  pl.BlockSpec((1, CMP_HIDDEN, HD), lambda b, c: (c // G, 0, 0)),
                  pl.BlockSpec((1, M, HD), lambda b, c: (b, 0, 0)),
                  pl.BlockSpec((1, M, HD), lambda b, c: (b, 0, 0))],
        out_specs=pl.BlockSpec((1, 1, M, HD), lambda b, c: (b, c, 0, 0)),
        out_shape=jax.ShapeDtypeStruct((B, NC, M, HD), BF16),
        name="cmp",
    )(xh, pe2, w1s, w2s, cos_c, sin_c)


def _nsa_kernel(q_ref, ks_ref, vs_ref, kw_ref, vw_ref, kc_ref, vc_ref, gt_ref, ng_ref, o_ref,
                qa_ref, ow_ref, acc_ref, oc_ref, gs_ref, kaug_ref, vsaug_ref, vwaug_ref, wb_ref):
    TQ = q_ref.shape[1]
    S = ks_ref.shape[1]
    R = NSA_REP
    RT = R * TQ
    WSPAN = WINDOW + TQ
    NSEL = S // SEL_BLOCK
    qi = pl.program_id(2)
    t0 = qi * TQ

    @pl.when(qi == 0)
    def _():
        blk = lax.broadcasted_iota(jnp.int32, (S, HD), 0) >> 6
        onehot = jnp.where(blk == lax.broadcasted_iota(jnp.int32, (S, HD), 1), 1.0, 0.0)
        kaug_ref[:, 0:HD] = ks_ref[0]
        kaug_ref[:, HD:2 * HD] = onehot.astype(BF16)
        ones = jnp.ones((S, HD), BF16)
        vsaug_ref[:, 0:HD] = vs_ref[0]
        vsaug_ref[:, HD:2 * HD] = ones
        vwaug_ref[:, 0:HD] = vw_ref[0]
        vwaug_ref[:, HD:2 * HD] = ones
        r = lax.broadcasted_iota(jnp.int32, (TQ, WSPAN), 0)
        c = lax.broadcasted_iota(jnp.int32, (TQ, WSPAN), 1)
        wb_ref[0] = jnp.where(c <= r, 0.0, NEG)
        wb_ref[1] = jnp.where(c <= r + TQ, 0.0, NEG)
        wb_ref[2] = jnp.where((c > r) & (c <= r + WINDOW), 0.0, NEG)

    for h in range(R):
        qa_ref[h * TQ:(h + 1) * TQ, 0:HD] = q_ref[0, :, h * HD:(h + 1) * HD]
    q4 = qa_ref[:, 0:HD]

    offw = pl.multiple_of(jnp.maximum(t0 - WINDOW, 0), TQ)
    kwin = kw_ref[0, pl.ds(offw, WSPAN), :]
    vwin = vwaug_ref[pl.ds(offw, WSPAN), :]
    wbias = wb_ref[jnp.minimum(qi, 2)]

    def win_scores(h):
        return _dot_nt(q_ref[0, :, h * HD:(h + 1) * HD], kwin) + wbias

    gt = _sigmoid(gt_ref[0])

    def gate(h, branch):
        c = 3 * h + branch
        return jnp.broadcast_to(gt[:, c:c + 1], (TQ, HD))

    def win_finish(h, sw):
        pw = jnp.exp2(sw - jnp.max(sw, -1, keepdims=True))
        ow = _dot(pw.astype(BF16), vwin)
        ow_ref[h * TQ:(h + 1) * TQ, :] = gate(h, 2) * (ow[:, 0:HD] / ow[:, HD:2 * HD])

    for h in range(R):
        gs_ref[h * TQ:(h + 1) * TQ, :] = gate(h, 1)
    sw0 = win_scores(0)

    s = _dot_nt(q4, kc_ref[0, 0])
    trow = t0 + (lax.broadcasted_iota(jnp.int32, (RT, HD), 0) & (TQ - 1))
    ncol = lax.broadcasted_iota(jnp.int32, (RT, HD), 1)
    s = jnp.where(ncol * CMP_STRIDE + (CMP_BLOCK - 1) <= trow, s, NEG)
    mx = jnp.max(s, -1, keepdims=True)
    p = jnp.exp2(s - mx)
    tcol = t0 + (lax.broadcasted_iota(jnp.int32, (RT, 1), 0) & (TQ - 1))
    any_valid = jnp.where(tcol >= CMP_BLOCK - 1, 1.0, 0.0)
    p = p * (any_valid / jnp.sum(p, -1, keepdims=True))
    sw1 = win_scores(1)
    win_finish(0, sw0)
    o_cmp = _dot(p.astype(BF16), vc_ref[0, 0])
    for h in range(R):
        oc_ref[h * TQ:(h + 1) * TQ, :] = gate(h, 0) * o_cmp[h * TQ:(h + 1) * TQ]

    psum = p[0:TQ]
    for h in range(1, R):
        psum = psum + p[h * TQ:(h + 1) * TQ]
    p_hi = psum.astype(BF16)
    rem = psum - p_hi.astype(F32)
    p_mid = rem.astype(BF16)
    p_lo = (rem - p_mid.astype(F32)).astype(BF16)
    jj = lax.broadcasted_iota(jnp.int32, (NSEL, HD), 0)
    nn = lax.broadcasted_iota(jnp.int32, (NSEL, HD), 1)
    ov = jnp.maximum(jnp.minimum(nn * CMP_STRIDE + CMP_BLOCK, (jj + 1) * SEL_BLOCK)
                     - jnp.maximum(nn * CMP_STRIDE, jj * SEL_BLOCK), 0).astype(F32) * (1.0 / CMP_BLOCK)
    ov = ov.astype(BF16)
    imp = (_dot_nt(ov, p_lo) + _dot_nt(ov, p_mid)) + _dot_nt(ov, p_hi)
    sw2 = win_scores(2)
    win_finish(1, sw1)

    j = lax.broadcasted_iota(jnp.int32, (NSEL, TQ), 0)
    tb = (t0 + lax.broadcasted_iota(jnp.int32, (NSEL, TQ), 1)) >> 6
    forced = (j == 0) | (j == tb) | (j == tb - 1)
    causal = j <= tb
    sc = jnp.where(causal, imp + jnp.where(forced, FORCE_BONUS, 0.0), NEG)
    rank = jnp.zeros((NSEL, TQ), F32)
    for i in range(NSEL):
        si = sc[i:i + 1, :]
        rank = rank + jnp.where(j > i, jnp.where(si >= sc, 1.0, 0.0), jnp.where(si > sc, 1.0, 0.0))
    sel = jnp.where((rank < SEL_TOPN) & causal, 1.0, 0.0)
    sw3 = win_scores(3)
    win_finish(2, sw2)

    selpad = jnp.concatenate([sel, jnp.zeros((HD - NSEL, TQ), F32)], 0).astype(BF16)
    eye = jnp.where(lax.broadcasted_iota(jnp.int32, (TQ, TQ), 0)
                    == lax.broadcasted_iota(jnp.int32, (TQ, TQ), 1), 1.0, 0.0).astype(BF16)
    sel_t = _dot_nt(eye, selpad)
    lane = lax.broadcasted_iota(jnp.int32, (TQ, HD), 1)
    bias = jnp.where((lane < NSEL) & (sel_t < 0.5), NEG, 0.0).astype(BF16)
    for h in range(R):
        qa_ref[h * TQ:(h + 1) * TQ, HD:2 * HD] = bias
    win_finish(3, sw3)

    for v in range(S // TQ):
        @pl.when(qi == v)
        def _(v=v):
            past = v * TQ
            for h in range(R):
                rows = slice(h * TQ, (h + 1) * TQ)
                qh = qa_ref[rows, :]
                sd = _dot_nt(qh, kaug_ref[past:past + TQ, :]) + wb_ref[0, :, 0:TQ]
                mx = jnp.max(sd, -1, keepdims=True)
                if past:
                    sp = _dot_nt(qh, kaug_ref[0:past, :])
                    mx = jnp.maximum(mx, jnp.max(sp, -1, keepdims=True))
                pv = _dot(jnp.exp2(sd - mx).astype(BF16), vsaug_ref[past:past + TQ, :])
                if past:
                    pv = pv + _dot(jnp.exp2(sp - mx).astype(BF16), vsaug_ref[0:past, :])
                acc_ref[rows, :] = pv

    for h in range(R):
        rows = slice(h * TQ, (h + 1) * TQ)
        o = (oc_ref[rows, :] + gs_ref[rows, :] * (acc_ref[rows, 0:HD] / acc_ref[rows, HD:2 * HD])
             + ow_ref[rows, :])
        o_ref[0, :, h * HD:(h + 1) * HD] = (o * ng_ref[0, :, h * HD:(h + 1) * HD].astype(F32)).astype(o_ref.dtype)


def _nsa(P3, F3, kcv, *, q_blk4, ks_blk, vs_blk, kw_blk, vw_blk, ng_blk4, gt_blk):
    B, S, _ = P3.shape
    G, R, TQ = NSA_GROUPS, NSA_REP, NSA_TQ
    assert WINDOW == 2 * TQ and SEL_BLOCK == 64 and S % (2 * TQ) == 0 and S >= WINDOW + TQ
    M = kcv.shape[2]
    kv_spec = lambda blk: pl.BlockSpec((1, S, HD), lambda b, g, i: (b, 0, blk + g))
    return pl.pallas_call(
        _nsa_kernel,
        grid=(B, G, S // TQ),
        in_specs=[pl.BlockSpec((1, TQ, R * HD), lambda b, g, i: (b, i, q_blk4 + g)),
                  kv_spec(ks_blk), kv_spec(vs_blk), kv_spec(kw_blk), kv_spec(vw_blk),
                  pl.BlockSpec((1, 1, M, HD), lambda b, g, i: (b, g, 0, 0)),
                  pl.BlockSpec((1, 1, M, HD), lambda b, g, i: (b, G + g, 0, 0)),
                  pl.BlockSpec((1, TQ, LANES), lambda b, g, i: (b, i, gt_blk + g)),
                  pl.BlockSpec((1, TQ, R * HD), lambda b, g, i: (b, i, ng_blk4 + g))],
        out_specs=pl.BlockSpec((1, TQ, R * HD), lambda b, g, i: (b, i, g)),
        out_shape=jax.ShapeDtypeStruct((B, S, NSA_HEADS * HD), BF16),
        scratch_shapes=[pltpu.VMEM((R * TQ, 2 * HD), BF16),
                        pltpu.VMEM((R * TQ, HD), F32),
                        pltpu.VMEM((R * TQ, 2 * HD), F32),
                        pltpu.VMEM((R * TQ, HD), F32),
                        pltpu.VMEM((R * TQ, HD), F32),
                        pltpu.VMEM((S, 2 * HD), BF16),
                        pltpu.VMEM((S, 2 * HD), BF16),
                        pltpu.VMEM((S, 2 * HD), BF16),
                        pltpu.VMEM((3, TQ, WINDOW + TQ), F32)],
        compiler_params=pltpu.CompilerParams(
            dimension_semantics=("arbitrary", "arbitrary", "arbitrary"), vmem_limit_bytes=VMEM_LIMIT),
        name="nsa",
    )(P3, P3, P3, P3, P3, kcv, kcv, F3, P3)


def _up_kernel(yr_ref, yn_ref, wr_ref, wn_ref, sa_ref, sb_ref, o_ref):
    a = _dot(yr_ref[...], wr_ref[...])
    b = _dot(yn_ref[...], wn_ref[...])
    o_ref[...] = (sa_ref[...].astype(F32) * a + sb_ref[...].astype(F32) * b).astype(o_ref.dtype)


def _up(y_ret, y_nsa, w_ur, w_un, P2, *, tm):
    T, W = y_ret.shape
    D = w_ur.shape[1]
    return pl.pallas_call(
        _up_kernel,
        grid=(T // tm,),
        in_specs=[pl.BlockSpec((tm, W), lambda i: (i, 0)),
                  pl.BlockSpec((tm, W), lambda i: (i, 0)),
                  pl.BlockSpec((W, D), lambda i: (0, 0)),
                  pl.BlockSpec((W, D), lambda i: (0, 0)),
                  pl.BlockSpec((tm, D), lambda i: (i, 0)),
                  pl.BlockSpec((tm, D), lambda i: (i, 1))],
        out_specs=pl.BlockSpec((tm, D), lambda i: (i, 0)),
        out_shape=jax.ShapeDtypeStruct((T, D), BF16),
        compiler_params=pltpu.CompilerParams(vmem_limit_bytes=VMEM_LIMIT),
        name="up",
    )(y_ret, y_nsa, w_ur, w_un, P2, P2)


def _out_kernel(m_ref, w_ref, x_ref, gate_ref, g_ref, o_ref, *, final_norm):
    y = x_ref[0] + gate_ref[0] * _dot(m_ref[0], w_ref[...])
    if final_norm:
        ms = jnp.mean(y * y, -1, keepdims=True)
        y = y * lax.rsqrt(ms + EPS) * g_ref[...]
    o_ref[0] = y


def _out(merged3, w_out, x, gate, g_final, *, ts, final_norm):
    B, S, D = x.shape
    return pl.pallas_call(
        functools.partial(_out_kernel, final_norm=final_norm),
        grid=(B, S // ts),
        in_specs=[pl.BlockSpec((1, ts, D), lambda b, i: (b, i, 0)),
                  pl.BlockSpec((D, D), lambda b, i: (0, 0)),
                  pl.BlockSpec((1, ts, D), lambda b, i: (b, i, 0)),
                  pl.BlockSpec((1, 1, D), lambda b, i: (b, 0, 0)),
                  pl.BlockSpec((1, D), lambda b, i: (0, 0))],
        out_specs=pl.BlockSpec((1, ts, D), lambda b, i: (b, i, 0)),
        out_shape=jax.ShapeDtypeStruct((B, S, D), F32),
        compiler_params=pltpu.CompilerParams(vmem_limit_bytes=VMEM_LIMIT),
        name="out",
    )(merged3, w_out, x, gate.reshape(B, 1, D), g_final.reshape(1, D))


def kernel(x, c, positions, w_ada, b_ada, g_norm, w_in, g_ret, w_ck1, w_ck2, pe_ck,
           w_cv1, w_cv2, pe_cv, w_up_ret, w_up_nsa, w_out, g_final):
    B, S, D = x.shape
    T = B * S
    depth = w_ada.shape[0]
    G = NSA_GROUPS
    RW = RET_HEADS * HD
    NW = NSA_HEADS * HD
    KVW = G * HD
    per_tile = PROJ_TN // LANES

    cols = (("r_q", RW, "rope", 1.0), ("r_k", RW, "rope", HD ** -0.5), ("r_v", RW, "plain", 1.0),
            ("r_g", RW, "silu", 1.0), ("n_q", NW, "rope", HD ** -0.5 * LOG2E),
            ("n_kc", KVW, "f32", 1.0), ("n_vc", KVW, "f32", 1.0),
            ("n_ks", KVW, "rope", 1.0), ("n_vs", KVW, "plain", 1.0),
            ("n_kw", KVW, "rope", 1.0), ("n_vw", KVW, "plain", 1.0), ("n_g", NW, "silu", 1.0))
    start = {}
    kinds = []
    off = 0
    for nm, wd, kind, scale in cols:
        start[nm] = off
        kinds += [(nm, kind, scale)] * (wd // LANES)
        off += wd
    start["n_bg"] = off
    start["m_a"] = off + NSA_HEADS * 3
    assert off % PROJ_TN == 0 and (2 * D) % PROJ_TN == 0

    n_sig = 2 * D // PROJ_TN
    tiles = [("sig", 1.0)] * n_sig
    src_tiles = [0] * n_sig
    blk = {"m_a": 0, "m_b": D // LANES}
    for t in range(off // PROJ_TN):
        group = kinds[t * per_tile:(t + 1) * per_tile]
        tkinds = [k for _, k, _ in group]
        if all(k == "f32" for k in tkinds):
            continue
        if len(set(tkinds)) == 1:
            assert len({s for _, _, s in group}) == 1
            tiles.append((tkinds[0], group[0][2]))
        else:
            half = per_tile // 2
            assert tkinds == ["rope"] * half + ["plain"] * half and all(s == 1.0 for _, _, s in group)
            tiles.append(("rope_plain", 1.0))
        for i, (nm, _, _) in enumerate(group):
            blk.setdefault(nm, (len(tiles) - 1) * per_tile + i)
        src_tiles.append(t)

    inv = jnp.exp(jnp.arange(0, HD, 2, dtype=F32) * (-math.log(ROPE_THETA) / HD))
    inv2 = jnp.concatenate([inv, inv]).reshape(1, HD)
    cend = jnp.arange(S // CMP_STRIDE) * CMP_STRIDE + (CMP_BLOCK - 1)
    cend = jnp.minimum(cend, S - 1)
    log_gamma = jnp.log1p(-jnp.exp2(-5.0 - jnp.arange(RET_HEADS, dtype=F32)))
    w_in_t = jnp.swapaxes(w_in, 1, 2)

    for l in range(depth):
        wt = w_in_t[l]
        w_sig = wt[start["m_a"]:start["m_a"] + 2 * D]
        bg = wt[start["n_bg"]:start["n_bg"] + NSA_HEADS * 3].reshape(G, NSA_REP * 3, D)
        bg = jnp.pad(bg, ((0, 0), (0, LANES - NSA_REP * 3), (0, 0))).reshape(G * LANES, D)
        w_f = jnp.concatenate([wt[start["n_kc"]:start["n_kc"] + 2 * KVW], bg], 0)

        mod = _mod(c, w_ada[l], b_ada[l])
        shift, scl, gate = mod[:, :D], mod[:, D:2 * D], mod[:, 2 * D:]
        h3, cos3, sin3, xh, gts = _h(x, g_norm[l], scl, shift, positions, inv2, w_f, n_kv=2 * G)
        h2, cos, sin = h3.reshape(T, D), cos3.reshape(T, HD), sin3.reshape(T, HD)
        cos_c, sin_c = cos3[:, cend], sin3[:, cend]

        P2 = _proj(h2, w_sig, w_in_t, l, cos, sin, tiles=tiles, src_tiles=src_tiles, tm=min(PROJ_TM, T))
        P3 = P2.reshape(B, S, -1)

        hps = RET_HPS
        y_ret = _retention(P3, log_gamma, g_ret[l], q_blk=blk["r_q"] // hps, k_blk=blk["r_k"] // hps,
                           v_blk=blk["r_v"] // hps, rg_blk=blk["r_g"] // hps)

        half = CMP_BLOCK // 2
        pe2 = jnp.stack([pe_ck[l].reshape(2, half * HD), pe_cv[l].reshape(2, half * HD)])
        w1s = jnp.stack([w_ck1[l], w_cv1[l]]).astype(BF16)
        w2s = jnp.stack([w_ck2[l], w_cv2[l]]).astype(BF16)
        kcv = _compress(xh, pe2, w1s, w2s, cos_c, sin_c)

        y_nsa = _nsa(P3, gts, kcv, q_blk4=blk["n_q"] // NSA_REP,
                     ks_blk=blk["n_ks"], vs_blk=blk["n_vs"], kw_blk=blk["n_kw"], vw_blk=blk["n_vw"],
                     ng_blk4=blk["n_g"] // NSA_REP, gt_blk=0)

        merged = _up(y_ret.reshape(T, RW), y_nsa.reshape(T, NW), w_up_ret[l].astype(BF16),
                     w_up_nsa[l].astype(BF16), P2, tm=min(512, T))
        x = _out(merged.reshape(B, S, D), w_out[l].astype(BF16), x, gate, g_final,
                 ts=min(512, S), final_norm=(l + 1 == depth))
    return x
```

```python
import functools
import math

import jax
import jax.numpy as jnp
from jax import lax
from jax.experimental import pallas as pl
from jax.experimental.pallas import tpu as pltpu

F32 = jnp.float32
BF16 = jnp.bfloat16

HD = 128
RET_HEADS = 8
NSA_HEADS = 8
NSA_GROUPS = 2
NSA_REP = NSA_HEADS // NSA_GROUPS
CMP_BLOCK = 32
CMP_STRIDE = 16
CMP_HIDDEN = 256
SEL_BLOCK = 64
SEL_TOPN = 16
WINDOW = 512
FORCE_BONUS = 1.0e4
ROPE_THETA = 10000.0
EPS = 1e-6
NEG = -1.0e30

LANES = 128
PROJ_TN = 1024
PROJ_TM = 2048
PROJ_RC = 512
H_RC = 256
RET_CHUNK = 256
RET_HPS = 4
NSA_TQ = 256
LOG2E = math.log2(math.e)
VMEM_LIMIT = 56 * 1024 * 1024


def _sigmoid(v):
    return 1.0 / (1.0 + jnp.exp(-v))


def _dot(a, b):
    return jnp.dot(a, b, preferred_element_type=F32)


def _dot_nt(a, b):
    return lax.dot_general(a, b, (((1,), (1,)), ((), ())), preferred_element_type=F32)


def _dot_tn(a, b):
    return lax.dot_general(a, b, (((0,), (0,)), ((), ())), preferred_element_type=F32)


def _rope(v, cos, sin_signed):
    return v * cos + pltpu.roll(v, HD // 2, 1) * sin_signed


def _mod_kernel(c_ref, w_ref, b_ref, o_ref):
    c = c_ref[...]
    sc = c * _sigmoid(c)
    o_ref[...] = jnp.dot(sc, w_ref[...], preferred_element_type=F32,
                         precision=lax.Precision.HIGHEST) + b_ref[...]


def _mod(c, w_ada, b_ada):
    B, D = c.shape
    N = w_ada.shape[1]
    tn = math.gcd(N, 512)
    return pl.pallas_call(
        _mod_kernel,
        grid=(N // tn,),
        in_specs=[pl.BlockSpec((B, D), lambda j: (0, 0)),
                  pl.BlockSpec((D, tn), lambda j: (0, j)),
                  pl.BlockSpec((1, tn), lambda j: (0, j))],
        out_specs=pl.BlockSpec((B, tn), lambda j: (0, j)),
        out_shape=jax.ShapeDtypeStruct((B, N), F32),
        name="mod",
    )(c, w_ada, b_ada.reshape(1, N))


def _h_kernel(x_ref, g_ref, scl_ref, sh_ref, pos_ref, inv_ref, wf_ref,
              h_ref, cos_ref, sin_ref, kv_ref, gt_ref, wfb_ref):
    @pl.when((pl.program_id(0) == 0) & (pl.program_id(1) == 0))
    def _():
        wfb_ref[...] = wf_ref[...].astype(BF16)

    n_kv = kv_ref.shape[0]
    ts = x_ref.shape[1]
    rc = min(H_RC, ts)
    gmul = g_ref[...] * (1.0 + scl_ref[0])
    for r in range(ts // rc):
        rows = slice(r * rc, (r + 1) * rc)
        x = x_ref[0, rows, :]
        ms = jnp.mean(x * x, -1, keepdims=True)
        hb = (x * lax.rsqrt(ms + EPS) * gmul + sh_ref[0]).astype(BF16)
        h_ref[0, rows, :] = hb
        acc = _dot_nt(hb, wfb_ref[...])
        for c in range(n_kv):
            kv_ref[c, 0, rows, :] = acc[:, c * HD:(c + 1) * HD]
        gt_ref[0, rows, :] = acc[:, n_kv * HD:]
    ang = pos_ref[0].astype(F32) * inv_ref[...]
    lane = lax.broadcasted_iota(jnp.int32, (1, HD), 1)
    cos_ref[0] = jnp.cos(ang)
    sin_ref[0] = jnp.where(lane < HD // 2, -1.0, 1.0) * jnp.sin(ang)


def _h(x, g_norm, scl, shift, positions, inv2, w_f_t, *, n_kv):
    B, S, D = x.shape
    N = w_f_t.shape[0]
    ts = 512
    return pl.pallas_call(
        _h_kernel,
        grid=(B, S // ts),
        in_specs=[pl.BlockSpec((1, ts, D), lambda b, i: (b, i, 0)),
                  pl.BlockSpec((1, D), lambda b, i: (0, 0)),
                  pl.BlockSpec((1, 1, D), lambda b, i: (b, 0, 0)),
                  pl.BlockSpec((1, 1, D), lambda b, i: (b, 0, 0)),
                  pl.BlockSpec((1, ts, 1), lambda b, i: (b, i, 0)),
                  pl.BlockSpec((1, HD), lambda b, i: (0, 0)),
                  pl.BlockSpec((N, D), lambda b, i: (0, 0))],
        out_specs=[pl.BlockSpec((1, ts, D), lambda b, i: (b, i, 0)),
                   pl.BlockSpec((1, ts, HD), lambda b, i: (b, i, 0)),
                   pl.BlockSpec((1, ts, HD), lambda b, i: (b, i, 0)),
                   pl.BlockSpec((n_kv, 1, ts, HD), lambda b, i: (0, b, i, 0)),
                   pl.BlockSpec((1, ts, N - n_kv * HD), lambda b, i: (b, i, 0))],
        out_shape=[jax.ShapeDtypeStruct((B, S, D), BF16),
                   jax.ShapeDtypeStruct((B, S, HD), F32),
                   jax.ShapeDtypeStruct((B, S, HD), F32),
                   jax.ShapeDtypeStruct((n_kv, B, S, HD), F32),
                   jax.ShapeDtypeStruct((B, S, N - n_kv * HD), F32)],
        scratch_shapes=[pltpu.VMEM((N, D), BF16)],
        compiler_params=pltpu.CompilerParams(
            dimension_semantics=("arbitrary", "arbitrary"), vmem_limit_bytes=VMEM_LIMIT),
        name="h",
    )(x, g_norm.reshape(1, D), scl.reshape(B, 1, D), shift.reshape(B, 1, D),
      positions.reshape(B, S, 1), inv2, w_f_t)


def _tile_cond(j, idxs):
    cond = None
    run_lo = prev = None
    for t in list(idxs) + [None]:
        if run_lo is None:
            run_lo = prev = t
            continue
        if t is not None and t == prev + 1:
            prev = t
            continue
        c = (j >= run_lo) & (j <= prev)
        cond = c if cond is None else (cond | c)
        run_lo = prev = t
    return cond


def _proj_kernel(h_ref, w_ref, cos_ref, sin_ref, o_ref, *, tiles):
    j = pl.program_id(1)
    tm = h_ref.shape[0]
    rc = min(PROJ_RC, tm)

    def row_chunks(w, epilogue):
        for c in range(tm // rc):
            rows = slice(c * rc, (c + 1) * rc)
            epilogue(rows, _dot_nt(h_ref[rows, :], w))

    shapes = {}
    for t, segs in enumerate(tiles):
        shapes.setdefault(tuple((k, wd) for k, _, wd in segs), []).append(t)

    for shape, idxs in shapes.items():
        @pl.when(_tile_cond(j, idxs))
        def _(shape=shape, idxs=idxs):
            scale = []
            for si in range(len(shape)):
                f = jnp.float32(1.0)
                for sc in sorted({tiles[t][si][1] for t in idxs} - {1.0}):
                    f = jnp.where(_tile_cond(j, [t for t in idxs if tiles[t][si][1] == sc]), jnp.float32(sc), f)
                scale.append(f)

            def epi(rows, acc):
                lo = 0
                for si, (kind, width) in enumerate(shape):
                    cols = slice(lo, lo + width)
                    if kind == "sig":
                        o_ref[rows, cols] = _sigmoid(acc[:, cols]).astype(o_ref.dtype)
                    elif kind == "silu":
                        o_ref[rows, cols] = (acc[:, cols] * _sigmoid(acc[:, cols])).astype(o_ref.dtype)
                    elif kind == "plain":
                        o_ref[rows, cols] = acc[:, cols].astype(o_ref.dtype)
                    else:
                        cos = cos_ref[rows, :] * scale[si]
                        sin = sin_ref[rows, :] * scale[si]
                        for hh in range(lo // HD, (lo + width) // HD):
                            sl = slice(hh * HD, (hh + 1) * HD)
                            o_ref[rows, sl] = _rope(acc[:, sl], cos, sin).astype(o_ref.dtype)
                    lo += width
            row_chunks(w_ref[0].astype(BF16), epi)


def _proj(h2, w_in_t, layer, cos, sin, *, tiles, offsets, tm):
    T, D = h2.shape
    n_tiles = len(tiles)

    def row_off(j):
        off = jnp.int32(offsets[0])
        for t in range(1, n_tiles):
            off = jnp.where(j == t, offsets[t], off)
        return pl.multiple_of(off, 8)

    kern = functools.partial(_proj_kernel, tiles=tuple(tiles))
    return pl.pallas_call(
        kern,
        grid=(T // tm, n_tiles),
        in_specs=[pl.BlockSpec((tm, D), lambda i, j: (i, 0)),
                  pl.BlockSpec((pl.Element(1), pl.Element(PROJ_TN), pl.Element(D)),
                               lambda i, j: (layer, row_off(j), 0)),
                  pl.BlockSpec((tm, HD), lambda i, j: (i, 0)),
                  pl.BlockSpec((tm, HD), lambda i, j: (i, 0))],
        out_specs=pl.BlockSpec((tm, PROJ_TN), lambda i, j: (i, j)),
        out_shape=jax.ShapeDtypeStruct((T, n_tiles * PROJ_TN), BF16),
        compiler_params=pltpu.CompilerParams(
            dimension_semantics=("arbitrary", "arbitrary"), vmem_limit_bytes=VMEM_LIMIT),
        name="proj",
    )(h2, w_in_t, cos, sin)


def _ret_kernel(lg_ref, q_ref, k_ref, v_ref, rg_ref, g_ref, o_ref):
    S = q_ref.shape[1]
    C = RET_CHUNK
    r = lax.broadcasted_iota(jnp.int32, (C, C), 0)
    c = lax.broadcasted_iota(jnp.int32, (C, C), 1)
    diff = (r - c).astype(F32)
    rowf = lax.broadcasted_iota(jnp.int32, (C, HD), 0).astype(F32)
    for hh in range(RET_HPS):
        cols = slice(hh * HD, (hh + 1) * HD)
        lg = lg_ref[pl.program_id(1) * RET_HPS + hh]
        dmask = jnp.where(diff >= 0, jnp.exp(lg * jnp.maximum(diff, 0.0)), 0.0)
        zeta = jnp.exp(lg * (C - 1 - rowf))
        xi = jnp.exp(lg * (rowf + 1.0))
        decay = jnp.exp(jnp.full((HD, HD), lg * C, F32))
        g = g_ref[:, cols]
        state = jnp.zeros((HD, HD), F32)
        for n in range(S // C):
            sl = slice(n * C, (n + 1) * C)
            q = q_ref[0, sl, cols]
            k = k_ref[0, sl, cols]
            v = v_ref[0, sl, cols]
            sc = _dot_nt(q, k) * dmask
            o = _dot(sc.astype(BF16), v)
            if n > 0:
                o = o + _dot(q, state.astype(BF16)) * xi
            if n + 1 < S // C:
                kz = (k.astype(F32) * zeta).astype(BF16)
                state = state * decay + _dot_tn(kz, v)
            mu = jnp.mean(o, -1, keepdims=True)
            d = o - mu
            var = jnp.mean(d * d, -1, keepdims=True)
            on = d * lax.rsqrt(var + EPS)
            o_ref[0, sl, cols] = (on * g * rg_ref[0, sl, cols].astype(F32)).astype(o_ref.dtype)


def _retention(P3, log_gamma, g_ret, *, q_blk, k_blk, v_blk, rg_blk):
    B, S, _ = P3.shape
    H = RET_HEADS
    W = RET_HPS * HD

    def col(off):
        return lambda b, h, lg: (b, 0, off + h)

    return pl.pallas_call(
        _ret_kernel,
        grid_spec=pltpu.PrefetchScalarGridSpec(
            num_scalar_prefetch=1,
            grid=(B, H // RET_HPS),
            in_specs=[pl.BlockSpec((1, S, W), col(q_blk)),
                      pl.BlockSpec((1, S, W), col(k_blk)),
                      pl.BlockSpec((1, S, W), col(v_blk)),
                      pl.BlockSpec((1, S, W), col(rg_blk)),
                      pl.BlockSpec((1, W), lambda b, h, lg: (0, h))],
            out_specs=pl.BlockSpec((1, S, W), lambda b, h, lg: (b, 0, h)),
        ),
        out_shape=jax.ShapeDtypeStruct((B, S, H * HD), BF16),
        compiler_params=pltpu.CompilerParams(vmem_limit_bytes=VMEM_LIMIT),
        name="ret",
    )(log_gamma, P3, P3, P3, P3, g_ret.reshape(1, H * HD))


def _cmp_kernel(x_ref, pe_ref, w1_ref, w2_ref, cos_ref, sin_ref, o_ref):
    half = (CMP_BLOCK // 2) * HD
    M = x_ref.shape[2] // CMP_STRIDE
    x = jnp.concatenate([x_ref[0, 0, pl.ds(p, M, stride=CMP_STRIDE), :] for p in range(CMP_STRIDE)], 1)
    xa = (x + pe_ref[0, 0:1, :]).astype(BF16)
    xb = (x + pe_ref[0, 1:2, :]).astype(BF16)
    a = _dot(xa, w1_ref[0, :half, :])
    b = _dot(xb, w1_ref[0, half:, :])
    nrow = a.shape[0]
    pre = a + pltpu.roll(b, nrow - 1, 0)
    hid = pre * _sigmoid(pre)
    out = _dot(hid.astype(BF16), w2_ref[0])
    roped = _rope(out, cos_ref[0], sin_ref[0])
    is_key = pl.program_id(1) < NSA_GROUPS
    o_ref[0, 0] = jnp.where(is_key, roped, out).astype(o_ref.dtype)


def _compress(xh, pe2, w1s, w2s, cos_c, sin_c):
    NC, B, S, _ = xh.shape
    G = NSA_GROUPS
    M = S // CMP_STRIDE
    W = CMP_STRIDE * HD
    return pl.pallas_call(
        _cmp_kernel,
        grid=(B, NC),
        in_specs=[pl.BlockSpec((1, 1, S, HD), lambda b, c: (c, b, 0, 0)),
                  pl.BlockSpec((1, 2, W), lambda b, c: (c // G, 0, 0)),
                  pl.BlockSpec((1, 2 * W, CMP_HIDDEN), lambda b, c: (c // G, 0, 0)),
                  pl.BlockSpec((1, CMP_HIDDEN, HD), lambda b, c: (c // G, 0, 0)),
                  pl.BlockSpec((1, M, HD), lambda b, c: (b, 0, 0)),
                  pl.BlockSpec((1, M, HD), lambda b, c: (b, 0, 0))],
        out_specs=pl.BlockSpec((1, 1, M, HD), lambda b, c: (b, c, 0, 0)),
        out_shape=jax.ShapeDtypeStruct((B, NC, M, HD), BF16),
        name="cmp",
    )(xh, pe2, w1s, w2s, cos_c, sin_c)


def _nsa_kernel(q_ref, ks_ref, vs_ref, kw_ref, vw_ref, kc_ref, vc_ref, gt_ref, ng_ref, o_ref,
                qa_ref, ow_ref, acc_ref, oc_ref, gs_ref, kaug_ref, vsaug_ref, vwaug_ref, wb_ref):
    TQ = q_ref.shape[1]
    S = ks_ref.shape[1]
    R = NSA_REP
    RT = R * TQ
    WSPAN = WINDOW + TQ
    NSEL = S // SEL_BLOCK
    qi = pl.program_id(2)
    t0 = qi * TQ

    @pl.when(qi == 0)
    def _():
        blk = lax.broadcasted_iota(jnp.int32, (S, HD), 0) >> 6
        onehot = jnp.where(blk == lax.broadcasted_iota(jnp.int32, (S, HD), 1), 1.0, 0.0)
        kaug_ref[:, 0:HD] = ks_ref[0]
        kaug_ref[:, HD:2 * HD] = onehot.astype(BF16)
        ones = jnp.ones((S, HD), BF16)
        vsaug_ref[:, 0:HD] = vs_ref[0]
        vsaug_ref[:, HD:2 * HD] = ones
        vwaug_ref[:, 0:HD] = vw_ref[0]
        vwaug_ref[:, HD:2 * HD] = ones
        r = lax.broadcasted_iota(jnp.int32, (TQ, WSPAN), 0)
        c = lax.broadcasted_iota(jnp.int32, (TQ, WSPAN), 1)
        wb_ref[0] = jnp.where(c <= r, 0.0, NEG)
        wb_ref[1] = jnp.where(c <= r + TQ, 0.0, NEG)
        wb_ref[2] = jnp.where((c > r) & (c <= r + WINDOW), 0.0, NEG)

    for h in range(R):
        qa_ref[h * TQ:(h + 1) * TQ, 0:HD] = q_ref[0, :, h * HD:(h + 1) * HD]
    q4 = qa_ref[:, 0:HD]

    offw = pl.multiple_of(jnp.maximum(t0 - WINDOW, 0), TQ)
    kwin = kw_ref[0, pl.ds(offw, WSPAN), :]
    vwin = vwaug_ref[pl.ds(offw, WSPAN), :]
    wbias = wb_ref[jnp.minimum(qi, 2)]

    def win_scores(h):
        return _dot_nt(q_ref[0, :, h * HD:(h + 1) * HD], kwin) + wbias

    gt = _sigmoid(gt_ref[0])

    def gate(h, branch):
        c = 3 * h + branch
        return jnp.broadcast_to(gt[:, c:c + 1], (TQ, HD))

    def win_finish(h, sw):
        pw = jnp.exp2(sw - jnp.max(sw, -1, keepdims=True))
        ow = _dot(pw.astype(BF16), vwin)
        ow_ref[h * TQ:(h + 1) * TQ, :] = gate(h, 2) * (ow[:, 0:HD] / ow[:, HD:2 * HD])

    for h in range(R):
        gs_ref[h * TQ:(h + 1) * TQ, :] = gate(h, 1)
    sw0 = win_scores(0)

    s = _dot_nt(q4, kc_ref[0, 0])
    trow = t0 + (lax.broadcasted_iota(jnp.int32, (RT, HD), 0) & (TQ - 1))
    ncol = lax.broadcasted_iota(jnp.int32, (RT, HD), 1)
    s = jnp.where(ncol * CMP_STRIDE + (CMP_BLOCK - 1) <= trow, s, NEG)
    mx = jnp.max(s, -1, keepdims=True)
    p = jnp.exp2(s - mx)
    tcol = t0 + (lax.broadcasted_iota(jnp.int32, (RT, 1), 0) & (TQ - 1))
    any_valid = jnp.where(tcol >= CMP_BLOCK - 1, 1.0, 0.0)
    p = p * (any_valid / jnp.sum(p, -1, keepdims=True))
    sw1 = win_scores(1)
    win_finish(0, sw0)
    o_cmp = _dot(p.astype(BF16), vc_ref[0, 0])
    for h in range(R):
        oc_ref[h * TQ:(h + 1) * TQ, :] = gate(h, 0) * o_cmp[h * TQ:(h + 1) * TQ]

    psum = p[0:TQ]
    for h in range(1, R):
        psum = psum + p[h * TQ:(h + 1) * TQ]
    p_hi = psum.astype(BF16)
    rem = psum - p_hi.astype(F32)
    p_mid = rem.astype(BF16)
    p_lo = (rem - p_mid.astype(F32)).astype(BF16)
    jj = lax.broadcasted_iota(jnp.int32, (NSEL, HD), 0)
    nn = lax.broadcasted_iota(jnp.int32, (NSEL, HD), 1)
    ov = jnp.maximum(jnp.minimum(nn * CMP_STRIDE + CMP_BLOCK, (jj + 1) * SEL_BLOCK)
                     - jnp.maximum(nn * CMP_STRIDE, jj * SEL_BLOCK), 0).astype(F32) * (1.0 / CMP_BLOCK)
    ov = ov.astype(BF16)
    imp = (_dot_nt(ov, p_lo) + _dot_nt(ov, p_mid)) + _dot_nt(ov, p_hi)
    sw2 = win_scores(2)
    win_finish(1, sw1)

    j = lax.broadcasted_iota(jnp.int32, (NSEL, TQ), 0)
    tb = (t0 + lax.broadcasted_iota(jnp.int32, (NSEL, TQ), 1)) >> 6
    forced = (j == 0) | (j == tb) | (j == tb - 1)
    causal = j <= tb
    sc = jnp.where(causal, imp + jnp.where(forced, FORCE_BONUS, 0.0), NEG)
    rank = jnp.zeros((NSEL, TQ), F32)
    for i in range(NSEL):
        si = sc[i:i + 1, :]
        rank = rank + jnp.where(j > i, jnp.where(si >= sc, 1.0, 0.0), jnp.where(si > sc, 1.0, 0.0))
    sel = jnp.where((rank < SEL_TOPN) & causal, 1.0, 0.0)
    sw3 = win_scores(3)
    win_finish(2, sw2)

    selpad = jnp.concatenate([sel, jnp.zeros((HD - NSEL, TQ), F32)], 0).astype(BF16)
    eye = jnp.where(lax.broadcasted_iota(jnp.int32, (TQ, TQ), 0)
                    == lax.broadcasted_iota(jnp.int32, (TQ, TQ), 1), 1.0, 0.0).astype(BF16)
    sel_t = _dot_nt(eye, selpad)
    lane = lax.broadcasted_iota(jnp.int32, (TQ, HD), 1)
    bias = jnp.where((lane < NSEL) & (sel_t < 0.5), NEG, 0.0).astype(BF16)
    for h in range(R):
        qa_ref[h * TQ:(h + 1) * TQ, HD:2 * HD] = bias
    win_finish(3, sw3)

    for v in range(S // TQ):
        @pl.when(qi == v)
        def _(v=v):
            past = v * TQ
            for h in range(R):
                rows = slice(h * TQ, (h + 1) * TQ)
                qh = qa_ref[rows, :]
                sd = _dot_nt(qh, kaug_ref[past:past + TQ, :]) + wb_ref[0, :, 0:TQ]
                mx = jnp.max(sd, -1, keepdims=True)
                if past:
                    sp = _dot_nt(qh, kaug_ref[0:past, :])
                    mx = jnp.maximum(mx, jnp.max(sp, -1, keepdims=True))
                pv = _dot(jnp.exp2(sd - mx).astype(BF16), vsaug_ref[past:past + TQ, :])
                if past:
                    pv = pv + _dot(jnp.exp2(sp - mx).astype(BF16), vsaug_ref[0:past, :])
                acc_ref[rows, :] = pv

    for h in range(R):
        rows = slice(h * TQ, (h + 1) * TQ)
        o = (oc_ref[rows, :] + gs_ref[rows, :] * (acc_ref[rows, 0:HD] / acc_ref[rows, HD:2 * HD])
             + ow_ref[rows, :])
        o_ref[0, :, h * HD:(h + 1) * HD] = (o * ng_ref[0, :, h * HD:(h + 1) * HD].astype(F32)).astype(o_ref.dtype)


def _nsa(P3, F3, kcv, *, q_blk4, ks_blk, vs_blk, kw_blk, vw_blk, ng_blk4, gt_blk):
    B, S, _ = P3.shape
    G, R, TQ = NSA_GROUPS, NSA_REP, NSA_TQ
    assert WINDOW == 2 * TQ and SEL_BLOCK == 64 and S % (2 * TQ) == 0 and S >= WINDOW + TQ
    M = kcv.shape[2]
    kv_spec = lambda blk: pl.BlockSpec((1, S, HD), lambda b, g, i: (b, 0, blk + g))
    return pl.pallas_call(
        _nsa_kernel,
        grid=(B, G, S // TQ),
        in_specs=[pl.BlockSpec((1, TQ, R * HD), lambda b, g, i: (b, i, q_blk4 + g)),
                  kv_spec(ks_blk), kv_spec(vs_blk), kv_spec(kw_blk), kv_spec(vw_blk),
                  pl.BlockSpec((1, 1, M, HD), lambda b, g, i: (b, g, 0, 0)),
                  pl.BlockSpec((1, 1, M, HD), lambda b, g, i: (b, G + g, 0, 0)),
                  pl.BlockSpec((1, TQ, LANES), lambda b, g, i: (b, i, gt_blk + g)),
                  pl.BlockSpec((1, TQ, R * HD), lambda b, g, i: (b, i, ng_blk4 + g))],
        out_specs=pl.BlockSpec((1, TQ, R * HD), lambda b, g, i: (b, i, g)),
        out_shape=jax.ShapeDtypeStruct((B, S, NSA_HEADS * HD), BF16),
        scratch_shapes=[pltpu.VMEM((R * TQ, 2 * HD), BF16),
                        pltpu.VMEM((R * TQ, HD), F32),
                        pltpu.VMEM((R * TQ, 2 * HD), F32),
                        pltpu.VMEM((R * TQ, HD), F32),
                        pltpu.VMEM((R * TQ, HD), F32),
                        pltpu.VMEM((S, 2 * HD), BF16),
                        pltpu.VMEM((S, 2 * HD), BF16),
                        pltpu.VMEM((S, 2 * HD), BF16),
                        pltpu.VMEM((3, TQ, WINDOW + TQ), F32)],
        compiler_params=pltpu.CompilerParams(
            dimension_semantics=("arbitrary", "arbitrary", "arbitrary"), vmem_limit_bytes=VMEM_LIMIT),
        name="nsa",
    )(P3, P3, P3, P3, P3, kcv, kcv, F3, P3)


def _up_kernel(yr_ref, yn_ref, wr_ref, wn_ref, sa_ref, sb_ref, o_ref):
    a = _dot(yr_ref[...], wr_ref[...])
    b = _dot(yn_ref[...], wn_ref[...])
    o_ref[...] = (sa_ref[...].astype(F32) * a + sb_ref[...].astype(F32) * b).astype(o_ref.dtype)


def _up(y_ret, y_nsa, w_ur, w_un, P2, *, tm):
    T, W = y_ret.shape
    D = w_ur.shape[1]
    return pl.pallas_call(
        _up_kernel,
        grid=(T // tm,),
        in_specs=[pl.BlockSpec((tm, W), lambda i: (i, 0)),
                  pl.BlockSpec((tm, W), lambda i: (i, 0)),
                  pl.BlockSpec((W, D), lambda i: (0, 0)),
                  pl.BlockSpec((W, D), lambda i: (0, 0)),
                  pl.BlockSpec((tm, D), lambda i: (i, 0)),
                  pl.BlockSpec((tm, D), lambda i: (i, 1))],
        out_specs=pl.BlockSpec((tm, D), lambda i: (i, 0)),
        out_shape=jax.ShapeDtypeStruct((T, D), BF16),
        compiler_params=pltpu.CompilerParams(vmem_limit_bytes=VMEM_LIMIT),
        name="up",
    )(y_ret, y_nsa, w_ur, w_un, P2, P2)


def _out_kernel(m_ref, w_ref, x_ref, gate_ref, g_ref, o_ref, *, final_norm):
    y = x_ref[0] + gate_ref[0] * _dot(m_ref[0], w_ref[...])
    if final_norm:
        ms = jnp.mean(y * y, -1, keepdims=True)
        y = y * lax.rsqrt(ms + EPS) * g_ref[...]
    o_ref[0] = y


def _out(merged3, w_out, x, gate, g_final, *, ts, final_norm):
    B, S, D = x.shape
    return pl.pallas_call(
        functools.partial(_out_kernel, final_norm=final_norm),
        grid=(B, S // ts),
        in_specs=[pl.BlockSpec((1, ts, D), lambda b, i: (b, i, 0)),
                  pl.BlockSpec((D, D), lambda b, i: (0, 0)),
                  pl.BlockSpec((1, ts, D), lambda b, i: (b, i, 0)),
                  pl.BlockSpec((1, 1, D), lambda b, i: (b, 0, 0)),
                  pl.BlockSpec((1, D), lambda b, i: (0, 0))],
        out_specs=pl.BlockSpec((1, ts, D), lambda b, i: (b, i, 0)),
        out_shape=jax.ShapeDtypeStruct((B, S, D), F32),
        compiler_params=pltpu.CompilerParams(vmem_limit_bytes=VMEM_LIMIT),
        name="out",
    )(merged3, w_out, x, gate.reshape(B, 1, D), g_final.reshape(1, D))


def kernel(x, c, positions, w_ada, b_ada, g_norm, w_in, g_ret, w_ck1, w_ck2, pe_ck,
           w_cv1, w_cv2, pe_cv, w_up_ret, w_up_nsa, w_out, g_final):
    B, S, D = x.shape
    T = B * S
    depth = w_ada.shape[0]
    G = NSA_GROUPS
    RW = RET_HEADS * HD
    NW = NSA_HEADS * HD
    KVW = G * HD
    cols = (("r_q", RW, "rope", 1.0), ("r_k", RW, "rope", HD ** -0.5), ("r_v", RW, "plain", 1.0),
            ("r_g", RW, "silu", 1.0), ("n_q", NW, "rope", HD ** -0.5 * LOG2E),
            ("n_kc", KVW, "f32", 1.0), ("n_vc", KVW, "f32", 1.0),
            ("n_ks", KVW, "rope", 1.0), ("n_vs", KVW, "plain", 1.0),
            ("n_kw", KVW, "rope", 1.0), ("n_vw", KVW, "plain", 1.0), ("n_g", NW, "silu", 1.0),
            ("n_bg", NSA_HEADS * 3, "f32", 1.0), ("m_a", D, "sig", 1.0), ("m_b", D, "sig", 1.0))
    start = {}
    off = 0
    for nm, wd, _, _ in cols:
        start[nm] = off
        off += wd

    runs = [(start["m_a"], 2 * D)]
    for nm, wd, kind, _ in cols:
        if kind in ("f32", "sig"):
            continue
        if runs[-1][0] + runs[-1][1] == start[nm] and len(runs) > 1:
            runs[-1] = (runs[-1][0], runs[-1][1] + wd)
        else:
            runs.append((start[nm], wd))
    tiles, offsets, blk = [], [], {}
    for lo, length in runs:
        assert length % PROJ_TN == 0 and lo % 8 == 0
        for t0 in range(lo, lo + length, PROJ_TN):
            segs = []
            for nm, wd, kind, scale in cols:
                a0, a1 = max(start[nm], t0), min(start[nm] + wd, t0 + PROJ_TN)
                if a0 >= a1:
                    continue
                assert (a0 - t0) % LANES == 0 and (a1 - a0) % LANES == 0
                if a0 == start[nm]:
                    blk[nm] = (len(tiles) * PROJ_TN + a0 - t0) // LANES
                if segs and segs[-1][0] == kind and segs[-1][1] == scale:
                    segs[-1] = (kind, scale, segs[-1][2] + a1 - a0)
                else:
                    segs.append((kind, scale, a1 - a0))
            tiles.append(tuple(segs))
            offsets.append(t0)

    inv = jnp.exp(jnp.arange(0, HD, 2, dtype=F32) * (-math.log(ROPE_THETA) / HD))
    inv2 = jnp.concatenate([inv, inv]).reshape(1, HD)
    cend = jnp.arange(S // CMP_STRIDE) * CMP_STRIDE + (CMP_BLOCK - 1)
    cend = jnp.minimum(cend, S - 1)
    log_gamma = jnp.log1p(-jnp.exp2(-5.0 - jnp.arange(RET_HEADS, dtype=F32)))
    w_in_t = jnp.swapaxes(w_in, 1, 2)

    for l in range(depth):
        wt = w_in_t[l]
        bg = wt[start["n_bg"]:start["n_bg"] + NSA_HEADS * 3].reshape(G, NSA_REP * 3, D)
        bg = jnp.pad(bg, ((0, 0), (0, LANES - NSA_REP * 3), (0, 0))).reshape(G * LANES, D)
        w_f = jnp.concatenate([wt[start["n_kc"]:start["n_kc"] + 2 * KVW], bg], 0)

        mod = _mod(c, w_ada[l], b_ada[l])
        shift, scl, gate = mod[:, :D], mod[:, D:2 * D], mod[:, 2 * D:]
        h3, cos3, sin3, xh, gts = _h(x, g_norm[l], scl, shift, positions, inv2, w_f, n_kv=2 * G)
        h2, cos, sin = h3.reshape(T, D), cos3.reshape(T, HD), sin3.reshape(T, HD)
        cos_c, sin_c = cos3[:, cend], sin3[:, cend]

        P2 = _proj(h2, w_in_t, l, cos, sin, tiles=tiles, offsets=offsets, tm=min(PROJ_TM, T))
        P3 = P2.reshape(B, S, -1)

        hps = RET_HPS
        y_ret = _retention(P3, log_gamma, g_ret[l], q_blk=blk["r_q"] // hps, k_blk=blk["r_k"] // hps,
                           v_blk=blk["r_v"] // hps, rg_blk=blk["r_g"] // hps)

        half = CMP_BLOCK // 2
        pe2 = jnp.stack([pe_ck[l].reshape(2, half * HD), pe_cv[l].reshape(2, half * HD)])
        w1s = jnp.stack([w_ck1[l], w_cv1[l]]).astype(BF16)
        w2s = jnp.stack([w_ck2[l], w_cv2[l]]).astype(BF16)
        kcv = _compress(xh, pe2, w1s, w2s, cos_c, sin_c)

        y_nsa = _nsa(P3, gts, kcv, q_blk4=blk["n_q"] // NSA_REP,
                     ks_blk=blk["n_ks"], vs_blk=blk["n_vs"], kw_blk=blk["n_kw"], vw_blk=blk["n_vw"],
                     ng_blk4=blk["n_g"] // NSA_REP, gt_blk=0)

        merged = _up(y_ret.reshape(T, RW), y_nsa.reshape(T, NW), w_up_ret[l].astype(BF16),
                     w_up_nsa[l].astype(BF16), P2, tm=min(512, T))
        x = _out(merged.reshape(B, S, D), w_out[l].astype(BF16), x, gate, g_final,
                 ts=min(512, S), final_norm=(l + 1 == depth))
    return x
```

```python
import functools
import math

import jax
import jax.numpy as jnp
from jax import lax
from jax.experimental import pallas as pl
from jax.experimental.pallas import tpu as pltpu

F32 = jnp.float32
BF16 = jnp.bfloat16

HD = 128
RET_HEADS = 8
NSA_HEADS = 8
NSA_GROUPS = 2
NSA_REP = NSA_HEADS // NSA_GROUPS
CMP_BLOCK = 32
CMP_STRIDE = 16
CMP_HIDDEN = 256
SEL_BLOCK = 64
SEL_TOPN = 16
WINDOW = 512
FORCE_BONUS = 1.0e4
ROPE_THETA = 10000.0
EPS = 1e-6
NEG = -1.0e30

LANES = 128
PROJ_TN = 1024
PROJ_TM = 2048
PROJ_RC = 256
H_RC = 256
RET_CHUNK = 256
RET_HPS = 4
NSA_TQ = 256
LOG2E = math.log2(math.e)
VMEM_LIMIT = 56 * 1024 * 1024


def _sigmoid(v):
    return 1.0 / (1.0 + jnp.exp(-v))


def _dot(a, b):
    return jnp.dot(a, b, preferred_element_type=F32)


def _dot_nt(a, b):
    return lax.dot_general(a, b, (((1,), (1,)), ((), ())), preferred_element_type=F32)


def _dot_tn(a, b):
    return lax.dot_general(a, b, (((0,), (0,)), ((), ())), preferred_element_type=F32)


def _rope(v, cos, sin_signed):
    return v * cos + pltpu.roll(v, HD // 2, 1) * sin_signed


def _mod_kernel(c_ref, w_ref, b_ref, o_ref):
    c = c_ref[...]
    sc = c * _sigmoid(c)
    o_ref[...] = jnp.dot(sc, w_ref[...], preferred_element_type=F32,
                         precision=lax.Precision.HIGHEST) + b_ref[...]


def _mod(c, w_ada, b_ada):
    B, D = c.shape
    N = w_ada.shape[1]
    tn = math.gcd(N, 512)
    return pl.pallas_call(
        _mod_kernel,
        grid=(N // tn,),
        in_specs=[pl.BlockSpec((B, D), lambda j: (0, 0)),
                  pl.BlockSpec((D, tn), lambda j: (0, j)),
                  pl.BlockSpec((1, tn), lambda j: (0, j))],
        out_specs=pl.BlockSpec((B, tn), lambda j: (0, j)),
        out_shape=jax.ShapeDtypeStruct((B, N), F32),
        name="mod",
    )(c, w_ada, b_ada.reshape(1, N))


def _h_kernel(x_ref, g_ref, scl_ref, sh_ref, pos_ref, inv_ref, wf_ref,
              h_ref, cos_ref, sin_ref, kv_ref, gt_ref, wfb_ref):
    @pl.when((pl.program_id(0) == 0) & (pl.program_id(1) == 0))
    def _():
        wfb_ref[...] = wf_ref[...].astype(BF16)

    n_kv = kv_ref.shape[0]
    ts = x_ref.shape[1]
    rc = min(H_RC, ts)
    gmul = g_ref[...] * (1.0 + scl_ref[0])
    for r in range(ts // rc):
        rows = slice(r * rc, (r + 1) * rc)
        x = x_ref[0, rows, :]
        ms = jnp.mean(x * x, -1, keepdims=True)
        hb = (x * lax.rsqrt(ms + EPS) * gmul + sh_ref[0]).astype(BF16)
        h_ref[0, rows, :] = hb
        acc = _dot_nt(hb, wfb_ref[...])
        for c in range(n_kv):
            kv_ref[c, 0, rows, :] = acc[:, c * HD:(c + 1) * HD]
        gt_ref[0, rows, :] = acc[:, n_kv * HD:]
    ang = pos_ref[0].astype(F32) * inv_ref[...]
    lane = lax.broadcasted_iota(jnp.int32, (1, HD), 1)
    cos_ref[0] = jnp.cos(ang)
    sin_ref[0] = jnp.where(lane < HD // 2, -1.0, 1.0) * jnp.sin(ang)


def _h(x, g_norm, scl, shift, positions, inv2, w_f_t, *, n_kv):
    B, S, D = x.shape
    N = w_f_t.shape[0]
    ts = 512
    return pl.pallas_call(
        _h_kernel,
        grid=(B, S // ts),
        in_specs=[pl.BlockSpec((1, ts, D), lambda b, i: (b, i, 0)),
                  pl.BlockSpec((1, D), lambda b, i: (0, 0)),
                  pl.BlockSpec((1, 1, D), lambda b, i: (b, 0, 0)),
                  pl.BlockSpec((1, 1, D), lambda b, i: (b, 0, 0)),
                  pl.BlockSpec((1, ts, 1), lambda b, i: (b, i, 0)),
                  pl.BlockSpec((1, HD), lambda b, i: (0, 0)),
                  pl.BlockSpec((N, D), lambda b, i: (0, 0))],
        out_specs=[pl.BlockSpec((1, ts, D), lambda b, i: (b, i, 0)),
                   pl.BlockSpec((1, ts, HD), lambda b, i: (b, i, 0)),
                   pl.BlockSpec((1, ts, HD), lambda b, i: (b, i, 0)),
                   pl.BlockSpec((n_kv, 1, ts, HD), lambda b, i: (0, b, i, 0)),
                   pl.BlockSpec((1, ts, N - n_kv * HD), lambda b, i: (b, i, 0))],
        out_shape=[jax.ShapeDtypeStruct((B, S, D), BF16),
                   jax.ShapeDtypeStruct((B, S, HD), F32),
                   jax.ShapeDtypeStruct((B, S, HD), F32),
                   jax.ShapeDtypeStruct((n_kv, B, S, HD), F32),
                   jax.ShapeDtypeStruct((B, S, N - n_kv * HD), F32)],
        scratch_shapes=[pltpu.VMEM((N, D), BF16)],
        compiler_params=pltpu.CompilerParams(
            dimension_semantics=("arbitrary", "arbitrary"), vmem_limit_bytes=VMEM_LIMIT),
        name="h",
    )(x, g_norm.reshape(1, D), scl.reshape(B, 1, D), shift.reshape(B, 1, D),
      positions.reshape(B, S, 1), inv2, w_f_t)


def _tile_cond(j, idxs):
    cond = None
    run_lo = prev = None
    for t in list(idxs) + [None]:
        if run_lo is None:
            run_lo = prev = t
            continue
        if t is not None and t == prev + 1:
            prev = t
            continue
        c = (j >= run_lo) & (j <= prev)
        cond = c if cond is None else (cond | c)
        run_lo = prev = t
    return cond


def _proj_kernel(h_ref, w_ref, cos_ref, sin_ref, o_ref, *, tiles):
    j = pl.program_id(1)
    tm = h_ref.shape[0]
    rc = min(PROJ_RC, tm)

    def row_chunks(w, epilogue):
        for c in range(tm // rc):
            rows = slice(c * rc, (c + 1) * rc)
            epilogue(rows, _dot_nt(h_ref[rows, :], w))

    shapes = {}
    for t, segs in enumerate(tiles):
        shapes.setdefault(tuple((k, wd) for k, _, wd in segs), []).append(t)

    for shape, idxs in shapes.items():
        @pl.when(_tile_cond(j, idxs))
        def _(shape=shape, idxs=idxs):
            scale = []
            for si in range(len(shape)):
                f = jnp.float32(1.0)
                for sc in sorted({tiles[t][si][1] for t in idxs} - {1.0}):
                    f = jnp.where(_tile_cond(j, [t for t in idxs if tiles[t][si][1] == sc]), jnp.float32(sc), f)
                scale.append(f)

            def epi(rows, acc):
                lo = 0
                for si, (kind, width) in enumerate(shape):
                    cols = slice(lo, lo + width)
                    if kind == "sig":
                        o_ref[rows, cols] = _sigmoid(acc[:, cols]).astype(o_ref.dtype)
                    elif kind == "silu":
                        o_ref[rows, cols] = (acc[:, cols] * _sigmoid(acc[:, cols])).astype(o_ref.dtype)
                    elif kind == "plain":
                        o_ref[rows, cols] = acc[:, cols].astype(o_ref.dtype)
                    else:
                        cos = cos_ref[rows, :] * scale[si]
                        sin = sin_ref[rows, :] * scale[si]
                        for hh in range(lo // HD, (lo + width) // HD):
                            sl = slice(hh * HD, (hh + 1) * HD)
                            o_ref[rows, sl] = _rope(acc[:, sl], cos, sin).astype(o_ref.dtype)
                    lo += width
            row_chunks(w_ref[0].astype(BF16), epi)


def _proj(h2, w_in_t, layer, cos, sin, *, tiles, offsets, tm):
    T, D = h2.shape
    n_tiles = len(tiles)

    def row_off(j):
        off = jnp.int32(offsets[0])
        for t in range(1, n_tiles):
            off = jnp.where(j == t, offsets[t], off)
        return pl.multiple_of(off, 8)

    kern = functools.partial(_proj_kernel, tiles=tuple(tiles))
    return pl.pallas_call(
        kern,
        grid=(T // tm, n_tiles),
        in_specs=[pl.BlockSpec((tm, D), lambda i, j: (i, 0)),
                  pl.BlockSpec((pl.Element(1), pl.Element(PROJ_TN), pl.Element(D)),
                               lambda i, j: (layer, row_off(j), 0)),
                  pl.BlockSpec((tm, HD), lambda i, j: (i, 0)),
                  pl.BlockSpec((tm, HD), lambda i, j: (i, 0))],
        out_specs=pl.BlockSpec((tm, PROJ_TN), lambda i, j: (i, j)),
        out_shape=jax.ShapeDtypeStruct((T, n_tiles * PROJ_TN), BF16),
        compiler_params=pltpu.CompilerParams(
            dimension_semantics=("arbitrary", "arbitrary"), vmem_limit_bytes=VMEM_LIMIT),
        name="proj",
    )(h2, w_in_t, cos, sin)


def _ret_kernel(lg_ref, q_ref, k_ref, v_ref, rg_ref, g_ref, o_ref):
    S = q_ref.shape[1]
    C = RET_CHUNK
    r = lax.broadcasted_iota(jnp.int32, (C, C), 0)
    c = lax.broadcasted_iota(jnp.int32, (C, C), 1)
    diff = (r - c).astype(F32)
    rowf = lax.broadcasted_iota(jnp.int32, (C, HD), 0).astype(F32)
    for hh in range(RET_HPS):
        cols = slice(hh * HD, (hh + 1) * HD)
        lg = lg_ref[pl.program_id(1) * RET_HPS + hh]
        dmask = jnp.where(diff >= 0, jnp.exp(lg * jnp.maximum(diff, 0.0)), 0.0)
        zeta = jnp.exp(lg * (C - 1 - rowf))
        xi = jnp.exp(lg * (rowf + 1.0))
        decay = jnp.exp(jnp.full((HD, HD), lg * C, F32))
        g = g_ref[:, cols]
        state = jnp.zeros((HD, HD), F32)
        for n in range(S // C):
            sl = slice(n * C, (n + 1) * C)
            q = q_ref[0, sl, cols]
            k = k_ref[0, sl, cols]
            v = v_ref[0, sl, cols]
            sc = _dot_nt(q, k) * dmask
            o = _dot(sc.astype(BF16), v)
            if n > 0:
                o = o + _dot(q, state.astype(BF16)) * xi
            if n + 1 < S // C:
                kz = (k.astype(F32) * zeta).astype(BF16)
                state = state * decay + _dot_tn(kz, v)
            mu = jnp.mean(o, -1, keepdims=True)
            d = o - mu
            var = jnp.mean(d * d, -1, keepdims=True)
            on = d * lax.rsqrt(var + EPS)
            o_ref[0, sl, cols] = (on * g * rg_ref[0, sl, cols].astype(F32)).astype(o_ref.dtype)


def _retention(P3, log_gamma, g_ret, *, q_blk, k_blk, v_blk, rg_blk):
    B, S, _ = P3.shape
    H = RET_HEADS
    W = RET_HPS * HD

    def col(off):
        return lambda b, h, lg: (b, 0, off + h)

    return pl.pallas_call(
        _ret_kernel,
        grid_spec=pltpu.PrefetchScalarGridSpec(
            num_scalar_prefetch=1,
            grid=(B, H // RET_HPS),
            in_specs=[pl.BlockSpec((1, S, W), col(q_blk)),
                      pl.BlockSpec((1, S, W), col(k_blk)),
                      pl.BlockSpec((1, S, W), col(v_blk)),
                      pl.BlockSpec((1, S, W), col(rg_blk)),
                      pl.BlockSpec((1, W), lambda b, h, lg: (0, h))],
            out_specs=pl.BlockSpec((1, S, W), lambda b, h, lg: (b, 0, h)),
        ),
        out_shape=jax.ShapeDtypeStruct((B, S, H * HD), BF16),
        compiler_params=pltpu.CompilerParams(vmem_limit_bytes=VMEM_LIMIT),
        name="ret",
    )(log_gamma, P3, P3, P3, P3, g_ret.reshape(1, H * HD))


def _cmp_kernel(x_ref, pe_ref, w1_ref, w2_ref, cos_ref, sin_ref, o_ref):
    half = (CMP_BLOCK // 2) * HD
    M = x_ref.shape[2] // CMP_STRIDE
    x = jnp.concatenate([x_ref[0, 0, pl.ds(p, M, stride=CMP_STRIDE), :] for p in range(CMP_STRIDE)], 1)
    xa = (x + pe_ref[0, 0:1, :]).astype(BF16)
    xb = (x + pe_ref[0, 1:2, :]).astype(BF16)
    a = _dot(xa, w1_ref[0, :half, :])
    b = _dot(xb, w1_ref[0, half:, :])
    nrow = a.shape[0]
    pre = a + pltpu.roll(b, nrow - 1, 0)
    hid = pre * _sigmoid(pre)
    out = _dot(hid.astype(BF16), w2_ref[0])
    roped = _rope(out, cos_ref[0], sin_ref[0])
    is_key = pl.program_id(0) < NSA_GROUPS
    o_ref[0, 0] = jnp.where(is_key, roped, out).astype(o_ref.dtype)


def _compress(xh, pe2, w1s, w2s, cos_c, sin_c):
    NC, B, S, _ = xh.shape
    G = NSA_GROUPS
    M = S // CMP_STRIDE
    W = CMP_STRIDE * HD
    return pl.pallas_call(
        _cmp_kernel,
        grid=(NC, B),
        in_specs=[pl.BlockSpec((1, 1, S, HD), lambda c, b: (c, b, 0, 0)),
                  pl.BlockSpec((1, 2, W), lambda c, b: (c // G, 0, 0)),
                  pl.BlockSpec((1, 2 * W, CMP_HIDDEN), lambda c, b: (c // G, 0, 0)),
                  pl.BlockSpec((1, CMP_HIDDEN, HD), lambda c, b: (c // G, 0, 0)),
                  pl.BlockSpec((1, M, HD), lambda c, b: (b, 0, 0)),
                  pl.BlockSpec((1, M, HD), lambda c, b: (b, 0, 0))],
        out_specs=pl.BlockSpec((1, 1, M, HD), lambda c, b: (b, c, 0, 0)),
        out_shape=jax.ShapeDtypeStruct((B, NC, M, HD), BF16),
        name="cmp",
    )(xh, pe2, w1s, w2s, cos_c, sin_c)


def _nsa_kernel(q_ref, ks_ref, vs_ref, kw_ref, vw_ref, kc_ref, vc_ref, gt_ref, ng_ref, o_ref,
                qa_ref, ow_ref, acc_ref, oc_ref, gs_ref, kaug_ref, vsaug_ref, vwaug_ref, wb_ref):
    TQ = q_ref.shape[1]
    S = ks_ref.shape[1]
    R = NSA_REP
    RT = R * TQ
    WSPAN = WINDOW + TQ
    NSEL = S // SEL_BLOCK
    qi = pl.program_id(2)
    t0 = qi * TQ

    @pl.when(qi == 0)
    def _():
        blk = lax.broadcasted_iota(jnp.int32, (S, HD), 0) >> 6
        onehot = jnp.where(blk == lax.broadcasted_iota(jnp.int32, (S, HD), 1), 1.0, 0.0)
        kaug_ref[:, 0:HD] = ks_ref[0]
        kaug_ref[:, HD:2 * HD] = onehot.astype(BF16)
        ones = jnp.ones((S, HD), BF16)
        vsaug_ref[:, 0:HD] = vs_ref[0]
        vsaug_ref[:, HD:2 * HD] = ones
        vwaug_ref[:, 0:HD] = vw_ref[0]
        vwaug_ref[:, HD:2 * HD] = ones
        r = lax.broadcasted_iota(jnp.int32, (TQ, WSPAN), 0)
        c = lax.broadcasted_iota(jnp.int32, (TQ, WSPAN), 1)
        wb_ref[0] = jnp.where(c <= r, 0.0, NEG)
        wb_ref[1] = jnp.where(c <= r + TQ, 0.0, NEG)
        wb_ref[2] = jnp.where((c > r) & (c <= r + WINDOW), 0.0, NEG)

    for h in range(R):
        qa_ref[h * TQ:(h + 1) * TQ, 0:HD] = q_ref[0, :, h * HD:(h + 1) * HD]
    q4 = qa_ref[:, 0:HD]

    offw = pl.multiple_of(jnp.maximum(t0 - WINDOW, 0), TQ)
    kwin = kw_ref[0, pl.ds(offw, WSPAN), :]
    vwin = vwaug_ref[pl.ds(offw, WSPAN), :]
    wbias = wb_ref[jnp.minimum(qi, 2)]

    def win_scores(h):
        return _dot_nt(q_ref[0, :, h * HD:(h + 1) * HD], kwin) + wbias

    gt = _sigmoid(gt_ref[0])

    def gate(h, branch):
        c = 3 * h + branch
        return jnp.broadcast_to(gt[:, c:c + 1], (TQ, HD))

    def win_finish(h, sw):
        pw = jnp.exp2(sw - jnp.max(sw, -1, keepdims=True))
        ow = _dot(pw.astype(BF16), vwin)
        ow_ref[h * TQ:(h + 1) * TQ, :] = gate(h, 2) * (ow[:, 0:HD] / ow[:, HD:2 * HD])

    for h in range(R):
        gs_ref[h * TQ:(h + 1) * TQ, :] = gate(h, 1)
    sw0 = win_scores(0)

    s = _dot_nt(q4, kc_ref[0, 0])
    trow = t0 + (lax.broadcasted_iota(jnp.int32, (RT, HD), 0) & (TQ - 1))
    ncol = lax.broadcasted_iota(jnp.int32, (RT, HD), 1)
    s = jnp.where(ncol * CMP_STRIDE + (CMP_BLOCK - 1) <= trow, s, NEG)
    mx = jnp.max(s, -1, keepdims=True)
    p = jnp.exp2(s - mx)
    tcol = t0 + (lax.broadcasted_iota(jnp.int32, (RT, 1), 0) & (TQ - 1))
    any_valid = jnp.where(tcol >= CMP_BLOCK - 1, 1.0, 0.0)
    p = p * (any_valid / jnp.sum(p, -1, keepdims=True))
    sw1 = win_scores(1)
    win_finish(0, sw0)
    o_cmp = _dot(p.astype(BF16), vc_ref[0, 0])
    for h in range(R):
        oc_ref[h * TQ:(h + 1) * TQ, :] = gate(h, 0) * o_cmp[h * TQ:(h + 1) * TQ]

    psum = p[0:TQ]
    for h in range(1, R):
        psum = psum + p[h * TQ:(h + 1) * TQ]
    p_hi = psum.astype(BF16)
    rem = psum - p_hi.astype(F32)
    p_mid = rem.astype(BF16)
    p_lo = (rem - p_mid.astype(F32)).astype(BF16)
    jj = lax.broadcasted_iota(jnp.int32, (NSEL, HD), 0)
    nn = lax.broadcasted_iota(jnp.int32, (NSEL, HD), 1)
    ov = jnp.maximum(jnp.minimum(nn * CMP_STRIDE + CMP_BLOCK, (jj + 1) * SEL_BLOCK)
                     - jnp.maximum(nn * CMP_STRIDE, jj * SEL_BLOCK), 0).astype(F32) * (1.0 / CMP_BLOCK)
    ov = ov.astype(BF16)
    imp = (_dot_nt(ov, p_lo) + _dot_nt(ov, p_mid)) + _dot_nt(ov, p_hi)
    sw2 = win_scores(2)
    win_finish(1, sw1)

    j = lax.broadcasted_iota(jnp.int32, (NSEL, TQ), 0)
    tb = (t0 + lax.broadcasted_iota(jnp.int32, (NSEL, TQ), 1)) >> 6
    forced = (j == 0) | (j == tb) | (j == tb - 1)
    causal = j <= tb
    sc = jnp.where(causal, imp + jnp.where(forced, FORCE_BONUS, 0.0), NEG)
    rank = jnp.zeros((NSEL, TQ), F32)
    for i in range(NSEL):
        si = sc[i:i + 1, :]
        rank = rank + jnp.where(j > i, jnp.where(si >= sc, 1.0, 0.0), jnp.where(si > sc, 1.0, 0.0))
    sel = jnp.where((rank < SEL_TOPN) & causal, 1.0, 0.0)
    sw3 = win_scores(3)
    win_finish(2, sw2)

    selpad = jnp.concatenate([sel, jnp.zeros((HD - NSEL, TQ), F32)], 0).astype(BF16)
    eye = jnp.where(lax.broadcasted_iota(jnp.int32, (TQ, TQ), 0)
                    == lax.broadcasted_iota(jnp.int32, (TQ, TQ), 1), 1.0, 0.0).astype(BF16)
    sel_t = _dot_nt(eye, selpad)
    lane = lax.broadcasted_iota(jnp.int32, (TQ, HD), 1)
    bias = jnp.where((lane < NSEL) & (sel_t < 0.5), NEG, 0.0).astype(BF16)
    for h in range(R):
        qa_ref[h * TQ:(h + 1) * TQ, HD:2 * HD] = bias
    win_finish(3, sw3)

    for v in range(S // TQ):
        @pl.when(qi == v)
        def _(v=v):
            past = v * TQ
            for h in range(R):
                rows = slice(h * TQ, (h + 1) * TQ)
                qh = qa_ref[rows, :]
                sd = _dot_nt(qh, kaug_ref[past:past + TQ, :]) + wb_ref[0, :, 0:TQ]
                mx = jnp.max(sd, -1, keepdims=True)
                if past:
                    sp = _dot_nt(qh, kaug_ref[0:past, :])
                    mx = jnp.maximum(mx, jnp.max(sp, -1, keepdims=True))
                pv = _dot(jnp.exp2(sd - mx).astype(BF16), vsaug_ref[past:past + TQ, :])
                if past:
                    pv = pv + _dot(jnp.exp2(sp - mx).astype(BF16), vsaug_ref[0:past, :])
                acc_ref[rows, :] = pv

    for h in range(R):
        rows = slice(h * TQ, (h + 1) * TQ)
        o = (oc_ref[rows, :] + gs_ref[rows, :] * (acc_ref[rows, 0:HD] / acc_ref[rows, HD:2 * HD])
             + ow_ref[rows, :])
        o_ref[0, :, h * HD:(h + 1) * HD] = (o * ng_ref[0, :, h * HD:(h + 1) * HD].astype(F32)).astype(o_ref.dtype)


def _nsa(P3, F3, kcv, *, q_blk4, ks_blk, vs_blk, kw_blk, vw_blk, ng_blk4, gt_blk):
    B, S, _ = P3.shape
    G, R, TQ = NSA_GROUPS, NSA_REP, NSA_TQ
    assert WINDOW == 2 * TQ and SEL_BLOCK == 64 and S % (2 * TQ) == 0 and S >= WINDOW + TQ
    M = kcv.shape[2]
    kv_spec = lambda blk: pl.BlockSpec((1, S, HD), lambda b, g, i: (b, 0, blk + g))
    return pl.pallas_call(
        _nsa_kernel,
        grid=(B, G, S // TQ),
        in_specs=[pl.BlockSpec((1, TQ, R * HD), lambda b, g, i: (b, i, q_blk4 + g)),
                  kv_spec(ks_blk), kv_spec(vs_blk), kv_spec(kw_blk), kv_spec(vw_blk),
                  pl.BlockSpec((1, 1, M, HD), lambda b, g, i: (b, g, 0, 0)),
                  pl.BlockSpec((1, 1, M, HD), lambda b, g, i: (b, G + g, 0, 0)),
                  pl.BlockSpec((1, TQ, LANES), lambda b, g, i: (b, i, gt_blk + g)),
                  pl.BlockSpec((1, TQ, R * HD), lambda b, g, i: (b, i, ng_blk4 + g))],
        out_specs=pl.BlockSpec((1, TQ, R * HD), lambda b, g, i: (b, i, g)),
        out_shape=jax.ShapeDtypeStruct((B, S, NSA_HEADS * HD), BF16),
        scratch_shapes=[pltpu.VMEM((R * TQ, 2 * HD), BF16),
                        pltpu.VMEM((R * TQ, HD), F32),
                        pltpu.VMEM((R * TQ, 2 * HD), F32),
                        pltpu.VMEM((R * TQ, HD), F32),
                        pltpu.VMEM((R * TQ, HD), F32),
                        pltpu.VMEM((S, 2 * HD), BF16),
                        pltpu.VMEM((S, 2 * HD), BF16),
                        pltpu.VMEM((S, 2 * HD), BF16),
                        pltpu.VMEM((3, TQ, WINDOW + TQ), F32)],
        compiler_params=pltpu.CompilerParams(
            dimension_semantics=("arbitrary", "arbitrary", "arbitrary"), vmem_limit_bytes=VMEM_LIMIT),
        name="nsa",
    )(P3, P3, P3, P3, P3, kcv, kcv, F3, P3)


def _up_kernel(yr_ref, yn_ref, wr_ref, wn_ref, sa_ref, sb_ref, o_ref):
    a = _dot(yr_ref[...], wr_ref[...])
    b = _dot(yn_ref[...], wn_ref[...])
    o_ref[...] = (sa_ref[...].astype(F32) * a + sb_ref[...].astype(F32) * b).astype(o_ref.dtype)


def _up(y_ret, y_nsa, w_ur, w_un, P2, *, tm):
    T, W = y_ret.shape
    D = w_ur.shape[1]
    return pl.pallas_call(
        _up_kernel,
        grid=(T // tm,),
        in_specs=[pl.BlockSpec((tm, W), lambda i: (i, 0)),
                  pl.BlockSpec((tm, W), lambda i: (i, 0)),
                  pl.BlockSpec((W, D), lambda i: (0, 0)),
                  pl.BlockSpec((W, D), lambda i: (0, 0)),
                  pl.BlockSpec((tm, D), lambda i: (i, 0)),
                  pl.BlockSpec((tm, D), lambda i: (i, 1))],
        out_specs=pl.BlockSpec((tm, D), lambda i: (i, 0)),
        out_shape=jax.ShapeDtypeStruct((T, D), BF16),
        compiler_params=pltpu.CompilerParams(vmem_limit_bytes=VMEM_LIMIT),
        name="up",
    )(y_ret, y_nsa, w_ur, w_un, P2, P2)


def _out_kernel(m_ref, w_ref, x_ref, gate_ref, g_ref, o_ref, *, final_norm):
    y = x_ref[0] + gate_ref[0] * _dot(m_ref[0], w_ref[...])
    if final_norm:
        ms = jnp.mean(y * y, -1, keepdims=True)
        y = y * lax.rsqrt(ms + EPS) * g_ref[...]
    o_ref[0] = y


def _out(merged3, w_out, x, gate, g_final, *, ts, final_norm):
    B, S, D = x.shape
    return pl.pallas_call(
        functools.partial(_out_kernel, final_norm=final_norm),
        grid=(B, S // ts),
        in_specs=[pl.BlockSpec((1, ts, D), lambda b, i: (b, i, 0)),
                  pl.BlockSpec((D, D), lambda b, i: (0, 0)),
                  pl.BlockSpec((1, ts, D), lambda b, i: (b, i, 0)),
                  pl.BlockSpec((1, 1, D), lambda b, i: (b, 0, 0)),
                  pl.BlockSpec((1, D), lambda b, i: (0, 0))],
        out_specs=pl.BlockSpec((1, ts, D), lambda b, i: (b, i, 0)),
        out_shape=jax.ShapeDtypeStruct((B, S, D), F32),
        compiler_params=pltpu.CompilerParams(vmem_limit_bytes=VMEM_LIMIT),
        name="out",
    )(merged3, w_out, x, gate.reshape(B, 1, D), g_final.reshape(1, D))


def kernel(x, c, positions, w_ada, b_ada, g_norm, w_in, g_ret, w_ck1, w_ck2, pe_ck,
           w_cv1, w_cv2, pe_cv, w_up_ret, w_up_nsa, w_out, g_final):
    B, S, D = x.shape
    T = B * S
    depth = w_ada.shape[0]
    G = NSA_GROUPS
    RW = RET_HEADS * HD
    NW = NSA_HEADS * HD
    KVW = G * HD
    cols = (("r_q", RW, "rope", 1.0), ("r_k", RW, "rope", HD ** -0.5), ("r_v", RW, "plain", 1.0),
            ("r_g", RW, "silu", 1.0), ("n_q", NW, "rope", HD ** -0.5 * LOG2E),
            ("n_kc", KVW, "f32", 1.0), ("n_vc", KVW, "f32", 1.0),
            ("n_ks", KVW, "rope", 1.0), ("n_vs", KVW, "plain", 1.0),
            ("n_kw", KVW, "rope", 1.0), ("n_vw", KVW, "plain", 1.0), ("n_g", NW, "silu", 1.0),
            ("n_bg", NSA_HEADS * 3, "f32", 1.0), ("m_a", D, "sig", 1.0), ("m_b", D, "sig", 1.0))
    start = {}
    off = 0
    for nm, wd, _, _ in cols:
        start[nm] = off
        off += wd

    runs = [(start["m_a"], 2 * D)]
    for nm, wd, kind, _ in cols:
        if kind in ("f32", "sig"):
            continue
        if runs[-1][0] + runs[-1][1] == start[nm] and len(runs) > 1:
            runs[-1] = (runs[-1][0], runs[-1][1] + wd)
        else:
            runs.append((start[nm], wd))
    tiles, offsets, blk = [], [], {}
    for lo, length in runs:
        assert length % PROJ_TN == 0 and lo % 8 == 0
        for t0 in range(lo, lo + length, PROJ_TN):
            segs = []
            for nm, wd, kind, scale in cols:
                a0, a1 = max(start[nm], t0), min(start[nm] + wd, t0 + PROJ_TN)
                if a0 >= a1:
                    continue
                assert (a0 - t0) % LANES == 0 and (a1 - a0) % LANES == 0
                if a0 == start[nm]:
                    blk[nm] = (len(tiles) * PROJ_TN + a0 - t0) // LANES
                if segs and segs[-1][0] == kind and segs[-1][1] == scale:
                    segs[-1] = (kind, scale, segs[-1][2] + a1 - a0)
                else:
                    segs.append((kind, scale, a1 - a0))
            tiles.append(tuple(segs))
            offsets.append(t0)

    inv = jnp.exp(jnp.arange(0, HD, 2, dtype=F32) * (-math.log(ROPE_THETA) / HD))
    inv2 = jnp.concatenate([inv, inv]).reshape(1, HD)
    cend = jnp.arange(S // CMP_STRIDE) * CMP_STRIDE + (CMP_BLOCK - 1)
    cend = jnp.minimum(cend, S - 1)
    log_gamma = jnp.log1p(-jnp.exp2(-5.0 - jnp.arange(RET_HEADS, dtype=F32)))
    w_in_t = jnp.swapaxes(w_in, 1, 2)

    for l in range(depth):
        wt = w_in_t[l]
        bg = wt[start["n_bg"]:start["n_bg"] + NSA_HEADS * 3].reshape(G, NSA_REP * 3, D)
        bg = jnp.pad(bg, ((0, 0), (0, LANES - NSA_REP * 3), (0, 0))).reshape(G * LANES, D)
        w_f = jnp.concatenate([wt[start["n_kc"]:start["n_kc"] + 2 * KVW], bg], 0)

        mod = _mod(c, w_ada[l], b_ada[l])
        shift, scl, gate = mod[:, :D], mod[:, D:2 * D], mod[:, 2 * D:]
        h3, cos3, sin3, xh, gts = _h(x, g_norm[l], scl, shift, positions, inv2, w_f, n_kv=2 * G)
        h2, cos, sin = h3.reshape(T, D), cos3.reshape(T, HD), sin3.reshape(T, HD)
        cos_c, sin_c = cos3[:, cend], sin3[:, cend]

        P2 = _proj(h2, w_in_t, l, cos, sin, tiles=tiles, offsets=offsets, tm=min(PROJ_TM, T))
        P3 = P2.reshape(B, S, -1)

        hps = RET_HPS
        y_ret = _retention(P3, log_gamma, g_ret[l], q_blk=blk["r_q"] // hps, k_blk=blk["r_k"] // hps,
                           v_blk=blk["r_v"] // hps, rg_blk=blk["r_g"] // hps)

        half = CMP_BLOCK // 2
        pe2 = jnp.stack([pe_ck[l].reshape(2, half * HD), pe_cv[l].reshape(2, half * HD)])
        w1s = jnp.stack([w_ck1[l], w_cv1[l]]).astype(BF16)
        w2s = jnp.stack([w_ck2[l], w_cv2[l]]).astype(BF16)
        kcv = _compress(xh, pe2, w1s, w2s, cos_c, sin_c)

        y_nsa = _nsa(P3, gts, kcv, q_blk4=blk["n_q"] // NSA_REP,
                     ks_blk=blk["n_ks"], vs_blk=blk["n_vs"], kw_blk=blk["n_kw"], vw_blk=blk["n_vw"],
                     ng_blk4=blk["n_g"] // NSA_REP, gt_blk=0)

        merged = _up(y_ret.reshape(T, RW), y_nsa.reshape(T, NW), w_up_ret[l].astype(BF16),
                     w_up_nsa[l].astype(BF16), P2, tm=min(512, T))
        x = _out(merged.reshape(B, S, D), w_out[l].astype(BF16), x, gate, g_final,
                 ts=min(512, S), final_norm=(l + 1 == depth))
    return x
```

```python
import functools
import math

import jax
import jax.numpy as jnp
from jax import lax
from jax.experimental import pallas as pl
from jax.experimental.pallas import tpu as pltpu

F32 = jnp.float32
BF16 = jnp.bfloat16

HD = 128
RET_HEADS = 8
NSA_HEADS = 8
NSA_GROUPS = 2
NSA_REP = NSA_HEADS // NSA_GROUPS
CMP_BLOCK = 32
CMP_STRIDE = 16
CMP_HIDDEN = 256
SEL_BLOCK = 64
SEL_TOPN = 16
WINDOW = 512
FORCE_BONUS = 1.0e4
ROPE_THETA = 10000.0
EPS = 1e-6
NEG = -1.0e30

LANES = 128
PROJ_TN = 1024
PROJ_TM = 2048
PROJ_RC = 1024
H_RC = 256
RET_CHUNK = 256
RET_HPS = 4
NSA_TQ = 512
LOG2E = math.log2(math.e)
VMEM_LIMIT = 56 * 1024 * 1024


def _sigmoid(v):
    return 1.0 / (1.0 + jnp.exp(-v))


def _dot(a, b):
    return jnp.dot(a, b, preferred_element_type=F32)


def _dot_nt(a, b):
    return lax.dot_general(a, b, (((1,), (1,)), ((), ())), preferred_element_type=F32)


def _dot_tn(a, b):
    return lax.dot_general(a, b, (((0,), (0,)), ((), ())), preferred_element_type=F32)


def _rope(v, cos, sin_signed):
    return v * cos + pltpu.roll(v, HD // 2, 1) * sin_signed


def _mod_kernel(c_ref, w_ref, b_ref, o_ref):
    c = c_ref[...]
    sc = c * _sigmoid(c)
    o_ref[...] = jnp.dot(sc, w_ref[...], preferred_element_type=F32,
                         precision=lax.Precision.HIGHEST) + b_ref[...]


def _mod(c, w_ada, b_ada):
    B, D = c.shape
    N = w_ada.shape[1]
    tn = math.gcd(N, 512)
    return pl.pallas_call(
        _mod_kernel,
        grid=(N // tn,),
        in_specs=[pl.BlockSpec((B, D), lambda j: (0, 0)),
                  pl.BlockSpec((D, tn), lambda j: (0, j)),
                  pl.BlockSpec((1, tn), lambda j: (0, j))],
        out_specs=pl.BlockSpec((B, tn), lambda j: (0, j)),
        out_shape=jax.ShapeDtypeStruct((B, N), F32),
        name="mod",
    )(c, w_ada, b_ada.reshape(1, N))


def _h_kernel(x_ref, g_ref, scl_ref, sh_ref, pos_ref, inv_ref, wf_ref,
              h_ref, cos_ref, sin_ref, kv_ref, gt_ref, wfb_ref):
    @pl.when((pl.program_id(0) == 0) & (pl.program_id(1) == 0))
    def _():
        wfb_ref[...] = wf_ref[...].astype(BF16)

    n_kv = kv_ref.shape[0]
    ts = x_ref.shape[1]
    rc = min(H_RC, ts)
    gmul = g_ref[...] * (1.0 + scl_ref[0])
    for r in range(ts // rc):
        rows = slice(r * rc, (r + 1) * rc)
        x = x_ref[0, rows, :]
        ms = jnp.mean(x * x, -1, keepdims=True)
        hb = (x * lax.rsqrt(ms + EPS) * gmul + sh_ref[0]).astype(BF16)
        h_ref[0, rows, :] = hb
        acc = _dot_nt(hb, wfb_ref[...])
        for c in range(n_kv):
            kv_ref[c, 0, rows, :] = acc[:, c * HD:(c + 1) * HD]
        gt_ref[0, rows, :] = acc[:, n_kv * HD:]
    ang = pos_ref[0].astype(F32) * inv_ref[...]
    lane = lax.broadcasted_iota(jnp.int32, (1, HD), 1)
    cos_ref[0] = jnp.cos(ang)
    sin_ref[0] = jnp.where(lane < HD // 2, -1.0, 1.0) * jnp.sin(ang)


def _h(x, g_norm, scl, shift, positions, inv2, w_f_t, *, n_kv):
    B, S, D = x.shape
    N = w_f_t.shape[0]
    ts = 512
    return pl.pallas_call(
        _h_kernel,
        grid=(B, S // ts),
        in_specs=[pl.BlockSpec((1, ts, D), lambda b, i: (b, i, 0)),
                  pl.BlockSpec((1, D), lambda b, i: (0, 0)),
                  pl.BlockSpec((1, 1, D), lambda b, i: (b, 0, 0)),
                  pl.BlockSpec((1, 1, D), lambda b, i: (b, 0, 0)),
                  pl.BlockSpec((1, ts, 1), lambda b, i: (b, i, 0)),
                  pl.BlockSpec((1, HD), lambda b, i: (0, 0)),
                  pl.BlockSpec((N, D), lambda b, i: (0, 0))],
        out_specs=[pl.BlockSpec((1, ts, D), lambda b, i: (b, i, 0)),
                   pl.BlockSpec((1, ts, HD), lambda b, i: (b, i, 0)),
                   pl.BlockSpec((1, ts, HD), lambda b, i: (b, i, 0)),
                   pl.BlockSpec((n_kv, 1, ts, HD), lambda b, i: (0, b, i, 0)),
                   pl.BlockSpec((1, ts, N - n_kv * HD), lambda b, i: (b, i, 0))],
        out_shape=[jax.ShapeDtypeStruct((B, S, D), BF16),
                   jax.ShapeDtypeStruct((B, S, HD), F32),
                   jax.ShapeDtypeStruct((B, S, HD), F32),
                   jax.ShapeDtypeStruct((n_kv, B, S, HD), F32),
                   jax.ShapeDtypeStruct((B, S, N - n_kv * HD), F32)],
        scratch_shapes=[pltpu.VMEM((N, D), BF16)],
        compiler_params=pltpu.CompilerParams(
            dimension_semantics=("arbitrary", "arbitrary"), vmem_limit_bytes=VMEM_LIMIT),
        name="h",
    )(x, g_norm.reshape(1, D), scl.reshape(B, 1, D), shift.reshape(B, 1, D),
      positions.reshape(B, S, 1), inv2, w_f_t)


def _tile_cond(j, idxs):
    cond = None
    run_lo = prev = None
    for t in list(idxs) + [None]:
        if run_lo is None:
            run_lo = prev = t
            continue
        if t is not None and t == prev + 1:
            prev = t
            continue
        c = (j >= run_lo) & (j <= prev)
        cond = c if cond is None else (cond | c)
        run_lo = prev = t
    return cond


def _proj_kernel(h_ref, w_ref, cos_ref, sin_ref, o_ref, *, tiles):
    j = pl.program_id(1)
    tm = h_ref.shape[0]
    rc = min(PROJ_RC, tm)

    def row_chunks(w, epilogue):
        for c in range(tm // rc):
            rows = slice(c * rc, (c + 1) * rc)
            epilogue(rows, _dot_nt(h_ref[rows, :], w))

    shapes = {}
    for t, segs in enumerate(tiles):
        shapes.setdefault(tuple((k, wd) for k, _, wd in segs), []).append(t)

    for shape, idxs in shapes.items():
        @pl.when(_tile_cond(j, idxs))
        def _(shape=shape, idxs=idxs):
            scale = []
            for si in range(len(shape)):
                f = jnp.float32(1.0)
                for sc in sorted({tiles[t][si][1] for t in idxs} - {1.0}):
                    f = jnp.where(_tile_cond(j, [t for t in idxs if tiles[t][si][1] == sc]), jnp.float32(sc), f)
                scale.append(f)

            def epi(rows, acc):
                lo = 0
                for si, (kind, width) in enumerate(shape):
                    cols = slice(lo, lo + width)
                    if kind == "sig":
                        o_ref[rows, cols] = _sigmoid(acc[:, cols]).astype(o_ref.dtype)
                    elif kind == "silu":
                        o_ref[rows, cols] = (acc[:, cols] * _sigmoid(acc[:, cols])).astype(o_ref.dtype)
                    elif kind == "plain":
                        o_ref[rows, cols] = acc[:, cols].astype(o_ref.dtype)
                    else:
                        cos = cos_ref[rows, :] * scale[si]
                        sin = sin_ref[rows, :] * scale[si]
                        for hh in range(lo // HD, (lo + width) // HD):
                            sl = slice(hh * HD, (hh + 1) * HD)
                            o_ref[rows, sl] = _rope(acc[:, sl], cos, sin).astype(o_ref.dtype)
                    lo += width
            row_chunks(w_ref[0].astype(BF16), epi)


def _proj(h2, w_in_t, layer, cos, sin, *, tiles, offsets, tm):
    T, D = h2.shape
    n_tiles = len(tiles)

    def row_off(j):
        off = jnp.int32(offsets[0])
        for t in range(1, n_tiles):
            off = jnp.where(j == t, offsets[t], off)
        return pl.multiple_of(off, 8)

    kern = functools.partial(_proj_kernel, tiles=tuple(tiles))
    return pl.pallas_call(
        kern,
        grid=(T // tm, n_tiles),
        in_specs=[pl.BlockSpec((tm, D), lambda i, j: (i, 0)),
                  pl.BlockSpec((pl.Element(1), pl.Element(PROJ_TN), pl.Element(D)),
                               lambda i, j: (layer, row_off(j), 0)),
                  pl.BlockSpec((tm, HD), lambda i, j: (i, 0)),
                  pl.BlockSpec((tm, HD), lambda i, j: (i, 0))],
        out_specs=pl.BlockSpec((tm, PROJ_TN), lambda i, j: (i, j)),
        out_shape=jax.ShapeDtypeStruct((T, n_tiles * PROJ_TN), BF16),
        compiler_params=pltpu.CompilerParams(
            dimension_semantics=("arbitrary", "arbitrary"), vmem_limit_bytes=VMEM_LIMIT),
        name="proj",
    )(h2, w_in_t, cos, sin)


def _ret_kernel(lg_ref, q_ref, k_ref, v_ref, rg_ref, g_ref, o_ref):
    S = q_ref.shape[1]
    C = RET_CHUNK
    r = lax.broadcasted_iota(jnp.int32, (C, C), 0)
    c = lax.broadcasted_iota(jnp.int32, (C, C), 1)
    diff = (r - c).astype(F32)
    rowf = lax.broadcasted_iota(jnp.int32, (C, HD), 0).astype(F32)
    for hh in range(RET_HPS):
        cols = slice(hh * HD, (hh + 1) * HD)
        lg = lg_ref[pl.program_id(1) * RET_HPS + hh]
        dmask = jnp.where(diff >= 0, jnp.exp(lg * jnp.maximum(diff, 0.0)), 0.0)
        zeta = jnp.exp(lg * (C - 1 - rowf))
        xi = jnp.exp(lg * (rowf + 1.0))
        decay = jnp.exp(jnp.full((HD, HD), lg * C, F32))
        g = g_ref[:, cols]
        state = jnp.zeros((HD, HD), F32)
        for n in range(S // C):
            sl = slice(n * C, (n + 1) * C)
            q = q_ref[0, sl, cols]
            k = k_ref[0, sl, cols]
            v = v_ref[0, sl, cols]
            sc = _dot_nt(q, k) * dmask
            o = _dot(sc.astype(BF16), v)
            if n > 0:
                o = o + _dot(q, state.astype(BF16)) * xi
            if n + 1 < S // C:
                kz = (k.astype(F32) * zeta).astype(BF16)
                state = state * decay + _dot_tn(kz, v)
            mu = jnp.mean(o, -1, keepdims=True)
            d = o - mu
            var = jnp.mean(d * d, -1, keepdims=True)
            on = d * lax.rsqrt(var + EPS)
            o_ref[0, sl, cols] = (on * g * rg_ref[0, sl, cols].astype(F32)).astype(o_ref.dtype)


def _retention(P3, log_gamma, g_ret, *, q_blk, k_blk, v_blk, rg_blk):
    B, S, _ = P3.shape
    H = RET_HEADS
    W = RET_HPS * HD

    def col(off):
        return lambda b, h, lg: (b, 0, off + h)

    return pl.pallas_call(
        _ret_kernel,
        grid_spec=pltpu.PrefetchScalarGridSpec(
            num_scalar_prefetch=1,
            grid=(B, H // RET_HPS),
            in_specs=[pl.BlockSpec((1, S, W), col(q_blk)),
                      pl.BlockSpec((1, S, W), col(k_blk)),
                      pl.BlockSpec((1, S, W), col(v_blk)),
                      pl.BlockSpec((1, S, W), col(rg_blk)),
                      pl.BlockSpec((1, W), lambda b, h, lg: (0, h))],
            out_specs=pl.BlockSpec((1, S, W), lambda b, h, lg: (b, 0, h)),
        ),
        out_shape=jax.ShapeDtypeStruct((B, S, H * HD), BF16),
        compiler_params=pltpu.CompilerParams(vmem_limit_bytes=VMEM_LIMIT),
        name="ret",
    )(log_gamma, P3, P3, P3, P3, g_ret.reshape(1, H * HD))


def _cmp_kernel(x_ref, pe_ref, w1_ref, w2_ref, cos_ref, sin_ref, o_ref):
    half = (CMP_BLOCK // 2) * HD
    M = x_ref.shape[2] // CMP_STRIDE
    x = jnp.concatenate([x_ref[0, 0, pl.ds(p, M, stride=CMP_STRIDE), :] for p in range(CMP_STRIDE)], 1)
    xa = (x + pe_ref[0, 0:1, :]).astype(BF16)
    xb = (x + pe_ref[0, 1:2, :]).astype(BF16)
    a = _dot(xa, w1_ref[0, :half, :])
    b = _dot(xb, w1_ref[0, half:, :])
    nrow = a.shape[0]
    pre = a + pltpu.roll(b, nrow - 1, 0)
    hid = pre * _sigmoid(pre)
    out = _dot(hid.astype(BF16), w2_ref[0])
    roped = _rope(out, cos_ref[0], sin_ref[0])
    is_key = pl.program_id(0) < NSA_GROUPS
    o_ref[0, 0] = jnp.where(is_key, roped, out).astype(o_ref.dtype)


def _compress(xh, pe2, w1s, w2s, cos_c, sin_c):
    NC, B, S, _ = xh.shape
    G = NSA_GROUPS
    M = S // CMP_STRIDE
    W = CMP_STRIDE * HD
    return pl.pallas_call(
        _cmp_kernel,
        grid=(NC, B),
        in_specs=[pl.BlockSpec((1, 1, S, HD), lambda c, b: (c, b, 0, 0)),
                  pl.BlockSpec((1, 2, W), lambda c, b: (c // G, 0, 0)),
                  pl.BlockSpec((1, 2 * W, CMP_HIDDEN), lambda c, b: (c // G, 0, 0)),
                  pl.BlockSpec((1, CMP_HIDDEN, HD), lambda c, b: (c // G, 0, 0)),
                  pl.BlockSpec((1, M, HD), lambda c, b: (b, 0, 0)),
                  pl.BlockSpec((1, M, HD), lambda c, b: (b, 0, 0))],
        out_specs=pl.BlockSpec((1, 1, M, HD), lambda c, b: (b, c, 0, 0)),
        out_shape=jax.ShapeDtypeStruct((B, NC, M, HD), BF16),
        name="cmp",
    )(xh, pe2, w1s, w2s, cos_c, sin_c)


def _nsa_kernel(q_ref, ks_ref, vs_ref, kw_ref, vw_ref, kc_ref, vc_ref, gt_ref, ng_ref, o_ref,
                qa_ref, ow_ref, acc_ref, oc_ref, gs_ref, kaug_ref, vsaug_ref, vwaug_ref, wb_ref):
    TQ = q_ref.shape[1]
    S = ks_ref.shape[1]
    R = NSA_REP
    RT = R * TQ
    WSPAN = WINDOW + TQ
    NSEL = S // SEL_BLOCK
    qi = pl.program_id(2)
    t0 = qi * TQ

    @pl.when(qi == 0)
    def _():
        blk = lax.broadcasted_iota(jnp.int32, (S, HD), 0) >> 6
        onehot = jnp.where(blk == lax.broadcasted_iota(jnp.int32, (S, HD), 1), 1.0, 0.0)
        kaug_ref[:, 0:HD] = ks_ref[0]
        kaug_ref[:, HD:2 * HD] = onehot.astype(BF16)
        ones = jnp.ones((S, HD), BF16)
        vsaug_ref[:, 0:HD] = vs_ref[0]
        vsaug_ref[:, HD:2 * HD] = ones
        vwaug_ref[:, 0:HD] = vw_ref[0]
        vwaug_ref[:, HD:2 * HD] = ones
        r = lax.broadcasted_iota(jnp.int32, (TQ, WSPAN), 0)
        c = lax.broadcasted_iota(jnp.int32, (TQ, WSPAN), 1)
        wb_ref[0] = jnp.where(c <= r, 0.0, NEG)
        wb_ref[1] = jnp.where(c <= r + TQ, 0.0, NEG)
        wb_ref[2] = jnp.where((c > r) & (c <= r + WINDOW), 0.0, NEG)

    for h in range(R):
        qa_ref[h * TQ:(h + 1) * TQ, 0:HD] = q_ref[0, :, h * HD:(h + 1) * HD]
    q4 = qa_ref[:, 0:HD]

    offw = pl.multiple_of(jnp.maximum(t0 - WINDOW, 0), TQ)
    kwin = kw_ref[0, pl.ds(offw, WSPAN), :]
    vwin = vwaug_ref[pl.ds(offw, WSPAN), :]
    wbias = wb_ref[jnp.where(t0 >= WINDOW, 2, jnp.where(t0 > 0, 1, 0))]

    def win_scores(h):
        return _dot_nt(q_ref[0, :, h * HD:(h + 1) * HD], kwin) + wbias

    gt = _sigmoid(gt_ref[0])

    def gate(h, branch):
        c = 3 * h + branch
        return jnp.broadcast_to(gt[:, c:c + 1], (TQ, HD))

    def win_finish(h, sw):
        pw = jnp.exp2(sw - jnp.max(sw, -1, keepdims=True))
        ow = _dot(pw.astype(BF16), vwin)
        ow_ref[h * TQ:(h + 1) * TQ, :] = gate(h, 2) * (ow[:, 0:HD] / ow[:, HD:2 * HD])

    for h in range(R):
        gs_ref[h * TQ:(h + 1) * TQ, :] = gate(h, 1)
    sw0 = win_scores(0)

    s = _dot_nt(q4, kc_ref[0, 0])
    trow = t0 + (lax.broadcasted_iota(jnp.int32, (RT, HD), 0) & (TQ - 1))
    ncol = lax.broadcasted_iota(jnp.int32, (RT, HD), 1)
    s = jnp.where(ncol * CMP_STRIDE + (CMP_BLOCK - 1) <= trow, s, NEG)
    mx = jnp.max(s, -1, keepdims=True)
    p = jnp.exp2(s - mx)
    tcol = t0 + (lax.broadcasted_iota(jnp.int32, (RT, 1), 0) & (TQ - 1))
    any_valid = jnp.where(tcol >= CMP_BLOCK - 1, 1.0, 0.0)
    p = p * (any_valid / jnp.sum(p, -1, keepdims=True))
    sw1 = win_scores(1)
    win_finish(0, sw0)
    o_cmp = _dot(p.astype(BF16), vc_ref[0, 0])
    for h in range(R):
        oc_ref[h * TQ:(h + 1) * TQ, :] = gate(h, 0) * o_cmp[h * TQ:(h + 1) * TQ]

    psum = p[0:TQ]
    for h in range(1, R):
        psum = psum + p[h * TQ:(h + 1) * TQ]
    p_hi = psum.astype(BF16)
    rem = psum - p_hi.astype(F32)
    p_mid = rem.astype(BF16)
    p_lo = (rem - p_mid.astype(F32)).astype(BF16)
    jj = lax.broadcasted_iota(jnp.int32, (NSEL, HD), 0)
    nn = lax.broadcasted_iota(jnp.int32, (NSEL, HD), 1)
    ov = jnp.maximum(jnp.minimum(nn * CMP_STRIDE + CMP_BLOCK, (jj + 1) * SEL_BLOCK)
                     - jnp.maximum(nn * CMP_STRIDE, jj * SEL_BLOCK), 0).astype(F32) * (1.0 / CMP_BLOCK)
    ov = ov.astype(BF16)
    imp = (_dot_nt(ov, p_lo) + _dot_nt(ov, p_mid)) + _dot_nt(ov, p_hi)
    sw2 = win_scores(2)
    win_finish(1, sw1)

    j = lax.broadcasted_iota(jnp.int32, (NSEL, TQ), 0)
    tb = (t0 + lax.broadcasted_iota(jnp.int32, (NSEL, TQ), 1)) >> 6
    forced = (j == 0) | (j == tb) | (j == tb - 1)
    causal = j <= tb
    sc = jnp.where(causal, imp + jnp.where(forced, FORCE_BONUS, 0.0), NEG)
    rank = jnp.zeros((NSEL, TQ), F32)
    for i in range(NSEL):
        si = sc[i:i + 1, :]
        rank = rank + jnp.where(j > i, jnp.where(si >= sc, 1.0, 0.0), jnp.where(si > sc, 1.0, 0.0))
    sel = jnp.where((rank < SEL_TOPN) & causal, 1.0, 0.0)
    sw3 = win_scores(3)
    win_finish(2, sw2)

    selpad = jnp.concatenate([sel, jnp.zeros((HD - NSEL, TQ), F32)], 0).astype(BF16)
    eye = jnp.where(lax.broadcasted_iota(jnp.int32, (TQ, TQ), 0)
                    == lax.broadcasted_iota(jnp.int32, (TQ, TQ), 1), 1.0, 0.0).astype(BF16)
    sel_t = _dot_nt(eye, selpad)
    lane = lax.broadcasted_iota(jnp.int32, (TQ, HD), 1)
    bias = jnp.where((lane < NSEL) & (sel_t < 0.5), NEG, 0.0).astype(BF16)
    for h in range(R):
        qa_ref[h * TQ:(h + 1) * TQ, HD:2 * HD] = bias
    win_finish(3, sw3)

    for v in range(S // TQ):
        @pl.when(qi == v)
        def _(v=v):
            past = v * TQ
            for h in range(R):
                rows = slice(h * TQ, (h + 1) * TQ)
                qh = qa_ref[rows, :]
                sd = _dot_nt(qh, kaug_ref[past:past + TQ, :]) + wb_ref[0, :, 0:TQ]
                mx = jnp.max(sd, -1, keepdims=True)
                if past:
                    sp = _dot_nt(qh, kaug_ref[0:past, :])
                    mx = jnp.maximum(mx, jnp.max(sp, -1, keepdims=True))
                pv = _dot(jnp.exp2(sd - mx).astype(BF16), vsaug_ref[past:past + TQ, :])
                if past:
                    pv = pv + _dot(jnp.exp2(sp - mx).astype(BF16), vsaug_ref[0:past, :])
                acc_ref[rows, :] = pv

    for h in range(R):
        rows = slice(h * TQ, (h + 1) * TQ)
        o = (oc_ref[rows, :] + gs_ref[rows, :] * (acc_ref[rows, 0:HD] / acc_ref[rows, HD:2 * HD])
             + ow_ref[rows, :])
        o_ref[0, :, h * HD:(h + 1) * HD] = (o * ng_ref[0, :, h * HD:(h + 1) * HD].astype(F32)).astype(o_ref.dtype)


def _nsa(P3, F3, kcv, *, q_blk4, ks_blk, vs_blk, kw_blk, vw_blk, ng_blk4, gt_blk):
    B, S, _ = P3.shape
    G, R, TQ = NSA_GROUPS, NSA_REP, NSA_TQ
    assert WINDOW in (TQ, 2 * TQ) and SEL_BLOCK == 64 and S % TQ == 0 and S >= WINDOW + TQ
    M = kcv.shape[2]
    kv_spec = lambda blk: pl.BlockSpec((1, S, HD), lambda b, g, i: (b, 0, blk + g))
    return pl.pallas_call(
        _nsa_kernel,
        grid=(B, G, S // TQ),
        in_specs=[pl.BlockSpec((1, TQ, R * HD), lambda b, g, i: (b, i, q_blk4 + g)),
                  kv_spec(ks_blk), kv_spec(vs_blk), kv_spec(kw_blk), kv_spec(vw_blk),
                  pl.BlockSpec((1, 1, M, HD), lambda b, g, i: (b, g, 0, 0)),
                  pl.BlockSpec((1, 1, M, HD), lambda b, g, i: (b, G + g, 0, 0)),
                  pl.BlockSpec((1, TQ, LANES), lambda b, g, i: (b, i, gt_blk + g)),
                  pl.BlockSpec((1, TQ, R * HD), lambda b, g, i: (b, i, ng_blk4 + g))],
        out_specs=pl.BlockSpec((1, TQ, R * HD), lambda b, g, i: (b, i, g)),
        out_shape=jax.ShapeDtypeStruct((B, S, NSA_HEADS * HD), BF16),
        scratch_shapes=[pltpu.VMEM((R * TQ, 2 * HD), BF16),
                        pltpu.VMEM((R * TQ, HD), F32),
                        pltpu.VMEM((R * TQ, 2 * HD), F32),
                        pltpu.VMEM((R * TQ, HD), F32),
                        pltpu.VMEM((R * TQ, HD), F32),
                        pltpu.VMEM((S, 2 * HD), BF16),
                        pltpu.VMEM((S, 2 * HD), BF16),
                        pltpu.VMEM((S, 2 * HD), BF16),
                        pltpu.VMEM((3, TQ, WINDOW + TQ), F32)],
        compiler_params=pltpu.CompilerParams(
            dimension_semantics=("arbitrary", "arbitrary", "arbitrary"), vmem_limit_bytes=VMEM_LIMIT),
        name="nsa",
    )(P3, P3, P3, P3, P3, kcv, kcv, F3, P3)


def _up_kernel(yr_ref, yn_ref, wr_ref, wn_ref, sa_ref, sb_ref, o_ref):
    a = _dot(yr_ref[...], wr_ref[...])
    b = _dot(yn_ref[...], wn_ref[...])
    o_ref[...] = (sa_ref[...].astype(F32) * a + sb_ref[...].astype(F32) * b).astype(o_ref.dtype)


def _up(y_ret, y_nsa, w_ur, w_un, P2, *, tm):
    T, W = y_ret.shape
    D = w_ur.shape[1]
    return pl.pallas_call(
        _up_kernel,
        grid=(T // tm,),
        in_specs=[pl.BlockSpec((tm, W), lambda i: (i, 0)),
                  pl.BlockSpec((tm, W), lambda i: (i, 0)),
                  pl.BlockSpec((W, D), lambda i: (0, 0)),
                  pl.BlockSpec((W, D), lambda i: (0, 0)),
                  pl.BlockSpec((tm, D), lambda i: (i, 0)),
                  pl.BlockSpec((tm, D), lambda i: (i, 1))],
        out_specs=pl.BlockSpec((tm, D), lambda i: (i, 0)),
        out_shape=jax.ShapeDtypeStruct((T, D), BF16),
        compiler_params=pltpu.CompilerParams(vmem_limit_bytes=VMEM_LIMIT),
        name="up",
    )(y_ret, y_nsa, w_ur, w_un, P2, P2)


def _out_kernel(m_ref, w_ref, x_ref, gate_ref, g_ref, o_ref, *, final_norm):
    y = x_ref[0] + gate_ref[0] * _dot(m_ref[0], w_ref[...])
    if final_norm:
        ms = jnp.mean(y * y, -1, keepdims=True)
        y = y * lax.rsqrt(ms + EPS) * g_ref[...]
    o_ref[0] = y


def _out(merged3, w_out, x, gate, g_final, *, ts, final_norm):
    B, S, D = x.shape
    return pl.pallas_call(
        functools.partial(_out_kernel, final_norm=final_norm),
        grid=(B, S // ts),
        in_specs=[pl.BlockSpec((1, ts, D), lambda b, i: (b, i, 0)),
                  pl.BlockSpec((D, D), lambda b, i: (0, 0)),
                  pl.BlockSpec((1, ts, D), lambda b, i: (b, i, 0)),
                  pl.BlockSpec((1, 1, D), lambda b, i: (b, 0, 0)),
                  pl.BlockSpec((1, D), lambda b, i: (0, 0))],
        out_specs=pl.BlockSpec((1, ts, D), lambda b, i: (b, i, 0)),
        out_shape=jax.ShapeDtypeStruct((B, S, D), F32),
        compiler_params=pltpu.CompilerParams(vmem_limit_bytes=VMEM_LIMIT),
        name="out",
    )(merged3, w_out, x, gate.reshape(B, 1, D), g_final.reshape(1, D))


def kernel(x, c, positions, w_ada, b_ada, g_norm, w_in, g_ret, w_ck1, w_ck2, pe_ck,
           w_cv1, w_cv2, pe_cv, w_up_ret, w_up_nsa, w_out, g_final):
    B, S, D = x.shape
    T = B * S
    depth = w_ada.shape[0]
    G = NSA_GROUPS
    RW = RET_HEADS * HD
    NW = NSA_HEADS * HD
    KVW = G * HD
    cols = (("r_q", RW, "rope", 1.0), ("r_k", RW, "rope", HD ** -0.5), ("r_v", RW, "plain", 1.0),
            ("r_g", RW, "silu", 1.0), ("n_q", NW, "rope", HD ** -0.5 * LOG2E),
            ("n_kc", KVW, "f32", 1.0), ("n_vc", KVW, "f32", 1.0),
            ("n_ks", KVW, "rope", 1.0), ("n_vs", KVW, "plain", 1.0),
            ("n_kw", KVW, "rope", 1.0), ("n_vw", KVW, "plain", 1.0), ("n_g", NW, "silu", 1.0),
            ("n_bg", NSA_HEADS * 3, "f32", 1.0), ("m_a", D, "sig", 1.0), ("m_b", D, "sig", 1.0))
    start = {}
    off = 0
    for nm, wd, _, _ in cols:
        start[nm] = off
        off += wd

    runs = [(start["m_a"], 2 * D)]
    for nm, wd, kind, _ in cols:
        if kind in ("f32", "sig"):
            continue
        if runs[-1][0] + runs[-1][1] == start[nm] and len(runs) > 1:
            runs[-1] = (runs[-1][0], runs[-1][1] + wd)
        else:
            runs.append((start[nm], wd))
    tiles, offsets, blk = [], [], {}
    for lo, length in runs:
        assert length % PROJ_TN == 0 and lo % 8 == 0
        for t0 in range(lo, lo + length, PROJ_TN):
            segs = []
            for nm, wd, kind, scale in cols:
                a0, a1 = max(start[nm], t0), min(start[nm] + wd, t0 + PROJ_TN)
                if a0 >= a1:
                    continue
                assert (a0 - t0) % LANES == 0 and (a1 - a0) % LANES == 0
                if a0 == start[nm]:
                    blk[nm] = (len(tiles) * PROJ_TN + a0 - t0) // LANES
                if segs and segs[-1][0] == kind and segs[-1][1] == scale:
                    segs[-1] = (kind, scale, segs[-1][2] + a1 - a0)
                else:
                    segs.append((kind, scale, a1 - a0))
            tiles.append(tuple(segs))
            offsets.append(t0)

    inv = jnp.exp(jnp.arange(0, HD, 2, dtype=F32) * (-math.log(ROPE_THETA) / HD))
    inv2 = jnp.concatenate([inv, inv]).reshape(1, HD)
    cend = jnp.arange(S // CMP_STRIDE) * CMP_STRIDE + (CMP_BLOCK - 1)
    cend = jnp.minimum(cend, S - 1)
    log_gamma = jnp.log1p(-jnp.exp2(-5.0 - jnp.arange(RET_HEADS, dtype=F32)))
    w_in_t = jnp.swapaxes(w_in, 1, 2)

    for l in range(depth):
        wt = w_in_t[l]
        bg = wt[start["n_bg"]:start["n_bg"] + NSA_HEADS * 3].reshape(G, NSA_REP * 3, D)
        bg = jnp.pad(bg, ((0, 0), (0, LANES - NSA_REP * 3), (0, 0))).reshape(G * LANES, D)
        w_f = jnp.concatenate([wt[start["n_kc"]:start["n_kc"] + 2 * KVW], bg], 0)

        mod = _mod(c, w_ada[l], b_ada[l])
        shift, scl, gate = mod[:, :D], mod[:, D:2 * D], mod[:, 2 * D:]
        h3, cos3, sin3, xh, gts = _h(x, g_norm[l], scl, shift, positions, inv2, w_f, n_kv=2 * G)
        h2, cos, sin = h3.reshape(T, D), cos3.reshape(T, HD), sin3.reshape(T, HD)
        cos_c, sin_c = cos3[:, cend], sin3[:, cend]

        P2 = _proj(h2, w_in_t, l, cos, sin, tiles=tiles, offsets=offsets, tm=min(PROJ_TM, T))
        P3 = P2.reshape(B, S, -1)

        hps = RET_HPS
        y_ret = _retention(P3, log_gamma, g_ret[l], q_blk=blk["r_q"] // hps, k_blk=blk["r_k"] // hps,
                           v_blk=blk["r_v"] // hps, rg_blk=blk["r_g"] // hps)

        half = CMP_BLOCK // 2
        pe2 = jnp.stack([pe_ck[l].reshape(2, half * HD), pe_cv[l].reshape(2, half * HD)])
        w1s = jnp.stack([w_ck1[l], w_cv1[l]]).astype(BF16)
        w2s = jnp.stack([w_ck2[l], w_cv2[l]]).astype(BF16)
        kcv = _compress(xh, pe2, w1s, w2s, cos_c, sin_c)

        y_nsa = _nsa(P3, gts, kcv, q_blk4=blk["n_q"] // NSA_REP,
                     ks_blk=blk["n_ks"], vs_blk=blk["n_vs"], kw_blk=blk["n_kw"], vw_blk=blk["n_vw"],
                     ng_blk4=blk["n_g"] // NSA_REP, gt_blk=0)

        merged = _up(y_ret.reshape(T, RW), y_nsa.reshape(T, NW), w_up_ret[l].astype(BF16),
                     w_up_nsa[l].astype(BF16), P2, tm=min(512, T))
        x = _out(merged.reshape(B, S, D), w_out[l].astype(BF16), x, gate, g_final,
                 ts=min(512, S), final_norm=(l + 1 == depth))
    return x
```

```python
import functools
import math

import jax
import jax.numpy as jnp
from jax import lax
from jax.experimental import pallas as pl
from jax.experimental.pallas import tpu as pltpu

F32 = jnp.float32
BF16 = jnp.bfloat16

HD = 128
RET_HEADS = 8
NSA_HEADS = 8
NSA_GROUPS = 2
NSA_REP = NSA_HEADS // NSA_GROUPS
CMP_BLOCK = 32
CMP_STRIDE = 16
CMP_HIDDEN = 256
SEL_BLOCK = 64
SEL_TOPN = 16
WINDOW = 512
FORCE_BONUS = 1.0e4
ROPE_THETA = 10000.0
EPS = 1e-6
NEG = -1.0e30

LANES = 128
PROJ_TN = 1024
PROJ_TM = 2048
PROJ_RC = 512
H_RC = 256
RET_CHUNK = 256
RET_HPS = 4
NSA_TQ = 256
MOD_TN = 512
H_TS = 512
UP_TM = 512
OUT_TS = 512
SEL_SHIFT = SEL_BLOCK.bit_length() - 1
assert SEL_BLOCK == 1 << SEL_SHIFT
LOG2E = math.log2(math.e)
VMEM_LIMIT = 56 * 1024 * 1024


def _sigmoid(v):
    return 1.0 / (1.0 + jnp.exp(-v))


def _dot(a, b):
    return jnp.dot(a, b, preferred_element_type=F32)


def _dot_nt(a, b):
    return lax.dot_general(a, b, (((1,), (1,)), ((), ())), preferred_element_type=F32)


def _dot_tn(a, b):
    return lax.dot_general(a, b, (((0,), (0,)), ((), ())), preferred_element_type=F32)


def _rope(v, cos, sin_signed):
    return v * cos + pltpu.roll(v, HD // 2, 1) * sin_signed


def _mod_kernel(c_ref, w_ref, b_ref, o_ref):
    c = c_ref[...]
    sc = c * _sigmoid(c)
    o_ref[...] = jnp.dot(sc, w_ref[...], preferred_element_type=F32,
                         precision=lax.Precision.HIGHEST) + b_ref[...]


def _mod(c, w_ada, b_ada):
    B, D = c.shape
    N = w_ada.shape[1]
    tn = math.gcd(N, MOD_TN)
    return pl.pallas_call(
        _mod_kernel,
        grid=(N // tn,),
        in_specs=[pl.BlockSpec((B, D), lambda j: (0, 0)),
                  pl.BlockSpec((D, tn), lambda j: (0, j)),
                  pl.BlockSpec((1, tn), lambda j: (0, j))],
        out_specs=pl.BlockSpec((B, tn), lambda j: (0, j)),
        out_shape=jax.ShapeDtypeStruct((B, N), F32),
        name="mod",
    )(c, w_ada, b_ada.reshape(1, N))


def _h_kernel(x_ref, g_ref, scl_ref, sh_ref, pos_ref, inv_ref, wf_ref,
              h_ref, cos_ref, sin_ref, kv_ref, gt_ref, wfb_ref):
    @pl.when((pl.program_id(0) == 0) & (pl.program_id(1) == 0))
    def _():
        wfb_ref[...] = wf_ref[...].astype(BF16)

    n_kv = kv_ref.shape[0]
    ts = x_ref.shape[1]
    rc = min(H_RC, ts)
    gmul = g_ref[...] * (1.0 + scl_ref[0])
    for r in range(ts // rc):
        rows = slice(r * rc, (r + 1) * rc)
        x = x_ref[0, rows, :]
        ms = jnp.mean(x * x, -1, keepdims=True)
        hb = (x * lax.rsqrt(ms + EPS) * gmul + sh_ref[0]).astype(BF16)
        h_ref[0, rows, :] = hb
        acc = _dot_nt(hb, wfb_ref[...])
        for c in range(n_kv):
            kv_ref[c, 0, rows, :] = acc[:, c * HD:(c + 1) * HD]
        gt_ref[0, rows, :] = acc[:, n_kv * HD:]
    ang = pos_ref[0].astype(F32) * inv_ref[...]
    lane = lax.broadcasted_iota(jnp.int32, (1, HD), 1)
    cos_ref[0] = jnp.cos(ang)
    sin_ref[0] = jnp.where(lane < HD // 2, -1.0, 1.0) * jnp.sin(ang)


def _h(x, g_norm, scl, shift, positions, inv2, w_f_t, *, n_kv):
    B, S, D = x.shape
    N = w_f_t.shape[0]
    ts = min(H_TS, S)
    return pl.pallas_call(
        _h_kernel,
        grid=(B, S // ts),
        in_specs=[pl.BlockSpec((1, ts, D), lambda b, i: (b, i, 0)),
                  pl.BlockSpec((1, D), lambda b, i: (0, 0)),
                  pl.BlockSpec((1, 1, D), lambda b, i: (b, 0, 0)),
                  pl.BlockSpec((1, 1, D), lambda b, i: (b, 0, 0)),
                  pl.BlockSpec((1, ts, 1), lambda b, i: (b, i, 0)),
                  pl.BlockSpec((1, HD), lambda b, i: (0, 0)),
                  pl.BlockSpec((N, D), lambda b, i: (0, 0))],
        out_specs=[pl.BlockSpec((1, ts, D), lambda b, i: (b, i, 0)),
                   pl.BlockSpec((1, ts, HD), lambda b, i: (b, i, 0)),
                   pl.BlockSpec((1, ts, HD), lambda b, i: (b, i, 0)),
                   pl.BlockSpec((n_kv, 1, ts, HD), lambda b, i: (0, b, i, 0)),
                   pl.BlockSpec((1, ts, N - n_kv * HD), lambda b, i: (b, i, 0))],
        out_shape=[jax.ShapeDtypeStruct((B, S, D), BF16),
                   jax.ShapeDtypeStruct((B, S, HD), F32),
                   jax.ShapeDtypeStruct((B, S, HD), F32),
                   jax.ShapeDtypeStruct((n_kv, B, S, HD), F32),
                   jax.ShapeDtypeStruct((B, S, N - n_kv * HD), F32)],
        scratch_shapes=[pltpu.VMEM((N, D), BF16)],
        compiler_params=pltpu.CompilerParams(
            dimension_semantics=("arbitrary", "arbitrary"), vmem_limit_bytes=VMEM_LIMIT),
        name="h",
    )(x, g_norm.reshape(1, D), scl.reshape(B, 1, D), shift.reshape(B, 1, D),
      positions.reshape(B, S, 1), inv2, w_f_t)


def _tile_cond(j, idxs):
    cond = None
    run_lo = prev = None
    for t in list(idxs) + [None]:
        if run_lo is None:
            run_lo = prev = t
            continue
        if t is not None and t == prev + 1:
            prev = t
            continue
        c = (j >= run_lo) & (j <= prev)
        cond = c if cond is None else (cond | c)
        run_lo = prev = t
    return cond


def _proj_kernel(h_ref, w_ref, cos_ref, sin_ref, o_ref, *, tiles):
    j = pl.program_id(1)
    tm = h_ref.shape[0]
    rc = min(PROJ_RC, tm)

    def row_chunks(w, epilogue):
        for c in range(tm // rc):
            rows = slice(c * rc, (c + 1) * rc)
            epilogue(rows, _dot_nt(h_ref[rows, :], w))

    shapes = {}
    for t, segs in enumerate(tiles):
        shapes.setdefault(tuple((k, wd) for k, _, wd in segs), []).append(t)

    for shape, idxs in shapes.items():
        @pl.when(_tile_cond(j, idxs))
        def _(shape=shape, idxs=idxs):
            scale = []
            for si in range(len(shape)):
                f = jnp.float32(1.0)
                for sc in sorted({tiles[t][si][1] for t in idxs} - {1.0}):
                    f = jnp.where(_tile_cond(j, [t for t in idxs if tiles[t][si][1] == sc]), jnp.float32(sc), f)
                scale.append(f)

            def epi(rows, acc):
                lo = 0
                for si, (kind, width) in enumerate(shape):
                    cols = slice(lo, lo + width)
                    if kind == "sig":
                        o_ref[rows, cols] = _sigmoid(acc[:, cols]).astype(o_ref.dtype)
                    elif kind == "silu":
                        o_ref[rows, cols] = (acc[:, cols] * _sigmoid(acc[:, cols])).astype(o_ref.dtype)
                    elif kind == "plain":
                        o_ref[rows, cols] = acc[:, cols].astype(o_ref.dtype)
                    else:
                        cos = cos_ref[rows, :] * scale[si]
                        sin = sin_ref[rows, :] * scale[si]
                        for hh in range(lo // HD, (lo + width) // HD):
                            sl = slice(hh * HD, (hh + 1) * HD)
                            o_ref[rows, sl] = _rope(acc[:, sl], cos, sin).astype(o_ref.dtype)
                    lo += width
            row_chunks(w_ref[0].astype(BF16), epi)


def _proj(h2, w_in_t, layer, cos, sin, *, tiles, offsets, tm):
    T, D = h2.shape
    n_tiles = len(tiles)

    def row_off(j):
        off = jnp.int32(offsets[0])
        for t in range(1, n_tiles):
            off = jnp.where(j == t, offsets[t], off)
        return pl.multiple_of(off, 8)

    kern = functools.partial(_proj_kernel, tiles=tuple(tiles))
    return pl.pallas_call(
        kern,
        grid=(T // tm, n_tiles),
        in_specs=[pl.BlockSpec((tm, D), lambda i, j: (i, 0)),
                  pl.BlockSpec((pl.Element(1), pl.Element(PROJ_TN), pl.Element(D)),
                               lambda i, j: (layer, row_off(j), 0)),
                  pl.BlockSpec((tm, HD), lambda i, j: (i, 0)),
                  pl.BlockSpec((tm, HD), lambda i, j: (i, 0))],
        out_specs=pl.BlockSpec((tm, PROJ_TN), lambda i, j: (i, j)),
        out_shape=jax.ShapeDtypeStruct((T, n_tiles * PROJ_TN), BF16),
        compiler_params=pltpu.CompilerParams(
            dimension_semantics=("arbitrary", "arbitrary"), vmem_limit_bytes=VMEM_LIMIT),
        name="proj",
    )(h2, w_in_t, cos, sin)


def _ret_kernel(lg_ref, q_ref, k_ref, v_ref, rg_ref, g_ref, o_ref):
    S = q_ref.shape[1]
    C = RET_CHUNK
    r = lax.broadcasted_iota(jnp.int32, (C, C), 0)
    c = lax.broadcasted_iota(jnp.int32, (C, C), 1)
    diff = (r - c).astype(F32)
    rowf = lax.broadcasted_iota(jnp.int32, (C, HD), 0).astype(F32)
    for hh in range(RET_HPS):
        cols = slice(hh * HD, (hh + 1) * HD)
        lg = lg_ref[pl.program_id(1) * RET_HPS + hh]
        dmask = jnp.where(diff >= 0, jnp.exp(lg * jnp.maximum(diff, 0.0)), 0.0)
        zeta = jnp.exp(lg * (C - 1 - rowf))
        xi = jnp.exp(lg * (rowf + 1.0))
        decay = jnp.exp(jnp.full((HD, HD), lg * C, F32))
        g = g_ref[:, cols]
        state = jnp.zeros((HD, HD), F32)
        for n in range(S // C):
            sl = slice(n * C, (n + 1) * C)
            q = q_ref[0, sl, cols]
            k = k_ref[0, sl, cols]
            v = v_ref[0, sl, cols]
            sc = _dot_nt(q, k) * dmask
            o = _dot(sc.astype(BF16), v)
            if n > 0:
                o = o + _dot(q, state.astype(BF16)) * xi
            if n + 1 < S // C:
                kz = (k.astype(F32) * zeta).astype(BF16)
                state = state * decay + _dot_tn(kz, v)
            mu = jnp.mean(o, -1, keepdims=True)
            d = o - mu
            var = jnp.mean(d * d, -1, keepdims=True)
            on = d * lax.rsqrt(var + EPS)
            o_ref[0, sl, cols] = (on * g * rg_ref[0, sl, cols].astype(F32)).astype(o_ref.dtype)


def _retention(P3, log_gamma, g_ret, *, q_blk, k_blk, v_blk, rg_blk):
    B, S, _ = P3.shape
    H = RET_HEADS
    W = RET_HPS * HD

    def col(off):
        return lambda b, h, lg: (b, 0, off + h)

    return pl.pallas_call(
        _ret_kernel,
        grid_spec=pltpu.PrefetchScalarGridSpec(
            num_scalar_prefetch=1,
            grid=(B, H // RET_HPS),
            in_specs=[pl.BlockSpec((1, S, W), col(q_blk)),
                      pl.BlockSpec((1, S, W), col(k_blk)),
                      pl.BlockSpec((1, S, W), col(v_blk)),
                      pl.BlockSpec((1, S, W), col(rg_blk)),
                      pl.BlockSpec((1, W), lambda b, h, lg: (0, h))],
            out_specs=pl.BlockSpec((1, S, W), lambda b, h, lg: (b, 0, h)),
        ),
        out_shape=jax.ShapeDtypeStruct((B, S, H * HD), BF16),
        compiler_params=pltpu.CompilerParams(vmem_limit_bytes=VMEM_LIMIT),
        name="ret",
    )(log_gamma, P3, P3, P3, P3, g_ret.reshape(1, H * HD))


def _cmp_kernel(x_ref, pe_ref, w1_ref, w2_ref, cos_ref, sin_ref, o_ref):
    half = (CMP_BLOCK // 2) * HD
    M = x_ref.shape[2] // CMP_STRIDE
    x = jnp.concatenate([x_ref[0, 0, pl.ds(p, M, stride=CMP_STRIDE), :] for p in range(CMP_STRIDE)], 1)
    xa = (x + pe_ref[0, 0:1, :]).astype(BF16)
    xb = (x + pe_ref[0, 1:2, :]).astype(BF16)
    a = _dot(xa, w1_ref[0, :half, :])
    b = _dot(xb, w1_ref[0, half:, :])
    nrow = a.shape[0]
    pre = a + pltpu.roll(b, nrow - 1, 0)
    hid = pre * _sigmoid(pre)
    out = _dot(hid.astype(BF16), w2_ref[0])
    roped = _rope(out, cos_ref[0], sin_ref[0])
    is_key = pl.program_id(0) < NSA_GROUPS
    o_ref[0, 0] = jnp.where(is_key, roped, out).astype(o_ref.dtype)


def _compress(xh, pe2, w1s, w2s, cos_c, sin_c):
    NC, B, S, _ = xh.shape
    G = NSA_GROUPS
    M = S // CMP_STRIDE
    W = CMP_STRIDE * HD
    return pl.pallas_call(
        _cmp_kernel,
        grid=(NC, B),
        in_specs=[pl.BlockSpec((1, 1, S, HD), lambda c, b: (c, b, 0, 0)),
                  pl.BlockSpec((1, 2, W), lambda c, b: (c // G, 0, 0)),
                  pl.BlockSpec((1, 2 * W, CMP_HIDDEN), lambda c, b: (c // G, 0, 0)),
                  pl.BlockSpec((1, CMP_HIDDEN, HD), lambda c, b: (c // G, 0, 0)),
                  pl.BlockSpec((1, M, HD), lambda c, b: (b, 0, 0)),
                  pl.BlockSpec((1, M, HD), lambda c, b: (b, 0, 0))],
        out_specs=pl.BlockSpec((1, 1, M, HD), lambda c, b: (b, c, 0, 0)),
        out_shape=jax.ShapeDtypeStruct((B, NC, M, HD), BF16),
        name="cmp",
    )(xh, pe2, w1s, w2s, cos_c, sin_c)


def _nsa_kernel(q_ref, ks_ref, vs_ref, kw_ref, vw_ref, kc_ref, vc_ref, gt_ref, ng_ref, o_ref,
                qa_ref, ow_ref, acc_ref, oc_ref, gs_ref, kaug_ref, vsaug_ref, vwaug_ref, wb_ref):
    TQ = q_ref.shape[1]
    S = ks_ref.shape[1]
    R = NSA_REP
    RT = R * TQ
    WSPAN = WINDOW + TQ
    NSEL = S // SEL_BLOCK
    qi = pl.program_id(2)
    t0 = qi * TQ

    @pl.when(qi == 0)
    def _():
        blk = lax.broadcasted_iota(jnp.int32, (S, HD), 0) >> SEL_SHIFT
        onehot = jnp.where(blk == lax.broadcasted_iota(jnp.int32, (S, HD), 1), 1.0, 0.0)
        kaug_ref[:, 0:HD] = ks_ref[0]
        kaug_ref[:, HD:2 * HD] = onehot.astype(BF16)
        ones = jnp.ones((S, HD), BF16)
        vsaug_ref[:, 0:HD] = vs_ref[0]
        vsaug_ref[:, HD:2 * HD] = ones
        vwaug_ref[:, 0:HD] = vw_ref[0]
        vwaug_ref[:, HD:2 * HD] = ones
        r = lax.broadcasted_iota(jnp.int32, (TQ, WSPAN), 0)
        c = lax.broadcasted_iota(jnp.int32, (TQ, WSPAN), 1)
        wb_ref[0] = jnp.where(c <= r, 0.0, NEG)
        wb_ref[1] = jnp.where(c <= r + TQ, 0.0, NEG)
        wb_ref[2] = jnp.where((c > r) & (c <= r + WINDOW), 0.0, NEG)

    for h in range(R):
        qa_ref[h * TQ:(h + 1) * TQ, 0:HD] = q_ref[0, :, h * HD:(h + 1) * HD]
    q4 = qa_ref[:, 0:HD]

    offw = pl.multiple_of(jnp.maximum(t0 - WINDOW, 0), TQ)
    kwin = kw_ref[0, pl.ds(offw, WSPAN), :]
    vwin = vwaug_ref[pl.ds(offw, WSPAN), :]
    wbias = wb_ref[jnp.where(t0 >= WINDOW, 2, jnp.where(t0 > 0, 1, 0))]

    def win_scores(h):
        return _dot_nt(q_ref[0, :, h * HD:(h + 1) * HD], kwin) + wbias

    gt = _sigmoid(gt_ref[0])

    def gate(h, branch):
        c = 3 * h + branch
        return jnp.broadcast_to(gt[:, c:c + 1], (TQ, HD))

    def win_finish(h, sw):
        pw = jnp.exp2(sw - jnp.max(sw, -1, keepdims=True))
        ow = _dot(pw.astype(BF16), vwin)
        ow_ref[h * TQ:(h + 1) * TQ, :] = gate(h, 2) * (ow[:, 0:HD] / ow[:, HD:2 * HD])

    for h in range(R):
        gs_ref[h * TQ:(h + 1) * TQ, :] = gate(h, 1)
    sw0 = win_scores(0)

    s = _dot_nt(q4, kc_ref[0, 0])
    trow = t0 + (lax.broadcasted_iota(jnp.int32, (RT, HD), 0) & (TQ - 1))
    ncol = lax.broadcasted_iota(jnp.int32, (RT, HD), 1)
    s = jnp.where(ncol * CMP_STRIDE + (CMP_BLOCK - 1) <= trow, s, NEG)
    mx = jnp.max(s, -1, keepdims=True)
    p = jnp.exp2(s - mx)
    tcol = t0 + (lax.broadcasted_iota(jnp.int32, (RT, 1), 0) & (TQ - 1))
    any_valid = jnp.where(tcol >= CMP_BLOCK - 1, 1.0, 0.0)
    p = p * (any_valid / jnp.sum(p, -1, keepdims=True))
    sw1 = win_scores(1)
    win_finish(0, sw0)
    o_cmp = _dot(p.astype(BF16), vc_ref[0, 0])
    for h in range(R):
        oc_ref[h * TQ:(h + 1) * TQ, :] = gate(h, 0) * o_cmp[h * TQ:(h + 1) * TQ]

    psum = p[0:TQ]
    for h in range(1, R):
        psum = psum + p[h * TQ:(h + 1) * TQ]
    p_hi = psum.astype(BF16)
    rem = psum - p_hi.astype(F32)
    p_mid = rem.astype(BF16)
    p_lo = (rem - p_mid.astype(F32)).astype(BF16)
    jj = lax.broadcasted_iota(jnp.int32, (NSEL, HD), 0)
    nn = lax.broadcasted_iota(jnp.int32, (NSEL, HD), 1)
    ov = jnp.maximum(jnp.minimum(nn * CMP_STRIDE + CMP_BLOCK, (jj + 1) * SEL_BLOCK)
                     - jnp.maximum(nn * CMP_STRIDE, jj * SEL_BLOCK), 0).astype(F32) * (1.0 / CMP_BLOCK)
    ov = ov.astype(BF16)
    imp = (_dot_nt(ov, p_lo) + _dot_nt(ov, p_mid)) + _dot_nt(ov, p_hi)
    sw2 = win_scores(2)
    win_finish(1, sw1)

    j = lax.broadcasted_iota(jnp.int32, (NSEL, TQ), 0)
    tb = (t0 + lax.broadcasted_iota(jnp.int32, (NSEL, TQ), 1)) >> SEL_SHIFT
    forced = (j == 0) | (j == tb) | (j == tb - 1)
    causal = j <= tb
    sc = jnp.where(causal, imp + jnp.where(forced, FORCE_BONUS, 0.0), NEG)
    rank = jnp.zeros((NSEL, TQ), F32)
    for i in range(NSEL):
        si = sc[i:i + 1, :]
        rank = rank + jnp.where(j > i, jnp.where(si >= sc, 1.0, 0.0), jnp.where(si > sc, 1.0, 0.0))
    sel = jnp.where((rank < SEL_TOPN) & causal, 1.0, 0.0)
    sw3 = win_scores(3)
    win_finish(2, sw2)

    selpad = jnp.concatenate([sel, jnp.zeros((HD - NSEL, TQ), F32)], 0).astype(BF16)
    eye = jnp.where(lax.broadcasted_iota(jnp.int32, (TQ, TQ), 0)
                    == lax.broadcasted_iota(jnp.int32, (TQ, TQ), 1), 1.0, 0.0).astype(BF16)
    sel_t = _dot_nt(eye, selpad)
    lane = lax.broadcasted_iota(jnp.int32, (TQ, HD), 1)
    bias = jnp.where((lane < NSEL) & (sel_t < 0.5), NEG, 0.0).astype(BF16)
    for h in range(R):
        qa_ref[h * TQ:(h + 1) * TQ, HD:2 * HD] = bias
    win_finish(3, sw3)

    for v in range(S // TQ):
        @pl.when(qi == v)
        def _(v=v):
            past = v * TQ
            for h in range(R):
                rows = slice(h * TQ, (h + 1) * TQ)
                qh = qa_ref[rows, :]
                sd = _dot_nt(qh, kaug_ref[past:past + TQ, :]) + wb_ref[0, :, 0:TQ]
                mx = jnp.max(sd, -1, keepdims=True)
                if past:
                    sp = _dot_nt(qh, kaug_ref[0:past, :])
                    mx = jnp.maximum(mx, jnp.max(sp, -1, keepdims=True))
                pv = _dot(jnp.exp2(sd - mx).astype(BF16), vsaug_ref[past:past + TQ, :])
                if past:
                    pv = pv + _dot(jnp.exp2(sp - mx).astype(BF16), vsaug_ref[0:past, :])
                acc_ref[rows, :] = pv

    for h in range(R):
        rows = slice(h * TQ, (h + 1) * TQ)
        o = (oc_ref[rows, :] + gs_ref[rows, :] * (acc_ref[rows, 0:HD] / acc_ref[rows, HD:2 * HD])
             + ow_ref[rows, :])
        o_ref[0, :, h * HD:(h + 1) * HD] = (o * ng_ref[0, :, h * HD:(h + 1) * HD].astype(F32)).astype(o_ref.dtype)


def _nsa(P3, gts, kcv, *, q_blk4, ks_blk, vs_blk, kw_blk, vw_blk, ng_blk4, gt_blk):
    B, S, _ = P3.shape
    G, R, TQ = NSA_GROUPS, NSA_REP, NSA_TQ
    assert WINDOW in (TQ, 2 * TQ) and SEL_BLOCK == 64 and S % TQ == 0 and S >= WINDOW + TQ
    M = kcv.shape[2]
    kv_spec = lambda blk: pl.BlockSpec((1, S, HD), lambda b, g, i: (b, 0, blk + g))
    return pl.pallas_call(
        _nsa_kernel,
        grid=(B, G, S // TQ),
        in_specs=[pl.BlockSpec((1, TQ, R * HD), lambda b, g, i: (b, i, q_blk4 + g)),
                  kv_spec(ks_blk), kv_spec(vs_blk), kv_spec(kw_blk), kv_spec(vw_blk),
                  pl.BlockSpec((1, 1, M, HD), lambda b, g, i: (b, g, 0, 0)),
                  pl.BlockSpec((1, 1, M, HD), lambda b, g, i: (b, G + g, 0, 0)),
                  pl.BlockSpec((1, TQ, LANES), lambda b, g, i: (b, i, gt_blk + g)),
                  pl.BlockSpec((1, TQ, R * HD), lambda b, g, i: (b, i, ng_blk4 + g))],
        out_specs=pl.BlockSpec((1, TQ, R * HD), lambda b, g, i: (b, i, g)),
        out_shape=jax.ShapeDtypeStruct((B, S, NSA_HEADS * HD), BF16),
        scratch_shapes=[pltpu.VMEM((R * TQ, 2 * HD), BF16),
                        pltpu.VMEM((R * TQ, HD), F32),
                        pltpu.VMEM((R * TQ, 2 * HD), F32),
                        pltpu.VMEM((R * TQ, HD), F32),
                        pltpu.VMEM((R * TQ, HD), F32),
                        pltpu.VMEM((S, 2 * HD), BF16),
                        pltpu.VMEM((S, 2 * HD), BF16),
                        pltpu.VMEM((S, 2 * HD), BF16),
                        pltpu.VMEM((3, TQ, WINDOW + TQ), F32)],
        compiler_params=pltpu.CompilerParams(
            dimension_semantics=("arbitrary", "arbitrary", "arbitrary"), vmem_limit_bytes=VMEM_LIMIT),
        name="nsa",
    )(P3, P3, P3, P3, P3, kcv, kcv, gts, P3)


def _up_kernel(yr_ref, yn_ref, wr_ref, wn_ref, sa_ref, sb_ref, o_ref):
    a = _dot(yr_ref[...], wr_ref[...])
    b = _dot(yn_ref[...], wn_ref[...])
    o_ref[...] = (sa_ref[...].astype(F32) * a + sb_ref[...].astype(F32) * b).astype(o_ref.dtype)


def _up(y_ret, y_nsa, w_ur, w_un, P2, *, tm):
    T, W = y_ret.shape
    D = w_ur.shape[1]
    return pl.pallas_call(
        _up_kernel,
        grid=(T // tm,),
        in_specs=[pl.BlockSpec((tm, W), lambda i: (i, 0)),
                  pl.BlockSpec((tm, W), lambda i: (i, 0)),
                  pl.BlockSpec((W, D), lambda i: (0, 0)),
                  pl.BlockSpec((W, D), lambda i: (0, 0)),
                  pl.BlockSpec((tm, D), lambda i: (i, 0)),
                  pl.BlockSpec((tm, D), lambda i: (i, 1))],
        out_specs=pl.BlockSpec((tm, D), lambda i: (i, 0)),
        out_shape=jax.ShapeDtypeStruct((T, D), BF16),
        compiler_params=pltpu.CompilerParams(vmem_limit_bytes=VMEM_LIMIT),
        name="up",
    )(y_ret, y_nsa, w_ur, w_un, P2, P2)


def _out_kernel(m_ref, w_ref, x_ref, gate_ref, g_ref, o_ref, *, final_norm):
    y = x_ref[0] + gate_ref[0] * _dot(m_ref[0], w_ref[...])
    if final_norm:
        ms = jnp.mean(y * y, -1, keepdims=True)
        y = y * lax.rsqrt(ms + EPS) * g_ref[...]
    o_ref[0] = y


def _out(merged3, w_out, x, gate, g_final, *, ts, final_norm):
    B, S, D = x.shape
    return pl.pallas_call(
        functools.partial(_out_kernel, final_norm=final_norm),
        grid=(B, S // ts),
        in_specs=[pl.BlockSpec((1, ts, D), lambda b, i: (b, i, 0)),
                  pl.BlockSpec((D, D), lambda b, i: (0, 0)),
                  pl.BlockSpec((1, ts, D), lambda b, i: (b, i, 0)),
                  pl.BlockSpec((1, 1, D), lambda b, i: (b, 0, 0)),
                  pl.BlockSpec((1, D), lambda b, i: (0, 0))],
        out_specs=pl.BlockSpec((1, ts, D), lambda b, i: (b, i, 0)),
        out_shape=jax.ShapeDtypeStruct((B, S, D), F32),
        compiler_params=pltpu.CompilerParams(vmem_limit_bytes=VMEM_LIMIT),
        name="out",
    )(merged3, w_out, x, gate.reshape(B, 1, D), g_final.reshape(1, D))


def kernel(x, c, positions, w_ada, b_ada, g_norm, w_in, g_ret, w_ck1, w_ck2, pe_ck,
           w_cv1, w_cv2, pe_cv, w_up_ret, w_up_nsa, w_out, g_final):
    B, S, D = x.shape
    T = B * S
    depth = w_ada.shape[0]
    G = NSA_GROUPS
    RW = RET_HEADS * HD
    NW = NSA_HEADS * HD
    KVW = G * HD
    cols = (("r_q", RW, "rope", 1.0), ("r_k", RW, "rope", HD ** -0.5), ("r_v", RW, "plain", 1.0),
            ("r_g", RW, "silu", 1.0), ("n_q", NW, "rope", HD ** -0.5 * LOG2E),
            ("n_kc", KVW, "f32", 1.0), ("n_vc", KVW, "f32", 1.0),
            ("n_ks", KVW, "rope", 1.0), ("n_vs", KVW, "plain", 1.0),
            ("n_kw", KVW, "rope", 1.0), ("n_vw", KVW, "plain", 1.0), ("n_g", NW, "silu", 1.0),
            ("n_bg", NSA_HEADS * 3, "f32", 1.0), ("m_a", D, "sig", 1.0), ("m_b", D, "sig", 1.0))
    start = {}
    off = 0
    for nm, wd, _, _ in cols:
        start[nm] = off
        off += wd

    runs = [(start["m_a"], 2 * D)]
    for nm, wd, kind, _ in cols:
        if kind in ("f32", "sig"):
            continue
        if runs[-1][0] + runs[-1][1] == start[nm] and len(runs) > 1:
            runs[-1] = (runs[-1][0], runs[-1][1] + wd)
        else:
            runs.append((start[nm], wd))
    tiles, offsets, blk = [], [], {}
    for lo, length in runs:
        assert length % PROJ_TN == 0 and lo % 8 == 0
        for t0 in range(lo, lo + length, PROJ_TN):
            segs = []
            for nm, wd, kind, scale in cols:
                a0, a1 = max(start[nm], t0), min(start[nm] + wd, t0 + PROJ_TN)
                if a0 >= a1:
                    continue
                assert (a0 - t0) % LANES == 0 and (a1 - a0) % LANES == 0
                if a0 == start[nm]:
                    blk[nm] = (len(tiles) * PROJ_TN + a0 - t0) // LANES
                if segs and segs[-1][0] == kind and segs[-1][1] == scale:
                    segs[-1] = (kind, scale, segs[-1][2] + a1 - a0)
                else:
                    segs.append((kind, scale, a1 - a0))
            tiles.append(tuple(segs))
            offsets.append(t0)

    inv = jnp.exp(jnp.arange(0, HD, 2, dtype=F32) * (-math.log(ROPE_THETA) / HD))
    inv2 = jnp.concatenate([inv, inv]).reshape(1, HD)
    cend = jnp.arange(S // CMP_STRIDE) * CMP_STRIDE + (CMP_BLOCK - 1)
    cend = jnp.minimum(cend, S - 1)
    log_gamma = jnp.log1p(-jnp.exp2(-5.0 - jnp.arange(RET_HEADS, dtype=F32)))
    w_in_t = jnp.swapaxes(w_in, 1, 2)

    for l in range(depth):
        wt = w_in_t[l]
        bg = wt[start["n_bg"]:start["n_bg"] + NSA_HEADS * 3].reshape(G, NSA_REP * 3, D)
        bg = jnp.pad(bg, ((0, 0), (0, LANES - NSA_REP * 3), (0, 0))).reshape(G * LANES, D)
        w_f = jnp.concatenate([wt[start["n_kc"]:start["n_kc"] + 2 * KVW], bg], 0)

        mod = _mod(c, w_ada[l], b_ada[l])
        shift, scl, gate = mod[:, :D], mod[:, D:2 * D], mod[:, 2 * D:]
        h3, cos3, sin3, xh, gts = _h(x, g_norm[l], scl, shift, positions, inv2, w_f, n_kv=2 * G)
        h2, cos, sin = h3.reshape(T, D), cos3.reshape(T, HD), sin3.reshape(T, HD)
        cos_c, sin_c = cos3[:, cend], sin3[:, cend]

        P2 = _proj(h2, w_in_t, l, cos, sin, tiles=tiles, offsets=offsets, tm=min(PROJ_TM, T))
        P3 = P2.reshape(B, S, -1)

        hps = RET_HPS
        y_ret = _retention(P3, log_gamma, g_ret[l], q_blk=blk["r_q"] // hps, k_blk=blk["r_k"] // hps,
                           v_blk=blk["r_v"] // hps, rg_blk=blk["r_g"] // hps)

        half = CMP_BLOCK // 2
        pe2 = jnp.stack([pe_ck[l].reshape(2, half * HD), pe_cv[l].reshape(2, half * HD)])
        w1s = jnp.stack([w_ck1[l], w_cv1[l]]).astype(BF16)
        w2s = jnp.stack([w_ck2[l], w_cv2[l]]).astype(BF16)
        kcv = _compress(xh, pe2, w1s, w2s, cos_c, sin_c)

        y_nsa = _nsa(P3, gts, kcv, q_blk4=blk["n_q"] // NSA_REP,
                     ks_blk=blk["n_ks"], vs_blk=blk["n_vs"], kw_blk=blk["n_kw"], vw_blk=blk["n_vw"],
                     ng_blk4=blk["n_g"] // NSA_REP, gt_blk=0)

        merged = _up(y_ret.reshape(T, RW), y_nsa.reshape(T, NW), w_up_ret[l].astype(BF16),
                     w_up_nsa[l].astype(BF16), P2, tm=min(UP_TM, T))
        x = _out(merged.reshape(B, S, D), w_out[l].astype(BF16), x, gate, g_final,
                 ts=min(OUT_TS, S), final_norm=(l + 1 == depth))
    return x
```

```python
import functools
import math

import jax
import jax.numpy as jnp
from jax import lax
from jax.experimental import pallas as pl
from jax.experimental.pallas import tpu as pltpu

F32 = jnp.float32
BF16 = jnp.bfloat16

HD = 128
RET_HEADS = 8
NSA_HEADS = 8
NSA_GROUPS = 2
NSA_REP = NSA_HEADS // NSA_GROUPS
CMP_BLOCK = 32
CMP_STRIDE = 16
CMP_HIDDEN = 256
SEL_BLOCK = 64
SEL_TOPN = 16
WINDOW = 512
FORCE_BONUS = 1.0e4
ROPE_THETA = 10000.0
EPS = 1e-6
NEG = -1.0e30

LANES = 128
PROJ_TN = 1024
PROJ_TM = 2048
PROJ_RC = 512
H_RC = 256
RET_CHUNK = 256
RET_HPS = 4
NSA_TQ = 256
MOD_TN = 1024
H_TS = 1024
UP_TM = 512
OUT_TS = 512
SEL_SHIFT = SEL_BLOCK.bit_length() - 1
assert SEL_BLOCK == 1 << SEL_SHIFT
LOG2E = math.log2(math.e)
VMEM_LIMIT = 56 * 1024 * 1024


def _sigmoid(v):
    return 1.0 / (1.0 + jnp.exp(-v))


def _dot(a, b):
    return jnp.dot(a, b, preferred_element_type=F32)


def _dot_nt(a, b):
    return lax.dot_general(a, b, (((1,), (1,)), ((), ())), preferred_element_type=F32)


def _dot_tn(a, b):
    return lax.dot_general(a, b, (((0,), (0,)), ((), ())), preferred_element_type=F32)


def _rope(v, cos, sin_signed):
    return v * cos + pltpu.roll(v, HD // 2, 1) * sin_signed


def _mod_kernel(c_ref, w_ref, b_ref, o_ref):
    c = c_ref[...]
    sc = c * _sigmoid(c)
    o_ref[...] = jnp.dot(sc, w_ref[...], preferred_element_type=F32,
                         precision=lax.Precision.HIGHEST) + b_ref[...]


def _mod(c, w_ada, b_ada):
    B, D = c.shape
    N = w_ada.shape[1]
    tn = math.gcd(N, MOD_TN)
    return pl.pallas_call(
        _mod_kernel,
        grid=(N // tn,),
        in_specs=[pl.BlockSpec((B, D), lambda j: (0, 0)),
                  pl.BlockSpec((D, tn), lambda j: (0, j)),
                  pl.BlockSpec((1, tn), lambda j: (0, j))],
        out_specs=pl.BlockSpec((B, tn), lambda j: (0, j)),
        out_shape=jax.ShapeDtypeStruct((B, N), F32),
        name="mod",
    )(c, w_ada, b_ada.reshape(1, N))


def _h_kernel(x_ref, g_ref, scl_ref, sh_ref, pos_ref, inv_ref, wf_ref,
              h_ref, cos_ref, sin_ref, kv_ref, gt_ref, wfb_ref):
    @pl.when((pl.program_id(0) == 0) & (pl.program_id(1) == 0))
    def _():
        wfb_ref[...] = wf_ref[...].astype(BF16)

    n_kv = kv_ref.shape[0]
    ts = x_ref.shape[1]
    rc = min(H_RC, ts)
    gmul = g_ref[...] * (1.0 + scl_ref[0])
    for r in range(ts // rc):
        rows = slice(r * rc, (r + 1) * rc)
        x = x_ref[0, rows, :]
        ms = jnp.mean(x * x, -1, keepdims=True)
        hb = (x * lax.rsqrt(ms + EPS) * gmul + sh_ref[0]).astype(BF16)
        h_ref[0, rows, :] = hb
        acc = _dot_nt(hb, wfb_ref[...])
        for c in range(n_kv):
            kv_ref[c, 0, rows, :] = acc[:, c * HD:(c + 1) * HD]
        gt_ref[0, rows, :] = acc[:, n_kv * HD:]
    ang = pos_ref[0].astype(F32) * inv_ref[...]
    lane = lax.broadcasted_iota(jnp.int32, (1, HD), 1)
    cos_ref[0] = jnp.cos(ang)
    sin_ref[0] = jnp.where(lane < HD // 2, -1.0, 1.0) * jnp.sin(ang)


def _h(x, g_norm, scl, shift, positions, inv2, w_f_t, *, n_kv):
    B, S, D = x.shape
    N = w_f_t.shape[0]
    ts = min(H_TS, S)
    return pl.pallas_call(
        _h_kernel,
        grid=(B, S // ts),
        in_specs=[pl.BlockSpec((1, ts, D), lambda b, i: (b, i, 0)),
                  pl.BlockSpec((1, D), lambda b, i: (0, 0)),
                  pl.BlockSpec((1, 1, D), lambda b, i: (b, 0, 0)),
                  pl.BlockSpec((1, 1, D), lambda b, i: (b, 0, 0)),
                  pl.BlockSpec((1, ts, 1), lambda b, i: (b, i, 0)),
                  pl.BlockSpec((1, HD), lambda b, i: (0, 0)),
                  pl.BlockSpec((N, D), lambda b, i: (0, 0))],
        out_specs=[pl.BlockSpec((1, ts, D), lambda b, i: (b, i, 0)),
                   pl.BlockSpec((1, ts, HD), lambda b, i: (b, i, 0)),
                   pl.BlockSpec((1, ts, HD), lambda b, i: (b, i, 0)),
                   pl.BlockSpec((n_kv, 1, ts, HD), lambda b, i: (0, b, i, 0)),
                   pl.BlockSpec((1, ts, N - n_kv * HD), lambda b, i: (b, i, 0))],
        out_shape=[jax.ShapeDtypeStruct((B, S, D), BF16),
                   jax.ShapeDtypeStruct((B, S, HD), F32),
                   jax.ShapeDtypeStruct((B, S, HD), F32),
                   jax.ShapeDtypeStruct((n_kv, B, S, HD), F32),
                   jax.ShapeDtypeStruct((B, S, N - n_kv * HD), F32)],
        scratch_shapes=[pltpu.VMEM((N, D), BF16)],
        compiler_params=pltpu.CompilerParams(
            dimension_semantics=("arbitrary", "arbitrary"), vmem_limit_bytes=VMEM_LIMIT),
        name="h",
    )(x, g_norm.reshape(1, D), scl.reshape(B, 1, D), shift.reshape(B, 1, D),
      positions.reshape(B, S, 1), inv2, w_f_t)


def _tile_cond(j, idxs):
    cond = None
    run_lo = prev = None
    for t in list(idxs) + [None]:
        if run_lo is None:
            run_lo = prev = t
            continue
        if t is not None and t == prev + 1:
            prev = t
            continue
        c = (j >= run_lo) & (j <= prev)
        cond = c if cond is None else (cond | c)
        run_lo = prev = t
    return cond


def _proj_kernel(h_ref, w_ref, cos_ref, sin_ref, o_ref, *, tiles):
    j = pl.program_id(1)
    tm = h_ref.shape[0]
    rc = min(PROJ_RC, tm)

    def row_chunks(w, epilogue):
        for c in range(tm // rc):
            rows = slice(c * rc, (c + 1) * rc)
            epilogue(rows, _dot_nt(h_ref[rows, :], w))

    shapes = {}
    for t, segs in enumerate(tiles):
        shapes.setdefault(tuple((k, wd) for k, _, wd in segs), []).append(t)

    for shape, idxs in shapes.items():
        @pl.when(_tile_cond(j, idxs))
        def _(shape=shape, idxs=idxs):
            scale = []
            for si in range(len(shape)):
                f = jnp.float32(1.0)
                for sc in sorted({tiles[t][si][1] for t in idxs} - {1.0}):
                    f = jnp.where(_tile_cond(j, [t for t in idxs if tiles[t][si][1] == sc]), jnp.float32(sc), f)
                scale.append(f)

            def epi(rows, acc):
                lo = 0
                for si, (kind, width) in enumerate(shape):
                    cols = slice(lo, lo + width)
                    if kind == "sig":
                        o_ref[rows, cols] = _sigmoid(acc[:, cols]).astype(o_ref.dtype)
                    elif kind == "silu":
                        o_ref[rows, cols] = (acc[:, cols] * _sigmoid(acc[:, cols])).astype(o_ref.dtype)
                    elif kind == "plain":
                        o_ref[rows, cols] = acc[:, cols].astype(o_ref.dtype)
                    else:
                        cos = cos_ref[rows, :] * scale[si]
                        sin = sin_ref[rows, :] * scale[si]
                        for hh in range(lo // HD, (lo + width) // HD):
                            sl = slice(hh * HD, (hh + 1) * HD)
                            o_ref[rows, sl] = _rope(acc[:, sl], cos, sin).astype(o_ref.dtype)
                    lo += width
            row_chunks(w_ref[0].astype(BF16), epi)


def _proj(h2, w_in_t, layer, cos, sin, *, tiles, offsets, tm):
    T, D = h2.shape
    n_tiles = len(tiles)

    def row_off(j):
        off = jnp.int32(offsets[0])
        for t in range(1, n_tiles):
            off = jnp.where(j == t, offsets[t], off)
        return pl.multiple_of(off, 8)

    kern = functools.partial(_proj_kernel, tiles=tuple(tiles))
    return pl.pallas_call(
        kern,
        grid=(T // tm, n_tiles),
        in_specs=[pl.BlockSpec((tm, D), lambda i, j: (i, 0)),
                  pl.BlockSpec((pl.Element(1), pl.Element(PROJ_TN), pl.Element(D)),
                               lambda i, j: (layer, row_off(j), 0)),
                  pl.BlockSpec((tm, HD), lambda i, j: (i, 0)),
                  pl.BlockSpec((tm, HD), lambda i, j: (i, 0))],
        out_specs=pl.BlockSpec((tm, PROJ_TN), lambda i, j: (i, j)),
        out_shape=jax.ShapeDtypeStruct((T, n_tiles * PROJ_TN), BF16),
        compiler_params=pltpu.CompilerParams(
            dimension_semantics=("arbitrary", "arbitrary"), vmem_limit_bytes=VMEM_LIMIT),
        name="proj",
    )(h2, w_in_t, cos, sin)


def _ret_kernel(lg_ref, q_ref, k_ref, v_ref, rg_ref, g_ref, o_ref):
    S = q_ref.shape[1]
    C = RET_CHUNK
    r = lax.broadcasted_iota(jnp.int32, (C, C), 0)
    c = lax.broadcasted_iota(jnp.int32, (C, C), 1)
    diff = (r - c).astype(F32)
    rowf = lax.broadcasted_iota(jnp.int32, (C, HD), 0).astype(F32)
    for hh in range(RET_HPS):
        cols = slice(hh * HD, (hh + 1) * HD)
        lg = lg_ref[pl.program_id(1) * RET_HPS + hh]
        dmask = jnp.where(diff >= 0, jnp.exp(lg * jnp.maximum(diff, 0.0)), 0.0)
        zeta = jnp.exp(lg * (C - 1 - rowf))
        xi = jnp.exp(lg * (rowf + 1.0))
        decay = jnp.exp(jnp.full((HD, HD), lg * C, F32))
        g = g_ref[:, cols]
        state = jnp.zeros((HD, HD), F32)
        for n in range(S // C):
            sl = slice(n * C, (n + 1) * C)
            q = q_ref[0, sl, cols]
            k = k_ref[0, sl, cols]
            v = v_ref[0, sl, cols]
            sc = _dot_nt(q, k) * dmask
            o = _dot(sc.astype(BF16), v)
            if n > 0:
                o = o + _dot(q, state.astype(BF16)) * xi
            if n + 1 < S // C:
                kz = (k.astype(F32) * zeta).astype(BF16)
                state = state * decay + _dot_tn(kz, v)
            mu = jnp.mean(o, -1, keepdims=True)
            d = o - mu
            var = jnp.mean(d * d, -1, keepdims=True)
            on = d * lax.rsqrt(var + EPS)
            o_ref[0, sl, cols] = (on * g * rg_ref[0, sl, cols].astype(F32)).astype(o_ref.dtype)


def _retention(P3, log_gamma, g_ret, *, q_blk, k_blk, v_blk, rg_blk):
    B, S, _ = P3.shape
    H = RET_HEADS
    W = RET_HPS * HD

    def col(off):
        return lambda b, h, lg: (b, 0, off + h)

    return pl.pallas_call(
        _ret_kernel,
        grid_spec=pltpu.PrefetchScalarGridSpec(
            num_scalar_prefetch=1,
            grid=(B, H // RET_HPS),
            in_specs=[pl.BlockSpec((1, S, W), col(q_blk)),
                      pl.BlockSpec((1, S, W), col(k_blk)),
                      pl.BlockSpec((1, S, W), col(v_blk)),
                      pl.BlockSpec((1, S, W), col(rg_blk)),
                      pl.BlockSpec((1, W), lambda b, h, lg: (0, h))],
            out_specs=pl.BlockSpec((1, S, W), lambda b, h, lg: (b, 0, h)),
        ),
        out_shape=jax.ShapeDtypeStruct((B, S, H * HD), BF16),
        compiler_params=pltpu.CompilerParams(vmem_limit_bytes=VMEM_LIMIT),
        name="ret",
    )(log_gamma, P3, P3, P3, P3, g_ret.reshape(1, H * HD))


def _cmp_kernel(x_ref, pe_ref, w1_ref, w2_ref, cos_ref, sin_ref, o_ref):
    half = (CMP_BLOCK // 2) * HD
    M = x_ref.shape[2] // CMP_STRIDE
    x = jnp.concatenate([x_ref[0, 0, pl.ds(p, M, stride=CMP_STRIDE), :] for p in range(CMP_STRIDE)], 1)
    xa = (x + pe_ref[0, 0:1, :]).astype(BF16)
    xb = (x + pe_ref[0, 1:2, :]).astype(BF16)
    a = _dot(xa, w1_ref[0, :half, :])
    b = _dot(xb, w1_ref[0, half:, :])
    nrow = a.shape[0]
    pre = a + pltpu.roll(b, nrow - 1, 0)
    hid = pre * _sigmoid(pre)
    out = _dot(hid.astype(BF16), w2_ref[0])
    roped = _rope(out, cos_ref[0], sin_ref[0])
    is_key = pl.program_id(0) < NSA_GROUPS
    o_ref[0, 0] = jnp.where(is_key, roped, out).astype(o_ref.dtype)


def _compress(xh, pe2, w1s, w2s, cos_c, sin_c):
    NC, B, S, _ = xh.shape
    G = NSA_GROUPS
    M = S // CMP_STRIDE
    W = CMP_STRIDE * HD
    return pl.pallas_call(
        _cmp_kernel,
        grid=(NC, B),
        in_specs=[pl.BlockSpec((1, 1, S, HD), lambda c, b: (c, b, 0, 0)),
                  pl.BlockSpec((1, 2, W), lambda c, b: (c // G, 0, 0)),
                  pl.BlockSpec((1, 2 * W, CMP_HIDDEN), lambda c, b: (c // G, 0, 0)),
                  pl.BlockSpec((1, CMP_HIDDEN, HD), lambda c, b: (c // G, 0, 0)),
                  pl.BlockSpec((1, M, HD), lambda c, b: (b, 0, 0)),
                  pl.BlockSpec((1, M, HD), lambda c, b: (b, 0, 0))],
        out_specs=pl.BlockSpec((1, 1, M, HD), lambda c, b: (b, c, 0, 0)),
        out_shape=jax.ShapeDtypeStruct((B, NC, M, HD), BF16),
        name="cmp",
    )(xh, pe2, w1s, w2s, cos_c, sin_c)


def _nsa_kernel(q_ref, ks_ref, vs_ref, kw_ref, vw_ref, kc_ref, vc_ref, gt_ref, ng_ref, o_ref,
                qa_ref, ow_ref, acc_ref, oc_ref, gs_ref, kaug_ref, vsaug_ref, vwaug_ref, wb_ref):
    TQ = q_ref.shape[1]
    S = ks_ref.shape[1]
    R = NSA_REP
    RT = R * TQ
    WSPAN = WINDOW + TQ
    NSEL = S // SEL_BLOCK
    qi = pl.program_id(2)
    t0 = qi * TQ

    @pl.when(qi == 0)
    def _():
        blk = lax.broadcasted_iota(jnp.int32, (S, HD), 0) >> SEL_SHIFT
        onehot = jnp.where(blk == lax.broadcasted_iota(jnp.int32, (S, HD), 1), 1.0, 0.0)
        kaug_ref[:, 0:HD] = ks_ref[0]
        kaug_ref[:, HD:2 * HD] = onehot.astype(BF16)
        ones = jnp.ones((S, HD), BF16)
        vsaug_ref[:, 0:HD] = vs_ref[0]
        vsaug_ref[:, HD:2 * HD] = ones
        vwaug_ref[:, 0:HD] = vw_ref[0]
        vwaug_ref[:, HD:2 * HD] = ones
        r = lax.broadcasted_iota(jnp.int32, (TQ, WSPAN), 0)
        c = lax.broadcasted_iota(jnp.int32, (TQ, WSPAN), 1)
        wb_ref[0] = jnp.where(c <= r, 0.0, NEG)
        wb_ref[1] = jnp.where(c <= r + TQ, 0.0, NEG)
        wb_ref[2] = jnp.where((c > r) & (c <= r + WINDOW), 0.0, NEG)

    for h in range(R):
        qa_ref[h * TQ:(h + 1) * TQ, 0:HD] = q_ref[0, :, h * HD:(h + 1) * HD]
    q4 = qa_ref[:, 0:HD]

    offw = pl.multiple_of(jnp.maximum(t0 - WINDOW, 0), TQ)
    kwin = kw_ref[0, pl.ds(offw, WSPAN), :]
    vwin = vwaug_ref[pl.ds(offw, WSPAN), :]
    wbias = wb_ref[jnp.where(t0 >= WINDOW, 2, jnp.where(t0 > 0, 1, 0))]

    def win_scores(h):
        return _dot_nt(q_ref[0, :, h * HD:(h + 1) * HD], kwin) + wbias

    gt = _sigmoid(gt_ref[0])

    def gate(h, branch):
        c = 3 * h + branch
        return jnp.broadcast_to(gt[:, c:c + 1], (TQ, HD))

    def win_finish(h, sw):
        pw = jnp.exp2(sw - jnp.max(sw, -1, keepdims=True))
        ow = _dot(pw.astype(BF16), vwin)
        ow_ref[h * TQ:(h + 1) * TQ, :] = gate(h, 2) * (ow[:, 0:HD] / ow[:, HD:2 * HD])

    for h in range(R):
        gs_ref[h * TQ:(h + 1) * TQ, :] = gate(h, 1)
    sw0 = win_scores(0)

    s = _dot_nt(q4, kc_ref[0, 0])
    trow = t0 + (lax.broadcasted_iota(jnp.int32, (RT, HD), 0) & (TQ - 1))
    ncol = lax.broadcasted_iota(jnp.int32, (RT, HD), 1)
    s = jnp.where(ncol * CMP_STRIDE + (CMP_BLOCK - 1) <= trow, s, NEG)
    mx = jnp.max(s, -1, keepdims=True)
    p = jnp.exp2(s - mx)
    tcol = t0 + (lax.broadcasted_iota(jnp.int32, (RT, 1), 0) & (TQ - 1))
    any_valid = jnp.where(tcol >= CMP_BLOCK - 1, 1.0, 0.0)
    p = p * (any_valid / jnp.sum(p, -1, keepdims=True))
    sw1 = win_scores(1)
    win_finish(0, sw0)
    o_cmp = _dot(p.astype(BF16), vc_ref[0, 0])
    for h in range(R):
        oc_ref[h * TQ:(h + 1) * TQ, :] = gate(h, 0) * o_cmp[h * TQ:(h + 1) * TQ]

    psum = p[0:TQ]
    for h in range(1, R):
        psum = psum + p[h * TQ:(h + 1) * TQ]
    p_hi = psum.astype(BF16)
    rem = psum - p_hi.astype(F32)
    p_mid = rem.astype(BF16)
    p_lo = (rem - p_mid.astype(F32)).astype(BF16)
    jj = lax.broadcasted_iota(jnp.int32, (NSEL, HD), 0)
    nn = lax.broadcasted_iota(jnp.int32, (NSEL, HD), 1)
    ov = jnp.maximum(jnp.minimum(nn * CMP_STRIDE + CMP_BLOCK, (jj + 1) * SEL_BLOCK)
                     - jnp.maximum(nn * CMP_STRIDE, jj * SEL_BLOCK), 0).astype(F32) * (1.0 / CMP_BLOCK)
    ov = ov.astype(BF16)
    imp = (_dot_nt(ov, p_lo) + _dot_nt(ov, p_mid)) + _dot_nt(ov, p_hi)
    sw2 = win_scores(2)
    win_finish(1, sw1)

    j = lax.broadcasted_iota(jnp.int32, (NSEL, TQ), 0)
    tb = (t0 + lax.broadcasted_iota(jnp.int32, (NSEL, TQ), 1)) >> SEL_SHIFT
    forced = (j == 0) | (j == tb) | (j == tb - 1)
    causal = j <= tb
    sc = jnp.where(causal, imp + jnp.where(forced, FORCE_BONUS, 0.0), NEG)
    rank = jnp.zeros((NSEL, TQ), F32)
    for i in range(NSEL):
        si = sc[i:i + 1, :]
        rank = rank + jnp.where(j > i, jnp.where(si >= sc, 1.0, 0.0), jnp.where(si > sc, 1.0, 0.0))
    sel = jnp.where((rank < SEL_TOPN) & causal, 1.0, 0.0)
    sw3 = win_scores(3)
    win_finish(2, sw2)

    selpad = jnp.concatenate([sel, jnp.zeros((HD - NSEL, TQ), F32)], 0).astype(BF16)
    eye = jnp.where(lax.broadcasted_iota(jnp.int32, (TQ, TQ), 0)
                    == lax.broadcasted_iota(jnp.int32, (TQ, TQ), 1), 1.0, 0.0).astype(BF16)
    sel_t = _dot_nt(eye, selpad)
    lane = lax.broadcasted_iota(jnp.int32, (TQ, HD), 1)
    bias = jnp.where((lane < NSEL) & (sel_t < 0.5), NEG, 0.0).astype(BF16)
    for h in range(R):
        qa_ref[h * TQ:(h + 1) * TQ, HD:2 * HD] = bias
    win_finish(3, sw3)

    for v in range(S // TQ):
        @pl.when(qi == v)
        def _(v=v):
            past = v * TQ
            for h in range(R):
                rows = slice(h * TQ, (h + 1) * TQ)
                qh = qa_ref[rows, :]
                sd = _dot_nt(qh, kaug_ref[past:past + TQ, :]) + wb_ref[0, :, 0:TQ]
                mx = jnp.max(sd, -1, keepdims=True)
                if past:
                    sp = _dot_nt(qh, kaug_ref[0:past, :])
                    mx = jnp.maximum(mx, jnp.max(sp, -1, keepdims=True))
                pv = _dot(jnp.exp2(sd - mx).astype(BF16), vsaug_ref[past:past + TQ, :])
                if past:
                    pv = pv + _dot(jnp.exp2(sp - mx).astype(BF16), vsaug_ref[0:past, :])
                acc_ref[rows, :] = pv

    for h in range(R):
        rows = slice(h * TQ, (h + 1) * TQ)
        o = (oc_ref[rows, :] + gs_ref[rows, :] * (acc_ref[rows, 0:HD] / acc_ref[rows, HD:2 * HD])
             + ow_ref[rows, :])
        o_ref[0, :, h * HD:(h + 1) * HD] = (o * ng_ref[0, :, h * HD:(h + 1) * HD].astype(F32)).astype(o_ref.dtype)


def _nsa(P3, gts, kcv, *, q_blk4, ks_blk, vs_blk, kw_blk, vw_blk, ng_blk4, gt_blk):
    B, S, _ = P3.shape
    G, R, TQ = NSA_GROUPS, NSA_REP, NSA_TQ
    assert WINDOW in (TQ, 2 * TQ) and SEL_BLOCK == 64 and S % TQ == 0 and S >= WINDOW + TQ
    M = kcv.shape[2]
    kv_spec = lambda blk: pl.BlockSpec((1, S, HD), lambda b, g, i: (b, 0, blk + g))
    return pl.pallas_call(
        _nsa_kernel,
        grid=(B, G, S // TQ),
        in_specs=[pl.BlockSpec((1, TQ, R * HD), lambda b, g, i: (b, i, q_blk4 + g)),
                  kv_spec(ks_blk), kv_spec(vs_blk), kv_spec(kw_blk), kv_spec(vw_blk),
                  pl.BlockSpec((1, 1, M, HD), lambda b, g, i: (b, g, 0, 0)),
                  pl.BlockSpec((1, 1, M, HD), lambda b, g, i: (b, G + g, 0, 0)),
                  pl.BlockSpec((1, TQ, LANES), lambda b, g, i: (b, i, gt_blk + g)),
                  pl.BlockSpec((1, TQ, R * HD), lambda b, g, i: (b, i, ng_blk4 + g))],
        out_specs=pl.BlockSpec((1, TQ, R * HD), lambda b, g, i: (b, i, g)),
        out_shape=jax.ShapeDtypeStruct((B, S, NSA_HEADS * HD), BF16),
        scratch_shapes=[pltpu.VMEM((R * TQ, 2 * HD), BF16),
                        pltpu.VMEM((R * TQ, HD), F32),
                        pltpu.VMEM((R * TQ, 2 * HD), F32),
                        pltpu.VMEM((R * TQ, HD), F32),
                        pltpu.VMEM((R * TQ, HD), F32),
                        pltpu.VMEM((S, 2 * HD), BF16),
                        pltpu.VMEM((S, 2 * HD), BF16),
                        pltpu.VMEM((S, 2 * HD), BF16),
                        pltpu.VMEM((3, TQ, WINDOW + TQ), F32)],
        compiler_params=pltpu.CompilerParams(
            dimension_semantics=("arbitrary", "arbitrary", "arbitrary"), vmem_limit_bytes=VMEM_LIMIT),
        name="nsa",
    )(P3, P3, P3, P3, P3, kcv, kcv, gts, P3)


def _up_kernel(yr_ref, yn_ref, wr_ref, wn_ref, sa_ref, sb_ref, o_ref):
    a = _dot(yr_ref[...], wr_ref[...])
    b = _dot(yn_ref[...], wn_ref[...])
    o_ref[...] = (sa_ref[...].astype(F32) * a + sb_ref[...].astype(F32) * b).astype(o_ref.dtype)


def _up(y_ret, y_nsa, w_ur, w_un, P2, *, tm):
    T, W = y_ret.shape
    D = w_ur.shape[1]
    return pl.pallas_call(
        _up_kernel,
        grid=(T // tm,),
        in_specs=[pl.BlockSpec((tm, W), lambda i: (i, 0)),
                  pl.BlockSpec((tm, W), lambda i: (i, 0)),
                  pl.BlockSpec((W, D), lambda i: (0, 0)),
                  pl.BlockSpec((W, D), lambda i: (0, 0)),
                  pl.BlockSpec((tm, D), lambda i: (i, 0)),
                  pl.BlockSpec((tm, D), lambda i: (i, 1))],
        out_specs=pl.BlockSpec((tm, D), lambda i: (i, 0)),
        out_shape=jax.ShapeDtypeStruct((T, D), BF16),
        compiler_params=pltpu.CompilerParams(vmem_limit_bytes=VMEM_LIMIT),
        name="up",
    )(y_ret, y_nsa, w_ur, w_un, P2, P2)


def _out_kernel(m_ref, w_ref, x_ref, gate_ref, g_ref, o_ref, *, final_norm):
    y = x_ref[0] + gate_ref[0] * _dot(m_ref[0], w_ref[...])
    if final_norm:
        ms = jnp.mean(y * y, -1, keepdims=True)
        y = y * lax.rsqrt(ms + EPS) * g_ref[...]
    o_ref[0] = y


def _out(merged3, w_out, x, gate, g_final, *, ts, final_norm):
    B, S, D = x.shape
    return pl.pallas_call(
        functools.partial(_out_kernel, final_norm=final_norm),
        grid=(B, S // ts),
        in_specs=[pl.BlockSpec((1, ts, D), lambda b, i: (b, i, 0)),
                  pl.BlockSpec((D, D), lambda b, i: (0, 0)),
                  pl.BlockSpec((1, ts, D), lambda b, i: (b, i, 0)),
                  pl.BlockSpec((1, 1, D), lambda b, i: (b, 0, 0)),
                  pl.BlockSpec((1, D), lambda b, i: (0, 0))],
        out_specs=pl.BlockSpec((1, ts, D), lambda b, i: (b, i, 0)),
        out_shape=jax.ShapeDtypeStruct((B, S, D), F32),
        compiler_params=pltpu.CompilerParams(vmem_limit_bytes=VMEM_LIMIT),
        name="out",
    )(merged3, w_out, x, gate.reshape(B, 1, D), g_final.reshape(1, D))


def kernel(x, c, positions, w_ada, b_ada, g_norm, w_in, g_ret, w_ck1, w_ck2, pe_ck,
           w_cv1, w_cv2, pe_cv, w_up_ret, w_up_nsa, w_out, g_final):
    B, S, D = x.shape
    T = B * S
    depth = w_ada.shape[0]
    G = NSA_GROUPS
    RW = RET_HEADS * HD
    NW = NSA_HEADS * HD
    KVW = G * HD
    cols = (("r_q", RW, "rope", 1.0), ("r_k", RW, "rope", HD ** -0.5), ("r_v", RW, "plain", 1.0),
            ("r_g", RW, "silu", 1.0), ("n_q", NW, "rope", HD ** -0.5 * LOG2E),
            ("n_kc", KVW, "f32", 1.0), ("n_vc", KVW, "f32", 1.0),
            ("n_ks", KVW, "rope", 1.0), ("n_vs", KVW, "plain", 1.0),
            ("n_kw", KVW, "rope", 1.0), ("n_vw", KVW, "plain", 1.0), ("n_g", NW, "silu", 1.0),
            ("n_bg", NSA_HEADS * 3, "f32", 1.0), ("m_a", D, "sig", 1.0), ("m_b", D, "sig", 1.0))
    start = {}
    off = 0
    for nm, wd, _, _ in cols:
        start[nm] = off
        off += wd

    runs = [(start["m_a"], 2 * D)]
    for nm, wd, kind, _ in cols:
        if kind in ("f32", "sig"):
            continue
        if runs[-1][0] + runs[-1][1] == start[nm] and len(runs) > 1:
            runs[-1] = (runs[-1][0], runs[-1][1] + wd)
        else:
            runs.append((start[nm], wd))
    tiles, offsets, blk = [], [], {}
    for lo, length in runs:
        assert length % PROJ_TN == 0 and lo % 8 == 0
        for t0 in range(lo, lo + length, PROJ_TN):
            segs = []
            for nm, wd, kind, scale in cols:
                a0, a1 = max(start[nm], t0), min(start[nm] + wd, t0 + PROJ_TN)
                if a0 >= a1:
                    continue
                assert (a0 - t0) % LANES == 0 and (a1 - a0) % LANES == 0
                if a0 == start[nm]:
                    blk[nm] = (len(tiles) * PROJ_TN + a0 - t0) // LANES
                if segs and segs[-1][0] == kind and segs[-1][1] == scale:
                    segs[-1] = (kind, scale, segs[-1][2] + a1 - a0)
                else:
                    segs.append((kind, scale, a1 - a0))
            tiles.append(tuple(segs))
            offsets.append(t0)

    inv = jnp.exp(jnp.arange(0, HD, 2, dtype=F32) * (-math.log(ROPE_THETA) / HD))
    inv2 = jnp.concatenate([inv, inv]).reshape(1, HD)
    cend = jnp.arange(S // CMP_STRIDE) * CMP_STRIDE + (CMP_BLOCK - 1)
    cend = jnp.minimum(cend, S - 1)
    log_gamma = jnp.log1p(-jnp.exp2(-5.0 - jnp.arange(RET_HEADS, dtype=F32)))
    w_in_t = jnp.swapaxes(w_in, 1, 2)

    for l in range(depth):
        wt = w_in_t[l]
        bg = wt[start["n_bg"]:start["n_bg"] + NSA_HEADS * 3].reshape(G, NSA_REP * 3, D)
        bg = jnp.pad(bg, ((0, 0), (0, LANES - NSA_REP * 3), (0, 0))).reshape(G * LANES, D)
        w_f = jnp.concatenate([wt[start["n_kc"]:start["n_kc"] + 2 * KVW], bg], 0)

        mod = _mod(c, w_ada[l], b_ada[l])
        shift, scl, gate = mod[:, :D], mod[:, D:2 * D], mod[:, 2 * D:]
        h3, cos3, sin3, xh, gts = _h(x, g_norm[l], scl, shift, positions, inv2, w_f, n_kv=2 * G)
        h2, cos, sin = h3.reshape(T, D), cos3.reshape(T, HD), sin3.reshape(T, HD)
        cos_c, sin_c = cos3[:, cend], sin3[:, cend]

        P2 = _proj(h2, w_in_t, l, cos, sin, tiles=tiles, offsets=offsets, tm=min(PROJ_TM, T))
        P3 = P2.reshape(B, S, -1)

        hps = RET_HPS
        y_ret = _retention(P3, log_gamma, g_ret[l], q_blk=blk["r_q"] // hps, k_blk=blk["r_k"] // hps,
                           v_blk=blk["r_v"] // hps, rg_blk=blk["r_g"] // hps)

        half = CMP_BLOCK // 2
        pe2 = jnp.stack([pe_ck[l].reshape(2, half * HD), pe_cv[l].reshape(2, half * HD)])
        w1s = jnp.stack([w_ck1[l], w_cv1[l]]).astype(BF16)
        w2s = jnp.stack([w_ck2[l], w_cv2[l]]).astype(BF16)
        kcv = _compress(xh, pe2, w1s, w2s, cos_c, sin_c)

        y_nsa = _nsa(P3, gts, kcv, q_blk4=blk["n_q"] // NSA_REP,
                     ks_blk=blk["n_ks"], vs_blk=blk["n_vs"], kw_blk=blk["n_kw"], vw_blk=blk["n_vw"],
                     ng_blk4=blk["n_g"] // NSA_REP, gt_blk=0)

        merged = _up(y_ret.reshape(T, RW), y_nsa.reshape(T, NW), w_up_ret[l].astype(BF16),
                     w_up_nsa[l].astype(BF16), P2, tm=min(UP_TM, T))
        x = _out(merged.reshape(B, S, D), w_out[l].astype(BF16), x, gate, g_final,
                 ts=min(OUT_TS, S), final_norm=(l + 1 == depth))
    return x
```

```python
import functools
import math

import jax
import jax.numpy as jnp
from jax import lax
from jax.experimental import pallas as pl
from jax.experimental.pallas import tpu as pltpu

F32 = jnp.float32
BF16 = jnp.bfloat16

HD = 128
RET_HEADS = 8
NSA_HEADS = 8
NSA_GROUPS = 2
NSA_REP = NSA_HEADS // NSA_GROUPS
CMP_BLOCK = 32
CMP_STRIDE = 16
CMP_HIDDEN = 256
SEL_BLOCK = 64
SEL_TOPN = 16
WINDOW = 512
FORCE_BONUS = 1.0e4
ROPE_THETA = 10000.0
EPS = 1e-6
NEG = -1.0e30

LANES = 128
PROJ_TN = 1024
PROJ_TM = 2048
PROJ_RC = 512
H_RC = 256
RET_CHUNK = 256
RET_HPS = 4
NSA_TQ = 256
MOD_TN = 512
H_TS = 512
UP_TM = 1024
OUT_TS = 512
SEL_SHIFT = SEL_BLOCK.bit_length() - 1
assert SEL_BLOCK == 1 << SEL_SHIFT
LOG2E = math.log2(math.e)
VMEM_LIMIT = 56 * 1024 * 1024


def _sigmoid(v):
    return 1.0 / (1.0 + jnp.exp(-v))


def _dot(a, b):
    return jnp.dot(a, b, preferred_element_type=F32)


def _dot_nt(a, b):
    return lax.dot_general(a, b, (((1,), (1,)), ((), ())), preferred_element_type=F32)


def _dot_tn(a, b):
    return lax.dot_general(a, b, (((0,), (0,)), ((), ())), preferred_element_type=F32)


def _rope(v, cos, sin_signed):
    return v * cos + pltpu.roll(v, HD // 2, 1) * sin_signed


def _mod_kernel(c_ref, w_ref, b_ref, o_ref):
    c = c_ref[...]
    sc = c * _sigmoid(c)
    o_ref[...] = jnp.dot(sc, w_ref[...], preferred_element_type=F32,
                         precision=lax.Precision.HIGHEST) + b_ref[...]


def _mod(c, w_ada, b_ada):
    B, D = c.shape
    N = w_ada.shape[1]
    tn = math.gcd(N, MOD_TN)
    return pl.pallas_call(
        _mod_kernel,
        grid=(N // tn,),
        in_specs=[pl.BlockSpec((B, D), lambda j: (0, 0)),
                  pl.BlockSpec((D, tn), lambda j: (0, j)),
                  pl.BlockSpec((1, tn), lambda j: (0, j))],
        out_specs=pl.BlockSpec((B, tn), lambda j: (0, j)),
        out_shape=jax.ShapeDtypeStruct((B, N), F32),
        name="mod",
    )(c, w_ada, b_ada.reshape(1, N))


def _h_kernel(x_ref, g_ref, scl_ref, sh_ref, pos_ref, inv_ref, wf_ref,
              h_ref, cos_ref, sin_ref, kv_ref, gt_ref, wfb_ref):
    @pl.when((pl.program_id(0) == 0) & (pl.program_id(1) == 0))
    def _():
        wfb_ref[...] = wf_ref[...].astype(BF16)

    n_kv = kv_ref.shape[0]
    ts = x_ref.shape[1]
    rc = min(H_RC, ts)
    gmul = g_ref[...] * (1.0 + scl_ref[0])
    for r in range(ts // rc):
        rows = slice(r * rc, (r + 1) * rc)
        x = x_ref[0, rows, :]
        ms = jnp.mean(x * x, -1, keepdims=True)
        hb = (x * lax.rsqrt(ms + EPS) * gmul + sh_ref[0]).astype(BF16)
        h_ref[0, rows, :] = hb
        acc = _dot_nt(hb, wfb_ref[...])
        for c in range(n_kv):
            kv_ref[c, 0, rows, :] = acc[:, c * HD:(c + 1) * HD]
        gt_ref[0, rows, :] = acc[:, n_kv * HD:]
    ang = pos_ref[0].astype(F32) * inv_ref[...]
    lane = lax.broadcasted_iota(jnp.int32, (1, HD), 1)
    cos_ref[0] = jnp.cos(ang)
    sin_ref[0] = jnp.where(lane < HD // 2, -1.0, 1.0) * jnp.sin(ang)


def _h(x, g_norm, scl, shift, positions, inv2, w_f_t, *, n_kv):
    B, S, D = x.shape
    N = w_f_t.shape[0]
    ts = min(H_TS, S)
    return pl.pallas_call(
        _h_kernel,
        grid=(B, S // ts),
        in_specs=[pl.BlockSpec((1, ts, D), lambda b, i: (b, i, 0)),
                  pl.BlockSpec((1, D), lambda b, i: (0, 0)),
                  pl.BlockSpec((1, 1, D), lambda b, i: (b, 0, 0)),
                  pl.BlockSpec((1, 1, D), lambda b, i: (b, 0, 0)),
                  pl.BlockSpec((1, ts, 1), lambda b, i: (b, i, 0)),
                  pl.BlockSpec((1, HD), lambda b, i: (0, 0)),
                  pl.BlockSpec((N, D), lambda b, i: (0, 0))],
        out_specs=[pl.BlockSpec((1, ts, D), lambda b, i: (b, i, 0)),
                   pl.BlockSpec((1, ts, HD), lambda b, i: (b, i, 0)),
                   pl.BlockSpec((1, ts, HD), lambda b, i: (b, i, 0)),
                   pl.BlockSpec((n_kv, 1, ts, HD), lambda b, i: (0, b, i, 0)),
                   pl.BlockSpec((1, ts, N - n_kv * HD), lambda b, i: (b, i, 0))],
        out_shape=[jax.ShapeDtypeStruct((B, S, D), BF16),
                   jax.ShapeDtypeStruct((B, S, HD), F32),
                   jax.ShapeDtypeStruct((B, S, HD), F32),
                   jax.ShapeDtypeStruct((n_kv, B, S, HD), F32),
                   jax.ShapeDtypeStruct((B, S, N - n_kv * HD), F32)],
        scratch_shapes=[pltpu.VMEM((N, D), BF16)],
        compiler_params=pltpu.CompilerParams(
            dimension_semantics=("arbitrary", "arbitrary"), vmem_limit_bytes=VMEM_LIMIT),
        name="h",
    )(x, g_norm.reshape(1, D), scl.reshape(B, 1, D), shift.reshape(B, 1, D),
      positions.reshape(B, S, 1), inv2, w_f_t)


def _tile_cond(j, idxs):
    cond = None
    run_lo = prev = None
    for t in list(idxs) + [None]:
        if run_lo is None:
            run_lo = prev = t
            continue
        if t is not None and t == prev + 1:
            prev = t
            continue
        c = (j >= run_lo) & (j <= prev)
        cond = c if cond is None else (cond | c)
        run_lo = prev = t
    return cond


def _proj_kernel(h_ref, w_ref, cos_ref, sin_ref, o_ref, *, tiles):
    j = pl.program_id(1)
    tm = h_ref.shape[0]
    rc = min(PROJ_RC, tm)

    def row_chunks(w, epilogue):
        for c in range(tm // rc):
            rows = slice(c * rc, (c + 1) * rc)
            epilogue(rows, _dot_nt(h_ref[rows, :], w))

    shapes = {}
    for t, segs in enumerate(tiles):
        shapes.setdefault(tuple((k, wd) for k, _, wd in segs), []).append(t)

    for shape, idxs in shapes.items():
        @pl.when(_tile_cond(j, idxs))
        def _(shape=shape, idxs=idxs):
            scale = []
            for si in range(len(shape)):
                f = jnp.float32(1.0)
                for sc in sorted({tiles[t][si][1] for t in idxs} - {1.0}):
                    f = jnp.where(_tile_cond(j, [t for t in idxs if tiles[t][si][1] == sc]), jnp.float32(sc), f)
                scale.append(f)

            def epi(rows, acc):
                lo = 0
                for si, (kind, width) in enumerate(shape):
                    cols = slice(lo, lo + width)
                    if kind == "sig":
                        o_ref[rows, cols] = _sigmoid(acc[:, cols]).astype(o_ref.dtype)
                    elif kind == "silu":
                        o_ref[rows, cols] = (acc[:, cols] * _sigmoid(acc[:, cols])).astype(o_ref.dtype)
                    elif kind == "plain":
                        o_ref[rows, cols] = acc[:, cols].astype(o_ref.dtype)
                    else:
                        cos = cos_ref[rows, :] * scale[si]
                        sin = sin_ref[rows, :] * scale[si]
                        for hh in range(lo // HD, (lo + width) // HD):
                            sl = slice(hh * HD, (hh + 1) * HD)
                            o_ref[rows, sl] = _rope(acc[:, sl], cos, sin).astype(o_ref.dtype)
                    lo += width
            row_chunks(w_ref[0].astype(BF16), epi)


def _proj(h2, w_in_t, layer, cos, sin, *, tiles, offsets, tm):
    T, D = h2.shape
    n_tiles = len(tiles)

    def row_off(j):
        off = jnp.int32(offsets[0])
        for t in range(1, n_tiles):
            off = jnp.where(j == t, offsets[t], off)
        return pl.multiple_of(off, 8)

    kern = functools.partial(_proj_kernel, tiles=tuple(tiles))
    return pl.pallas_call(
        kern,
        grid=(T // tm, n_tiles),
        in_specs=[pl.BlockSpec((tm, D), lambda i, j: (i, 0)),
                  pl.BlockSpec((pl.Element(1), pl.Element(PROJ_TN), pl.Element(D)),
                               lambda i, j: (layer, row_off(j), 0)),
                  pl.BlockSpec((tm, HD), lambda i, j: (i, 0)),
                  pl.BlockSpec((tm, HD), lambda i, j: (i, 0))],
        out_specs=pl.BlockSpec((tm, PROJ_TN), lambda i, j: (i, j)),
        out_shape=jax.ShapeDtypeStruct((T, n_tiles * PROJ_TN), BF16),
        compiler_params=pltpu.CompilerParams(
            dimension_semantics=("arbitrary", "arbitrary"), vmem_limit_bytes=VMEM_LIMIT),
        name="proj",
    )(h2, w_in_t, cos, sin)


def _ret_kernel(lg_ref, q_ref, k_ref, v_ref, rg_ref, g_ref, o_ref):
    S = q_ref.shape[1]
    C = RET_CHUNK
    r = lax.broadcasted_iota(jnp.int32, (C, C), 0)
    c = lax.broadcasted_iota(jnp.int32, (C, C), 1)
    diff = (r - c).astype(F32)
    rowf = lax.broadcasted_iota(jnp.int32, (C, HD), 0).astype(F32)
    for hh in range(RET_HPS):
        cols = slice(hh * HD, (hh + 1) * HD)
        lg = lg_ref[pl.program_id(1) * RET_HPS + hh]
        dmask = jnp.where(diff >= 0, jnp.exp(lg * jnp.maximum(diff, 0.0)), 0.0)
        zeta = jnp.exp(lg * (C - 1 - rowf))
        xi = jnp.exp(lg * (rowf + 1.0))
        decay = jnp.exp(jnp.full((HD, HD), lg * C, F32))
        g = g_ref[:, cols]
        state = jnp.zeros((HD, HD), F32)
        for n in range(S // C):
            sl = slice(n * C, (n + 1) * C)
            q = q_ref[0, sl, cols]
            k = k_ref[0, sl, cols]
            v = v_ref[0, sl, cols]
            sc = _dot_nt(q, k) * dmask
            o = _dot(sc.astype(BF16), v)
            if n > 0:
                o = o + _dot(q, state.astype(BF16)) * xi
            if n + 1 < S // C:
                kz = (k.astype(F32) * zeta).astype(BF16)
                state = state * decay + _dot_tn(kz, v)
            mu = jnp.mean(o, -1, keepdims=True)
            d = o - mu
            var = jnp.mean(d * d, -1, keepdims=True)
            on = d * lax.rsqrt(var + EPS)
            o_ref[0, sl, cols] = (on * g * rg_ref[0, sl, cols].astype(F32)).astype(o_ref.dtype)


def _retention(P3, log_gamma, g_ret, *, q_blk, k_blk, v_blk, rg_blk):
    B, S, _ = P3.shape
    H = RET_HEADS
    W = RET_HPS * HD

    def col(off):
        return lambda b, h, lg: (b, 0, off + h)

    return pl.pallas_call(
        _ret_kernel,
        grid_spec=pltpu.PrefetchScalarGridSpec(
            num_scalar_prefetch=1,
            grid=(B, H // RET_HPS),
            in_specs=[pl.BlockSpec((1, S, W), col(q_blk)),
                      pl.BlockSpec((1, S, W), col(k_blk)),
                      pl.BlockSpec((1, S, W), col(v_blk)),
                      pl.BlockSpec((1, S, W), col(rg_blk)),
                      pl.BlockSpec((1, W), lambda b, h, lg: (0, h))],
            out_specs=pl.BlockSpec((1, S, W), lambda b, h, lg: (b, 0, h)),
        ),
        out_shape=jax.ShapeDtypeStruct((B, S, H * HD), BF16),
        compiler_params=pltpu.CompilerParams(vmem_limit_bytes=VMEM_LIMIT),
        name="ret",
    )(log_gamma, P3, P3, P3, P3, g_ret.reshape(1, H * HD))


def _cmp_kernel(x_ref, pe_ref, w1_ref, w2_ref, cos_ref, sin_ref, o_ref):
    half = (CMP_BLOCK // 2) * HD
    M = x_ref.shape[2] // CMP_STRIDE
    x = jnp.concatenate([x_ref[0, 0, pl.ds(p, M, stride=CMP_STRIDE), :] for p in range(CMP_STRIDE)], 1)
    xa = (x + pe_ref[0, 0:1, :]).astype(BF16)
    xb = (x + pe_ref[0, 1:2, :]).astype(BF16)
    a = _dot(xa, w1_ref[0, :half, :])
    b = _dot(xb, w1_ref[0, half:, :])
    nrow = a.shape[0]
    pre = a + pltpu.roll(b, nrow - 1, 0)
    hid = pre * _sigmoid(pre)
    out = _dot(hid.astype(BF16), w2_ref[0])
    roped = _rope(out, cos_ref[0], sin_ref[0])
    is_key = pl.program_id(0) < NSA_GROUPS
    o_ref[0, 0] = jnp.where(is_key, roped, out).astype(o_ref.dtype)


def _compress(xh, pe2, w1s, w2s, cos_c, sin_c):
    NC, B, S, _ = xh.shape
    G = NSA_GROUPS
    M = S // CMP_STRIDE
    W = CMP_STRIDE * HD
    return pl.pallas_call(
        _cmp_kernel,
        grid=(NC, B),
        in_specs=[pl.BlockSpec((1, 1, S, HD), lambda c, b: (c, b, 0, 0)),
                  pl.BlockSpec((1, 2, W), lambda c, b: (c // G, 0, 0)),
                  pl.BlockSpec((1, 2 * W, CMP_HIDDEN), lambda c, b: (c // G, 0, 0)),
                  pl.BlockSpec((1, CMP_HIDDEN, HD), lambda c, b: (c // G, 0, 0)),
                  pl.BlockSpec((1, M, HD), lambda c, b: (b, 0, 0)),
                  pl.BlockSpec((1, M, HD), lambda c, b: (b, 0, 0))],
        out_specs=pl.BlockSpec((1, 1, M, HD), lambda c, b: (b, c, 0, 0)),
        out_shape=jax.ShapeDtypeStruct((B, NC, M, HD), BF16),
        name="cmp",
    )(xh, pe2, w1s, w2s, cos_c, sin_c)


def _nsa_kernel(q_ref, ks_ref, vs_ref, kw_ref, vw_ref, kc_ref, vc_ref, gt_ref, ng_ref, o_ref,
                qa_ref, ow_ref, acc_ref, oc_ref, gs_ref, kaug_ref, vsaug_ref, vwaug_ref, wb_ref):
    TQ = q_ref.shape[1]
    S = ks_ref.shape[1]
    R = NSA_REP
    RT = R * TQ
    WSPAN = WINDOW + TQ
    NSEL = S // SEL_BLOCK
    qi = pl.program_id(2)
    t0 = qi * TQ

    @pl.when(qi == 0)
    def _():
        blk = lax.broadcasted_iota(jnp.int32, (S, HD), 0) >> SEL_SHIFT
        onehot = jnp.where(blk == lax.broadcasted_iota(jnp.int32, (S, HD), 1), 1.0, 0.0)
        kaug_ref[:, 0:HD] = ks_ref[0]
        kaug_ref[:, HD:2 * HD] = onehot.astype(BF16)
        ones = jnp.ones((S, HD), BF16)
        vsaug_ref[:, 0:HD] = vs_ref[0]
        vsaug_ref[:, HD:2 * HD] = ones
        vwaug_ref[:, 0:HD] = vw_ref[0]
        vwaug_ref[:, HD:2 * HD] = ones
        r = lax.broadcasted_iota(jnp.int32, (TQ, WSPAN), 0)
        c = lax.broadcasted_iota(jnp.int32, (TQ, WSPAN), 1)
        wb_ref[0] = jnp.where(c <= r, 0.0, NEG)
        wb_ref[1] = jnp.where(c <= r + TQ, 0.0, NEG)
        wb_ref[2] = jnp.where((c > r) & (c <= r + WINDOW), 0.0, NEG)

    for h in range(R):
        qa_ref[h * TQ:(h + 1) * TQ, 0:HD] = q_ref[0, :, h * HD:(h + 1) * HD]
    q4 = qa_ref[:, 0:HD]

    offw = pl.multiple_of(jnp.maximum(t0 - WINDOW, 0), TQ)
    kwin = kw_ref[0, pl.ds(offw, WSPAN), :]
    vwin = vwaug_ref[pl.ds(offw, WSPAN), :]
    wbias = wb_ref[jnp.where(t0 >= WINDOW, 2, jnp.where(t0 > 0, 1, 0))]

    def win_scores(h):
        return _dot_nt(q_ref[0, :, h * HD:(h + 1) * HD], kwin) + wbias

    gt = _sigmoid(gt_ref[0])

    def gate(h, branch):
        c = 3 * h + branch
        return jnp.broadcast_to(gt[:, c:c + 1], (TQ, HD))

    def win_finish(h, sw):
        pw = jnp.exp2(sw - jnp.max(sw, -1, keepdims=True))
        ow = _dot(pw.astype(BF16), vwin)
        ow_ref[h * TQ:(h + 1) * TQ, :] = gate(h, 2) * (ow[:, 0:HD] / ow[:, HD:2 * HD])

    for h in range(R):
        gs_ref[h * TQ:(h + 1) * TQ, :] = gate(h, 1)
    sw0 = win_scores(0)

    s = _dot_nt(q4, kc_ref[0, 0])
    trow = t0 + (lax.broadcasted_iota(jnp.int32, (RT, HD), 0) & (TQ - 1))
    ncol = lax.broadcasted_iota(jnp.int32, (RT, HD), 1)
    s = jnp.where(ncol * CMP_STRIDE + (CMP_BLOCK - 1) <= trow, s, NEG)
    mx = jnp.max(s, -1, keepdims=True)
    p = jnp.exp2(s - mx)
    tcol = t0 + (lax.broadcasted_iota(jnp.int32, (RT, 1), 0) & (TQ - 1))
    any_valid = jnp.where(tcol >= CMP_BLOCK - 1, 1.0, 0.0)
    p = p * (any_valid / jnp.sum(p, -1, keepdims=True))
    sw1 = win_scores(1)
    win_finish(0, sw0)
    o_cmp = _dot(p.astype(BF16), vc_ref[0, 0])
    for h in range(R):
        oc_ref[h * TQ:(h + 1) * TQ, :] = gate(h, 0) * o_cmp[h * TQ:(h + 1) * TQ]

    psum = p[0:TQ]
    for h in range(1, R):
        psum = psum + p[h * TQ:(h + 1) * TQ]
    p_hi = psum.astype(BF16)
    rem = psum - p_hi.astype(F32)
    p_mid = rem.astype(BF16)
    p_lo = (rem - p_mid.astype(F32)).astype(BF16)
    jj = lax.broadcasted_iota(jnp.int32, (NSEL, HD), 0)
    nn = lax.broadcasted_iota(jnp.int32, (NSEL, HD), 1)
    ov = jnp.maximum(jnp.minimum(nn * CMP_STRIDE + CMP_BLOCK, (jj + 1) * SEL_BLOCK)
                     - jnp.maximum(nn * CMP_STRIDE, jj * SEL_BLOCK), 0).astype(F32) * (1.0 / CMP_BLOCK)
    ov = ov.astype(BF16)
    imp = (_dot_nt(ov, p_lo) + _dot_nt(ov, p_mid)) + _dot_nt(ov, p_hi)
    sw2 = win_scores(2)
    win_finish(1, sw1)

    j = lax.broadcasted_iota(jnp.int32, (NSEL, TQ), 0)
    tb = (t0 + lax.broadcasted_iota(jnp.int32, (NSEL, TQ), 1)) >> SEL_SHIFT
    forced = (j == 0) | (j == tb) | (j == tb - 1)
    causal = j <= tb
    sc = jnp.where(causal, imp + jnp.where(forced, FORCE_BONUS, 0.0), NEG)
    rank = jnp.zeros((NSEL, TQ), F32)
    for i in range(NSEL):
        si = sc[i:i + 1, :]
        rank = rank + jnp.where(j > i, jnp.where(si >= sc, 1.0, 0.0), jnp.where(si > sc, 1.0, 0.0))
    sel = jnp.where((rank < SEL_TOPN) & causal, 1.0, 0.0)
    sw3 = win_scores(3)
    win_finish(2, sw2)

    selpad = jnp.concatenate([sel, jnp.zeros((HD - NSEL, TQ), F32)], 0).astype(BF16)
    eye = jnp.where(lax.broadcasted_iota(jnp.int32, (TQ, TQ), 0)
                    == lax.broadcasted_iota(jnp.int32, (TQ, TQ), 1), 1.0, 0.0).astype(BF16)
    sel_t = _dot_nt(eye, selpad)
    lane = lax.broadcasted_iota(jnp.int32, (TQ, HD), 1)
    bias = jnp.where((lane < NSEL) & (sel_t < 0.5), NEG, 0.0).astype(BF16)
    for h in range(R):
        qa_ref[h * TQ:(h + 1) * TQ, HD:2 * HD] = bias
    win_finish(3, sw3)

    for v in range(S // TQ):
        @pl.when(qi == v)
        def _(v=v):
            past = v * TQ
            for h in range(R):
                rows = slice(h * TQ, (h + 1) * TQ)
                qh = qa_ref[rows, :]
                sd = _dot_nt(qh, kaug_ref[past:past + TQ, :]) + wb_ref[0, :, 0:TQ]
                mx = jnp.max(sd, -1, keepdims=True)
                if past:
                    sp = _dot_nt(qh, kaug_ref[0:past, :])
                    mx = jnp.maximum(mx, jnp.max(sp, -1, keepdims=True))
                pv = _dot(jnp.exp2(sd - mx).astype(BF16), vsaug_ref[past:past + TQ, :])
                if past:
                    pv = pv + _dot(jnp.exp2(sp - mx).astype(BF16), vsaug_ref[0:past, :])
                acc_ref[rows, :] = pv

    for h in range(R):
        rows = slice(h * TQ, (h + 1) * TQ)
        o = (oc_ref[rows, :] + gs_ref[rows, :] * (acc_ref[rows, 0:HD] / acc_ref[rows, HD:2 * HD])
             + ow_ref[rows, :])
        o_ref[0, :, h * HD:(h + 1) * HD] = (o * ng_ref[0, :, h * HD:(h + 1) * HD].astype(F32)).astype(o_ref.dtype)


def _nsa(P3, gts, kcv, *, q_blk4, ks_blk, vs_blk, kw_blk, vw_blk, ng_blk4, gt_blk):
    B, S, _ = P3.shape
    G, R, TQ = NSA_GROUPS, NSA_REP, NSA_TQ
    assert WINDOW in (TQ, 2 * TQ) and SEL_BLOCK == 64 and S % TQ == 0 and S >= WINDOW + TQ
    M = kcv.shape[2]
    kv_spec = lambda blk: pl.BlockSpec((1, S, HD), lambda b, g, i: (b, 0, blk + g))
    return pl.pallas_call(
        _nsa_kernel,
        grid=(B, G, S // TQ),
        in_specs=[pl.BlockSpec((1, TQ, R * HD), lambda b, g, i: (b, i, q_blk4 + g)),
                  kv_spec(ks_blk), kv_spec(vs_blk), kv_spec(kw_blk), kv_spec(vw_blk),
                  pl.BlockSpec((1, 1, M, HD), lambda b, g, i: (b, g, 0, 0)),
                  pl.BlockSpec((1, 1, M, HD), lambda b, g, i: (b, G + g, 0, 0)),
                  pl.BlockSpec((1, TQ, LANES), lambda b, g, i: (b, i, gt_blk + g)),
                  pl.BlockSpec((1, TQ, R * HD), lambda b, g, i: (b, i, ng_blk4 + g))],
        out_specs=pl.BlockSpec((1, TQ, R * HD), lambda b, g, i: (b, i, g)),
        out_shape=jax.ShapeDtypeStruct((B, S, NSA_HEADS * HD), BF16),
        scratch_shapes=[pltpu.VMEM((R * TQ, 2 * HD), BF16),
                        pltpu.VMEM((R * TQ, HD), F32),
                        pltpu.VMEM((R * TQ, 2 * HD), F32),
                        pltpu.VMEM((R * TQ, HD), F32),
                        pltpu.VMEM((R * TQ, HD), F32),
                        pltpu.VMEM((S, 2 * HD), BF16),
                        pltpu.VMEM((S, 2 * HD), BF16),
                        pltpu.VMEM((S, 2 * HD), BF16),
                        pltpu.VMEM((3, TQ, WINDOW + TQ), F32)],
        compiler_params=pltpu.CompilerParams(
            dimension_semantics=("arbitrary", "arbitrary", "arbitrary"), vmem_limit_bytes=VMEM_LIMIT),
        name="nsa",
    )(P3, P3, P3, P3, P3, kcv, kcv, gts, P3)


def _up_kernel(yr_ref, yn_ref, wr_ref, wn_ref, sa_ref, sb_ref, o_ref):
    a = _dot(yr_ref[...], wr_ref[...])
    b = _dot(yn_ref[...], wn_ref[...])
    o_ref[...] = (sa_ref[...].astype(F32) * a + sb_ref[...].astype(F32) * b).astype(o_ref.dtype)


def _up(y_ret, y_nsa, w_ur, w_un, P2, *, tm):
    T, W = y_ret.shape
    D = w_ur.shape[1]
    return pl.pallas_call(
        _up_kernel,
        grid=(T // tm,),
        in_specs=[pl.BlockSpec((tm, W), lambda i: (i, 0)),
                  pl.BlockSpec((tm, W), lambda i: (i, 0)),
                  pl.BlockSpec((W, D), lambda i: (0, 0), pipeline_mode=pl.Buffered(1)),
                  pl.BlockSpec((W, D), lambda i: (0, 0), pipeline_mode=pl.Buffered(1)),
                  pl.BlockSpec((tm, D), lambda i: (i, 0)),
                  pl.BlockSpec((tm, D), lambda i: (i, 1))],
        out_specs=pl.BlockSpec((tm, D), lambda i: (i, 0)),
        out_shape=jax.ShapeDtypeStruct((T, D), BF16),
        compiler_params=pltpu.CompilerParams(vmem_limit_bytes=VMEM_LIMIT),
        name="up",
    )(y_ret, y_nsa, w_ur, w_un, P2, P2)


def _out_kernel(m_ref, w_ref, x_ref, gate_ref, g_ref, o_ref, *, final_norm):
    y = x_ref[0] + gate_ref[0] * _dot(m_ref[0], w_ref[...])
    if final_norm:
        ms = jnp.mean(y * y, -1, keepdims=True)
        y = y * lax.rsqrt(ms + EPS) * g_ref[...]
    o_ref[0] = y


def _out(merged3, w_out, x, gate, g_final, *, ts, final_norm):
    B, S, D = x.shape
    return pl.pallas_call(
        functools.partial(_out_kernel, final_norm=final_norm),
        grid=(B, S // ts),
        in_specs=[pl.BlockSpec((1, ts, D), lambda b, i: (b, i, 0)),
                  pl.BlockSpec((D, D), lambda b, i: (0, 0), pipeline_mode=pl.Buffered(1)),
                  pl.BlockSpec((1, ts, D), lambda b, i: (b, i, 0)),
                  pl.BlockSpec((1, 1, D), lambda b, i: (b, 0, 0)),
                  pl.BlockSpec((1, D), lambda b, i: (0, 0))],
        out_specs=pl.BlockSpec((1, ts, D), lambda b, i: (b, i, 0)),
        out_shape=jax.ShapeDtypeStruct((B, S, D), F32),
        compiler_params=pltpu.CompilerParams(vmem_limit_bytes=VMEM_LIMIT),
        name="out",
    )(merged3, w_out, x, gate.reshape(B, 1, D), g_final.reshape(1, D))


def kernel(x, c, positions, w_ada, b_ada, g_norm, w_in, g_ret, w_ck1, w_ck2, pe_ck,
           w_cv1, w_cv2, pe_cv, w_up_ret, w_up_nsa, w_out, g_final):
    B, S, D = x.shape
    T = B * S
    depth = w_ada.shape[0]
    G = NSA_GROUPS
    RW = RET_HEADS * HD
    NW = NSA_HEADS * HD
    KVW = G * HD
    cols = (("r_q", RW, "rope", 1.0), ("r_k", RW, "rope", HD ** -0.5), ("r_v", RW, "plain", 1.0),
            ("r_g", RW, "silu", 1.0), ("n_q", NW, "rope", HD ** -0.5 * LOG2E),
            ("n_kc", KVW, "f32", 1.0), ("n_vc", KVW, "f32", 1.0),
            ("n_ks", KVW, "rope", 1.0), ("n_vs", KVW, "plain", 1.0),
            ("n_kw", KVW, "rope", 1.0), ("n_vw", KVW, "plain", 1.0), ("n_g", NW, "silu", 1.0),
            ("n_bg", NSA_HEADS * 3, "f32", 1.0), ("m_a", D, "sig", 1.0), ("m_b", D, "sig", 1.0))
    start = {}
    off = 0
    for nm, wd, _, _ in cols:
        start[nm] = off
        off += wd

    runs = [(start["m_a"], 2 * D)]
    for nm, wd, kind, _ in cols:
        if kind in ("f32", "sig"):
            continue
        if runs[-1][0] + runs[-1][1] == start[nm] and len(runs) > 1:
            runs[-1] = (runs[-1][0], runs[-1][1] + wd)
        else:
            runs.append((start[nm], wd))
    tiles, offsets, blk = [], [], {}
    for lo, length in runs:
        assert length % PROJ_TN == 0 and lo % 8 == 0
        for t0 in range(lo, lo + length, PROJ_TN):
            segs = []
            for nm, wd, kind, scale in cols:
                a0, a1 = max(start[nm], t0), min(start[nm] + wd, t0 + PROJ_TN)
                if a0 >= a1:
                    continue
                assert (a0 - t0) % LANES == 0 and (a1 - a0) % LANES == 0
                if a0 == start[nm]:
                    blk[nm] = (len(tiles) * PROJ_TN + a0 - t0) // LANES
                if segs and segs[-1][0] == kind and segs[-1][1] == scale:
                    segs[-1] = (kind, scale, segs[-1][2] + a1 - a0)
                else:
                    segs.append((kind, scale, a1 - a0))
            tiles.append(tuple(segs))
            offsets.append(t0)

    inv = jnp.exp(jnp.arange(0, HD, 2, dtype=F32) * (-math.log(ROPE_THETA) / HD))
    inv2 = jnp.concatenate([inv, inv]).reshape(1, HD)
    cend = jnp.arange(S // CMP_STRIDE) * CMP_STRIDE + (CMP_BLOCK - 1)
    cend = jnp.minimum(cend, S - 1)
    log_gamma = jnp.log1p(-jnp.exp2(-5.0 - jnp.arange(RET_HEADS, dtype=F32)))
    w_in_t = jnp.swapaxes(w_in, 1, 2)

    for l in range(depth):
        wt = w_in_t[l]
        bg = wt[start["n_bg"]:start["n_bg"] + NSA_HEADS * 3].reshape(G, NSA_REP * 3, D)
        bg = jnp.pad(bg, ((0, 0), (0, LANES - NSA_REP * 3), (0, 0))).reshape(G * LANES, D)
        w_f = jnp.concatenate([wt[start["n_kc"]:start["n_kc"] + 2 * KVW], bg], 0)

        mod = _mod(c, w_ada[l], b_ada[l])
        shift, scl, gate = mod[:, :D], mod[:, D:2 * D], mod[:, 2 * D:]
        h3, cos3, sin3, xh, gts = _h(x, g_norm[l], scl, shift, positions, inv2, w_f, n_kv=2 * G)
        h2, cos, sin = h3.reshape(T, D), cos3.reshape(T, HD), sin3.reshape(T, HD)
        cos_c, sin_c = cos3[:, cend], sin3[:, cend]

        P2 = _proj(h2, w_in_t, l, cos, sin, tiles=tiles, offsets=offsets, tm=min(PROJ_TM, T))
        P3 = P2.reshape(B, S, -1)

        hps = RET_HPS
        y_ret = _retention(P3, log_gamma, g_ret[l], q_blk=blk["r_q"] // hps, k_blk=blk["r_k"] // hps,
                           v_blk=blk["r_v"] // hps, rg_blk=blk["r_g"] // hps)

        half = CMP_BLOCK // 2
        pe2 = jnp.stack([pe_ck[l].reshape(2, half * HD), pe_cv[l].reshape(2, half * HD)])
        w1s = jnp.stack([w_ck1[l], w_cv1[l]]).astype(BF16)
        w2s = jnp.stack([w_ck2[l], w_cv2[l]]).astype(BF16)
        kcv = _compress(xh, pe2, w1s, w2s, cos_c, sin_c)

        y_nsa = _nsa(P3, gts, kcv, q_blk4=blk["n_q"] // NSA_REP,
                     ks_blk=blk["n_ks"], vs_blk=blk["n_vs"], kw_blk=blk["n_kw"], vw_blk=blk["n_vw"],
                     ng_blk4=blk["n_g"] // NSA_REP, gt_blk=0)

        merged = _up(y_ret.reshape(T, RW), y_nsa.reshape(T, NW), w_up_ret[l].astype(BF16),
                     w_up_nsa[l].astype(BF16), P2, tm=min(UP_TM, T))
        x = _out(merged.reshape(B, S, D), w_out[l].astype(BF16), x, gate, g_final,
                 ts=min(OUT_TS, S), final_norm=(l + 1 == depth))
    return x
```
